```python
import math
import jax, jax.numpy as jnp
from jax import lax
import numpy as np

D_MODEL = 4096
BATCH = 4
SEQ = 2048
DEPTH = 2
DEC_BATCH = 128
DEC_SEQ = 1
PAST_LEN = 16384
PAGE_SIZE = 128

MIX_WIDTH = 2 * D_MODEL
SSD_WIDTH = MIX_WIDTH // 2
SSD_HEAD_DIM = 64
SSD_HEADS = SSD_WIDTH // SSD_HEAD_DIM
SSD_GROUPS = 8
SSD_HEADS_PER_GROUP = SSD_HEADS // SSD_GROUPS
SSD_STATE = 128
SSD_CONV = 4
SSD_CONV_DIM = SSD_WIDTH + 2 * SSD_GROUPS * SSD_STATE
RWKV_WIDTH = MIX_WIDTH - SSD_WIDTH
RWKV_HEAD_DIM = 64
RWKV_HEADS = RWKV_WIDTH // RWKV_HEAD_DIM
DECAY_LORA = 128
AAA_LORA = 128
SHIFT_DIM = 3 * RWKV_WIDTH + DECAY_LORA + AAA_LORA
AB_IN = SSD_WIDTH + SSD_CONV_DIM + SSD_HEADS + SHIFT_DIM + RWKV_WIDTH
RET_HEADS = 16
RET_QK_DIM = D_MODEL // RET_HEADS
RET_V_DIM = 2 * RET_QK_DIM
RET_QK_WIDTH = RET_HEADS * RET_QK_DIM
RET_WIDTH = RET_HEADS * RET_V_DIM
RET_IN = 2 * RET_QK_WIDTH + 2 * RET_WIDTH
ROPE_BASE = 10000.0
CHUNK = 128
N_AB_LAYERS = (DEPTH + 1) // 2
N_C_LAYERS = DEPTH // 2
ALPHA = (2 * DEPTH) ** 0.25
BETA = (8 * DEPTH) ** -0.25
LN_EPS = 1e-5
RMS_EPS = 1e-5
RWKV_GN_EPS = 64e-5
RET_GN_EPS = 1e-6

kernel_name = 'hybrid_ssd_rwkv7_retention_step'


def _chunk_len(l):
    return CHUNK if l % CHUNK == 0 else l


def _to_chunks(t, L):
    b, l = t.shape[:2]
    return jnp.moveaxis(t.reshape(b, l // L, L, *t.shape[2:]), 1, 0)


def _from_chunks(t):
    c, b, L = t.shape[:3]
    return jnp.moveaxis(t, 0, 1).reshape(b, c * L, *t.shape[3:])


def layer_norm(x, w, b):
    xf = x.astype(jnp.float32)
    mu = jnp.mean(xf, -1, keepdims=True)
    var = jnp.mean(jnp.square(xf - mu), -1, keepdims=True)
    return ((xf - mu) * lax.rsqrt(var + LN_EPS) * w + b).astype(x.dtype)


def head_norm(t, eps):
    mu = jnp.mean(t, -1, keepdims=True)
    var = jnp.mean(jnp.square(t - mu), -1, keepdims=True)
    return (t - mu) * lax.rsqrt(var + eps)


def group_rms_norm(y, groups, w):
    b, l, c = y.shape
    yg = y.reshape(b, l, groups, c // groups)
    yg = yg * lax.rsqrt(jnp.mean(jnp.square(yg), -1, keepdims=True) + RMS_EPS)
    return yg.reshape(b, l, c) * w


def causal_dwconv(u, buf, w, bias):
    l = u.shape[1]
    up = jnp.concatenate([buf.astype(u.dtype), u], axis=1)
    out = bias + sum(up[:, k:k + l] * w[k] for k in range(SSD_CONV))
    return out, up[:, l:]


def rotary(t, pos):
    half = t.shape[-1] // 2
    freq = ROPE_BASE ** (-jnp.arange(half, dtype=jnp.float32) / half)
    ang = pos.astype(jnp.float32)[:, None] * freq[None]
    cos = jnp.cos(ang)[None, :, None]
    sin = jnp.sin(ang)[None, :, None]
    t1, t2 = t[..., :half], t[..., half:]
    return jnp.concatenate([t1 * cos - t2 * sin, t1 * sin + t2 * cos], -1)


def ssd_chunked(xdt, adt, bm, cm, s0):
    L = _chunk_len(xdt.shape[1])
    causal = jnp.tril(jnp.ones((L, L), dtype=bool))[None, :, :, None, None]

    def step(s, inp):
        xc, ac, bc, cc = inp
        cum = jnp.cumsum(ac, axis=1)
        seg = cum[:, :, None] - cum[:, None, :]
        decay = jnp.exp(jnp.where(causal, seg, -jnp.inf))
        cb = jnp.einsum('blgn,bsgn->blsg', cc, bc)
        y = jnp.einsum('blsgr,bsgrp->blgrp', cb[..., None] * decay, xc)
        y = y + jnp.einsum('blgn,bgrpn->blgrp', cc, s) * jnp.exp(cum)[..., None]
        tail = jnp.exp(cum[:, -1:] - cum)
        s = s * jnp.exp(cum[:, -1])[..., None, None] + jnp.einsum('blgn,blgrp->bgrpn', bc, xc * tail[..., None])
        return s, y

    s, ys = lax.scan(step, s0, tuple(_to_chunks(t, L) for t in (xdt, adt, bm, cm)))
    return _from_chunks(ys), s


def wkv7_scan(r, w, k, v, kk, a, s0):
    def step(s, inp):
        rt, wt, kt, vt, kkt, at = inp
        sk = jnp.einsum('bhvk,bhk->bhv', s, kkt)
        s = s * wt[:, :, None, :] - sk[..., None] * (kkt * at)[:, :, None, :] + vt[..., None] * kt[:, :, None, :]
        return s, jnp.einsum('bhvk,bhk->bhv', s, rt)

    xs = tuple(jnp.moveaxis(t, 1, 0) for t in (r, w, k, v, kk, a))
    s, ys = lax.scan(step, s0, xs)
    return jnp.moveaxis(ys, 0, 1), s


def retention_chunked(q, k, v, s0):
    L = _chunk_len(q.shape[1])
    log_g = jnp.log1p(-jnp.exp2(-5.0 - jnp.arange(RET_HEADS, dtype=jnp.float32)))
    i = jnp.arange(L, dtype=jnp.float32)
    rel = (i[:, None] - i[None, :])[None]
    inner = jnp.exp(jnp.where(rel >= 0, rel * log_g[:, None, None], -jnp.inf))
    q_decay = jnp.exp((i[:, None] + 1.0) * log_g[None])
    k_decay = jnp.exp((L - 1.0 - i)[:, None] * log_g[None])
    chunk_decay = jnp.exp(L * log_g)

    def step(s, inp):
        qc, kc, vc = inp
        sc = jnp.einsum('blhd,bshd->bhls', qc, kc) * inner
        y = jnp.einsum('bhls,bshv->blhv', sc, vc) + jnp.einsum('blhd,bhdv->blhv', qc, s) * q_decay[None, :, :, None]
        s = s * chunk_decay[None, :, None, None] + jnp.einsum('blhd,blhv->bhdv', kc * k_decay[None, :, :, None], vc)
        return s, y

    s, ys = lax.scan(step, s0, tuple(_to_chunks(t, L) for t in (q, k, v)))
    return _from_chunks(ys), s


def ab_layer(x, conv_buf, ssm_s, shift_buf, wkv_s, w_in, conv_w, conv_b, dt_bias, a_log, d_skip, norm_w,
             mu, w0, w_up, a0, a_up, k_k, k_a, r_k, lnx_w, lnx_b, w_out, ln_w, ln_b):
    f32 = jnp.float32
    b, l, _ = x.shape
    proj = jnp.einsum('bld,de->ble', x, w_in)
    z, xbc, dt, p, g = jnp.split(proj, np.cumsum([SSD_WIDTH, SSD_CONV_DIM, SSD_HEADS, SHIFT_DIM]).tolist(), axis=-1)

    xbc, new_conv = causal_dwconv(xbc, conv_buf, conv_w, conv_b)
    xbc = jax.nn.silu(xbc).astype(f32)
    xs, bm, cm = jnp.split(xbc, [SSD_WIDTH, SSD_WIDTH + SSD_GROUPS * SSD_STATE], axis=-1)
    xs = xs.reshape(b, l, SSD_GROUPS, SSD_HEADS_PER_GROUP, SSD_HEAD_DIM)
    bm = bm.reshape(b, l, SSD_GROUPS, SSD_STATE)
    cm = cm.reshape(b, l, SSD_GROUPS, SSD_STATE)
    dt = jax.nn.softplus(dt.astype(f32) + dt_bias).reshape(b, l, SSD_GROUPS, SSD_HEADS_PER_GROUP)
    a = -jnp.exp(a_log.astype(f32)).reshape(SSD_GROUPS, SSD_HEADS_PER_GROUP)
    s0 = ssm_s.astype(f32).reshape(b, SSD_GROUPS, SSD_HEADS_PER_GROUP, SSD_HEAD_DIM, SSD_STATE)
    y, s_ssm = ssd_chunked(xs * dt[..., None], dt * a, bm, cm, s0)
    y = y + xs * d_skip.reshape(SSD_GROUPS, SSD_HEADS_PER_GROUP)[..., None]
    y = y.reshape(b, l, SSD_WIDTH) * jax.nn.silu(z.astype(f32))
    y_a = group_rms_norm(y, SSD_GROUPS, norm_w)
    new_ssm = s_ssm.reshape(b, SSD_HEADS, SSD_HEAD_DIM, SSD_STATE)

    prev = jnp.concatenate([shift_buf.astype(p.dtype), p[:, :-1]], axis=1)
    new_shift = p[:, -1:]
    m = (p + (prev - p) * mu).astype(f32)
    r, k, v, wd, ad = jnp.split(m, np.cumsum([RWKV_WIDTH, RWKV_WIDTH, RWKV_WIDTH, DECAY_LORA]).tolist(), axis=-1)
    wlog = -jax.nn.softplus(-(w0 + jnp.tanh(wd) @ w_up)) - 0.5
    decay = jnp.exp(-jnp.exp(wlog))
    aa = jax.nn.sigmoid(a0 + ad @ a_up)
    heads = lambda t: t.reshape(b, l, RWKV_HEADS, RWKV_HEAD_DIM)
    kk = heads(k * k_k)
    kk = kk * lax.rsqrt(jnp.maximum(jnp.sum(kk * kk, -1, keepdims=True), 1e-24))
    k = k * (1.0 + (aa - 1.0) * k_a)
    r, decay, k, v, aa = heads(r), heads(decay), heads(k), heads(v), heads(aa)
    o, s_wkv = wkv7_scan(r, decay, k, v, kk, aa, wkv_s.astype(f32))
    o = head_norm(o, RWKV_GN_EPS).reshape(b, l, RWKV_WIDTH) * lnx_w + lnx_b
    o = o + (jnp.sum(r * k * r_k, -1, keepdims=True) * v).reshape(b, l, RWKV_WIDTH)
    y_b = o * jax.nn.silu(g.astype(f32))

    out = jnp.einsum('ble,ed->bld', jnp.concatenate([y_a, y_b], -1).astype(x.dtype), w_out)
    x = layer_norm(ALPHA * x + out, ln_w, ln_b)
    return x, new_conv, new_ssm, new_shift, s_wkv


def ret_layer(x, ret_s, pos, w_in, gn_w, w_out, ln_w, ln_b):
    f32 = jnp.float32
    b, l, _ = x.shape
    proj = jnp.einsum('bld,de->ble', x, w_in).astype(f32)
    q, k, v, g = jnp.split(proj, [RET_QK_WIDTH, 2 * RET_QK_WIDTH, 2 * RET_QK_WIDTH + RET_WIDTH], axis=-1)
    q = rotary(q.reshape(b, l, RET_HEADS, RET_QK_DIM), pos) * RET_QK_DIM ** -0.5
    k = rotary(k.reshape(b, l, RET_HEADS, RET_QK_DIM), pos)
    v = v.reshape(b, l, RET_HEADS, RET_V_DIM)
    o, s = retention_chunked(q, k, v, ret_s.astype(f32))
    o = head_norm(o, RET_GN_EPS).reshape(b, l, RET_WIDTH) * gn_w
    y = o * jax.nn.silu(g)
    out = jnp.einsum('ble,ed->bld', y.astype(x.dtype), w_out)
    x = layer_norm(ALPHA * x + out, ln_w, ln_b)
    return x, s


def setup_inputs(seed: int = 0) -> dict:
    key = jax.random.key(seed)
    ks = jax.random.split(key, 32)
    nab, nc = N_AB_LAYERS, N_C_LAYERS

    def nrm(k, shape, scale):
        return jax.random.normal(k, shape, jnp.float32) * scale

    def uni(k, shape, lo, hi):
        return jax.random.uniform(k, shape, jnp.float32, lo, hi)

    dt0 = jnp.exp(uni(ks[10], (nab, SSD_HEADS), math.log(1e-3), math.log(1e-1)))
    return {
        'x_prompt': nrm(ks[0], (BATCH, SEQ, D_MODEL), 1.0),
        'x_sample': nrm(ks[1], (DEC_BATCH, DEC_SEQ, D_MODEL), 1.0),
        'state_conv': nrm(ks[2], (nab, DEC_BATCH, SSD_CONV - 1, SSD_CONV_DIM), 1.0),
        'state_ssm': nrm(ks[3], (nab, DEC_BATCH, SSD_HEADS, SSD_HEAD_DIM, SSD_STATE), 0.5),
        'state_shift': nrm(ks[4], (nab, DEC_BATCH, 1, SHIFT_DIM), 1.0),
        'state_wkv': nrm(ks[5], (nab, DEC_BATCH, RWKV_HEADS, RWKV_HEAD_DIM, RWKV_HEAD_DIM), 0.5),
        'state_ret': nrm(ks[6], (nc, DEC_BATCH, RET_HEADS, RET_QK_DIM, RET_V_DIM), 1.0),
        'ab_w_in': nrm(ks[7], (nab, D_MODEL, AB_IN), D_MODEL ** -0.5),
        'ssd_conv_w': nrm(ks[8], (nab, SSD_CONV, SSD_CONV_DIM), SSD_CONV ** -0.5),
        'ssd_conv_b': nrm(ks[9], (nab, SSD_CONV_DIM), 0.02),
        'ssd_dt_bias': dt0 + jnp.log(-jnp.expm1(-dt0)),
        'ssd_a_log': jnp.log(uni(ks[11], (nab, SSD_HEADS), 1.0, 16.0)),
        'ssd_d': 1.0 + nrm(ks[12], (nab, SSD_HEADS), 0.1),
        'ssd_norm_w': 1.0 + nrm(ks[13], (nab, SSD_WIDTH), 0.02),
        'rwkv_mu': uni(ks[14], (nab, SHIFT_DIM), 0.0, 1.0),
        'rwkv_w0': uni(ks[15], (nab, RWKV_WIDTH), -6.0, 1.0),
        'rwkv_w_up': nrm(ks[16], (nab, DECAY_LORA, RWKV_WIDTH), 0.1 * DECAY_LORA ** -0.5),
        'rwkv_a0': nrm(ks[17], (nab, RWKV_WIDTH), 0.1),
        'rwkv_a_up': nrm(ks[18], (nab, AAA_LORA, RWKV_WIDTH), 0.1 * AAA_LORA ** -0.5),
        'rwkv_k_k': 0.85 + nrm(ks[19], (nab, RWKV_WIDTH), 0.02),
        'rwkv_k_a': 1.0 + nrm(ks[20], (nab, RWKV_WIDTH), 0.02),
        'rwkv_r_k': nrm(ks[21], (nab, RWKV_HEADS, RWKV_HEAD_DIM), 0.1),
        'rwkv_lnx_w': 1.0 + nrm(ks[22], (nab, RWKV_WIDTH), 0.02),
        'rwkv_lnx_b': nrm(ks[23], (nab, RWKV_WIDTH), 0.02),
        'ab_w_out': nrm(ks[24], (nab, MIX_WIDTH, D_MODEL), BETA * MIX_WIDTH ** -0.5),
        'ab_ln_w': 1.0 + nrm(ks[25], (nab, D_MODEL), 0.02),
        'ab_ln_b': nrm(ks[26], (nab, D_MODEL), 0.02),
        'ret_w_in': nrm(ks[27], (nc, D_MODEL, RET_IN), D_MODEL ** -0.5),
        'ret_gn_w': 1.0 + nrm(ks[28], (nc, RET_WIDTH), 0.02),
        'ret_w_out': nrm(ks[29], (nc, RET_WIDTH, D_MODEL), BETA * RET_WIDTH ** -0.5),
        'ret_ln_w': 1.0 + nrm(ks[30], (nc, D_MODEL), 0.02),
        'ret_ln_b': nrm(ks[31], (nc, D_MODEL), 0.02),
    }


def reference(x_prompt, x_sample, state_conv, state_ssm, state_shift, state_wkv, state_ret,
              ab_w_in, ssd_conv_w, ssd_conv_b, ssd_dt_bias, ssd_a_log, ssd_d, ssd_norm_w,
              rwkv_mu, rwkv_w0, rwkv_w_up, rwkv_a0, rwkv_a_up, rwkv_k_k, rwkv_k_a, rwkv_r_k,
              rwkv_lnx_w, rwkv_lnx_b, ab_w_out, ab_ln_w, ab_ln_b,
              ret_w_in, ret_gn_w, ret_w_out, ret_ln_w, ret_ln_b):
    bp = x_prompt.shape[0]
    pos_p = jnp.arange(x_prompt.shape[1])
    pos_s = PAST_LEN + jnp.arange(x_sample.shape[1])
    hp, hs = x_prompt, x_sample
    pc, pssm, pshift, pwkv, pret = [], [], [], [], []
    sc, sssm, sshift, swkv, sret = [], [], [], [], []
    for layer in range(DEPTH):
        j = layer // 2
        if layer % 2 == 0:
            wab = [t[j] for t in (ab_w_in, ssd_conv_w, ssd_conv_b, ssd_dt_bias, ssd_a_log, ssd_d, ssd_norm_w,
                                  rwkv_mu, rwkv_w0, rwkv_w_up, rwkv_a0, rwkv_a_up, rwkv_k_k, rwkv_k_a, rwkv_r_k,
                                  rwkv_lnx_w, rwkv_lnx_b, ab_w_out, ab_ln_w, ab_ln_b)]
            hp, c, s, sh, wk = ab_layer(
                hp,
                jnp.zeros((bp, SSD_CONV - 1, SSD_CONV_DIM), hp.dtype),
                jnp.zeros((bp, SSD_HEADS, SSD_HEAD_DIM, SSD_STATE), jnp.float32),
                jnp.zeros((bp, 1, SHIFT_DIM), hp.dtype),
                jnp.zeros((bp, RWKV_HEADS, RWKV_HEAD_DIM, RWKV_HEAD_DIM), jnp.float32),
                *wab)
            pc.append(c); pssm.append(s); pshift.append(sh); pwkv.append(wk)
            hs, c, s, sh, wk = ab_layer(hs, state_conv[j], state_ssm[j], state_shift[j], state_wkv[j], *wab)
            sc.append(c); sssm.append(s); sshift.append(sh); swkv.append(wk)
        else:
            wc = [t[j] for t in (ret_w_in, ret_gn_w, ret_w_out, ret_ln_w, ret_ln_b)]
            hp, s = ret_layer(hp, jnp.zeros((bp, RET_HEADS, RET_QK_DIM, RET_V_DIM), jnp.float32), pos_p, *wc)
            pret.append(s)
            hs, s = ret_layer(hs, state_ret[j], pos_s, *wc)
            sret.append(s)
    return (hp, hs,
            jnp.stack(pc), jnp.stack(pssm), jnp.stack(pshift), jnp.stack(pwkv), jnp.stack(pret),
            jnp.stack(sc), jnp.stack(sssm), jnp.stack(sshift), jnp.stack(swkv), jnp.stack(sret))
```

```python
import functools
import math

import jax
import jax.numpy as jnp
import numpy as np
from jax import lax
from jax.experimental import pallas as pl
from jax.experimental.pallas import tpu as pltpu

F32 = jnp.float32
BF16 = jnp.bfloat16

D_MODEL = 4096
DEPTH = 2
PAST_LEN = 16384
SSD_WIDTH = 4096
SSD_HEAD_DIM = 64
SSD_HEADS = 64
SSD_GROUPS = 8
SSD_HPG = 8
SSD_STATE = 128
SSD_CONV = 4
SSD_CONV_DIM = SSD_WIDTH + 2 * SSD_GROUPS * SSD_STATE
RWKV_WIDTH = 4096
RWKV_HEAD_DIM = 64
RWKV_HEADS = 64
LORA = 128
SHIFT_DIM = 3 * RWKV_WIDTH + 2 * LORA
RET_HEADS = 16
RET_QK_DIM = 256
RET_V_DIM = 512
RET_QK_WIDTH = 4096
RET_WIDTH = 8192
ROPE_BASE = 10000.0
CHUNK = 128
ALPHA = (2 * DEPTH) ** 0.25
LN_EPS = 1e-5
RMS_EPS = 1e-5
RWKV_GN_EPS = 64e-5
RET_GN_EPS = 1e-6

LANE = 128
VMEM_LIMIT = 56 * 1024 * 1024
WKV_CHUNK = 64
RW_ROWS = 128
HB = 8
RET_HB = 2
SSM_BT = 2
WKV_BT = 2

_C_Z, _C_XS, _C_B, _C_C = 0, 4096, 8192, 9216
_C_R, _C_K, _C_V, _C_G, _C_WD, _C_AD = 10240, 14336, 18432, 22528, 26624, 26752
AB_MAIN = 26880


def _cp(sem):
    return pltpu.CompilerParams(dimension_semantics=sem, vmem_limit_bytes=VMEM_LIMIT)


def _silu(x):
    return x * jax.nn.sigmoid(x)


def _softplus(x):
    return jnp.maximum(x, 0.0) + jnp.log1p(jnp.exp(-jnp.abs(x)))


def _dot(a, b):
    return jnp.dot(a.astype(BF16), b.astype(BF16), preferred_element_type=F32)


def _dot_nt(a, b):
    return lax.dot_general(a.astype(BF16), b.astype(BF16), (((1,), (1,)), ((), ())),
                           preferred_element_type=F32)


def _dot_tn(a, b):
    return lax.dot_general(a.astype(BF16), b.astype(BF16), (((0,), (0,)), ((), ())),
                           preferred_element_type=F32)


def _split(x, n):
    parts, r = [], x
    for _ in range(n):
        h = r.astype(BF16)
        parts.append(h)
        r = r - h.astype(F32)
    return parts


def _dot01(m01, x, n=3):
    return sum(jnp.dot(m01, p, preferred_element_type=F32) for p in _split(x, n))


def _dot01_r(x, m01, n=2):
    return sum(jnp.dot(p, m01, preferred_element_type=F32) for p in _split(x, n))


def _onehot_cols(b, n):
    rows = lax.broadcasted_iota(jnp.int32, (LANE, n), 0)
    return jnp.where(rows == b, 1.0, 0.0).astype(BF16)


def _mm_kernel(x_ref, w_ref, o_ref):
    o_ref[...] = jnp.dot(x_ref[...], w_ref[...], preferred_element_type=F32)


def _pick_tile(n, prefs):
    for t in prefs:
        if n % t == 0:
            return t
    return n


def _matmul(x, w, name):
    m, k = x.shape
    n = w.shape[1]
    tm = _pick_tile(m, (640, 512, 256, 128))
    tn = _pick_tile(n, (1280, 1024, 512, 256, 128) if k <= 4096 else (512, 256, 128))
    return pl.pallas_call(
        _mm_kernel,
        grid=(n // tn, m // tm),
        in_specs=[pl.BlockSpec((tm, k), lambda j, i: (i, 0)),
                  pl.BlockSpec((k, tn), lambda j, i: (0, j))],
        out_specs=pl.BlockSpec((tm, tn), lambda j, i: (i, j)),
        out_shape=jax.ShapeDtypeStruct((m, n), F32),
        compiler_params=_cp(("parallel", "parallel")),
        name=name,
    )(x, w)


def _ln_kernel(x_ref, o_ref, w_ref, b_ref, y_ref, yb_ref):
    h = ALPHA * x_ref[...] + o_ref[...]
    mu = jnp.mean(h, -1, keepdims=True)
    d = h - mu
    var = jnp.mean(d * d, -1, keepdims=True)
    y = d * lax.rsqrt(var + LN_EPS) * w_ref[...] + b_ref[...]
    y_ref[...] = y
    yb_ref[...] = y.astype(BF16)


def _deepnorm(x, o, w, b, name):
    m, d = x.shape
    tm = _pick_tile(m, (128, 64, 8))
    row = pl.BlockSpec((tm, d), lambda i: (i, 0))
    vec = pl.BlockSpec((1, d), lambda i: (0, 0))
    return pl.pallas_call(
        _ln_kernel,
        grid=(m // tm,),
        in_specs=[row, row, vec, vec],
        out_specs=[row, row],
        out_shape=[jax.ShapeDtypeStruct((m, d), F32), jax.ShapeDtypeStruct((m, d), BF16)],
        compiler_params=_cp(("parallel",)),
        name=name,
    )(x, o, w.reshape(1, d), b.reshape(1, d))


def _ssd_prompt_kernel(z_ref, xs_ref, b_ref, c_ref, dtc_ref, dtr_ref,
                       cwx_ref, cwb_ref, cwc_ref, cbx_ref, cbb_ref, cbc_ref,
                       dtbc_ref, dtbr_ref, alc_ref, alr_ref, dsk_ref, nw_ref, tri_ref,
                       y_ref, s_ref, bufx, bufb, bufc):
    L = CHUNK
    c = pl.program_id(2)

    @pl.when(c == 0)
    def _init():
        for buf in (bufx, bufb, bufc):
            buf[0:8, :] = jnp.zeros((8, buf.shape[1]), F32)
        s_ref[...] = jnp.zeros(s_ref.shape, F32)

    def conv(u_ref, buf, w_ref, bias_ref):
        buf[8:8 + L, :] = u_ref[...]
        acc = bias_ref[...] + buf[5:5 + L, :] * w_ref[0:1, :]
        for k in range(1, SSD_CONV):
            acc = acc + buf[5 + k:5 + k + L, :] * w_ref[k:k + 1, :]
        buf[0:8, :] = buf[L:L + 8, :]
        return _silu(acc)

    xs = conv(xs_ref, bufx, cwx_ref, cbx_ref)
    bm = conv(b_ref, bufb, cwb_ref, cbb_ref)
    cm = conv(c_ref, bufc, cwc_ref, cbc_ref)
    dtc = _softplus(dtc_ref[0] + dtbc_ref[0])
    dtr = _softplus(dtr_ref[0] + dtbr_ref[0])
    adt_c = dtc * (-jnp.exp(alc_ref[0]))
    adt_r = dtr * (-jnp.exp(alr_ref[0]))
    tri = tri_ref[...]
    cum_c = _dot01(tri, adt_c)
    cum_r = sum(lax.dot_general(p, tri, (((1,), (1,)), ((), ())), preferred_element_type=F32)
                for p in _split(adt_r, 3))

    li = lax.broadcasted_iota(jnp.int32, (L, L), 0)
    si = lax.broadcasted_iota(jnp.int32, (L, L), 1)
    causal = li >= si
    cb = _dot_nt(cm, bm)
    ys = []
    for r in range(SSD_HPG):
        cc = cum_c[:, r:r + 1]
        cr = cum_r[r:r + 1, :]
        decay = jnp.exp(jnp.where(causal, cc - cr, -jnp.inf))
        xdt = xs[:, r * 64:(r + 1) * 64] * dtc[:, r:r + 1]
        s_old = s_ref[0, r]
        clast = cc[L - 1:L, :]
        ys.append(_dot(cb * decay, xdt) + _dot_nt(cm, s_old) * jnp.exp(cc))
        s_ref[0, r] = s_old * jnp.exp(clast) + _dot_tn(xdt * jnp.exp(clast - cc), bm)
    y = jnp.concatenate(ys, axis=1) + xs * dsk_ref[...]
    y = y * _silu(z_ref[...])
    y = y * lax.rsqrt(jnp.mean(y * y, -1, keepdims=True) + RMS_EPS) * nw_ref[...]
    y_ref[...] = y.astype(BF16)


def _ssd_params(conv_w, conv_b, dt_bias, a_log, d_skip, norm_w):
    g = SSD_GROUPS
    return dict(
        cwx=conv_w[:, :4096], cwb=conv_w[:, 4096:5120], cwc=conv_w[:, 5120:],
        cbx=conv_b[:4096].reshape(1, -1), cbb=conv_b[4096:5120].reshape(1, -1), cbc=conv_b[5120:].reshape(1, -1),
        dtbc=dt_bias.reshape(g, 1, 8), dtbr=dt_bias.reshape(g, 8, 1),
        alc=a_log.reshape(g, 1, 8), alr=a_log.reshape(g, 8, 1),
        dsk=jnp.repeat(d_skip, SSD_HEAD_DIM).reshape(1, -1), nw=norm_w.reshape(1, -1))


def _ssd_prompt(proj, dtc, dtr, nb, l, sp):
    nc = l // CHUNK
    rb = lambda b, g, c: b * nc + c
    tri = jnp.tril(jnp.ones((CHUNK, CHUNK), BF16))
    in_specs = [
        pl.BlockSpec((CHUNK, 512), lambda b, g, c: (rb(b, g, c), _C_Z // 512 + g)),
        pl.BlockSpec((CHUNK, 512), lambda b, g, c: (rb(b, g, c), _C_XS // 512 + g)),
        pl.BlockSpec((CHUNK, 128), lambda b, g, c: (rb(b, g, c), _C_B // 128 + g)),
        pl.BlockSpec((CHUNK, 128), lambda b, g, c: (rb(b, g, c), _C_C // 128 + g)),
        pl.BlockSpec((1, CHUNK, 8), lambda b, g, c: (g, rb(b, g, c), 0)),
        pl.BlockSpec((1, 8, CHUNK), lambda b, g, c: (g, 0, rb(b, g, c))),
        pl.BlockSpec((SSD_CONV, 512), lambda b, g, c: (0, g)),
        pl.BlockSpec((SSD_CONV, 128), lambda b, g, c: (0, g)),
        pl.BlockSpec((SSD_CONV, 128), lambda b, g, c: (0, g)),
        pl.BlockSpec((1, 512), lambda b, g, c: (0, g)),
        pl.BlockSpec((1, 128), lambda b, g, c: (0, g)),
        pl.BlockSpec((1, 128), lambda b, g, c: (0, g)),
        pl.BlockSpec((1, 1, 8), lambda b, g, c: (g, 0, 0)),
        pl.BlockSpec((1, 8, 1), lambda b, g, c: (g, 0, 0)),
        pl.BlockSpec((1, 1, 8), lambda b, g, c: (g, 0, 0)),
        pl.BlockSpec((1, 8, 1), lambda b, g, c: (g, 0, 0)),
        pl.BlockSpec((1, 512), lambda b, g, c: (0, g)),
        pl.BlockSpec((1, 512), lambda b, g, c: (0, g)),
        pl.BlockSpec((CHUNK, CHUNK), lambda b, g, c: (0, 0)),
    ]
    out_specs = [pl.BlockSpec((CHUNK, 512), lambda b, g, c: (rb(b, g, c), g)),
                 pl.BlockSpec((1, SSD_HPG, SSD_HEAD_DIM, SSD_STATE), lambda b, g, c: (b, g, 0, 0))]
    return pl.pallas_call(
        _ssd_prompt_kernel,
        grid=(nb, SSD_GROUPS, nc),
        in_specs=in_specs,
        out_specs=out_specs,
        out_shape=[jax.ShapeDtypeStruct((nb * l, SSD_WIDTH), BF16),
                   jax.ShapeDtypeStruct((nb, SSD_HEADS, SSD_HEAD_DIM, SSD_STATE), F32)],
        scratch_shapes=[pltpu.VMEM((CHUNK + 8, 512), F32), pltpu.VMEM((CHUNK + 8, 128), F32),
                        pltpu.VMEM((CHUNK + 8, 128), F32)],
        compiler_params=_cp(("parallel", "parallel", "arbitrary")),
        name="ssd_prompt",
    )(proj, proj, proj, proj, dtc, dtr, sp["cwx"], sp["cwb"], sp["cwc"], sp["cbx"], sp["cbb"], sp["cbc"],
      sp["dtbc"], sp["dtbr"], sp["alc"], sp["alr"], sp["dsk"], sp["nw"], tri)


def _rwkv_mix(rm, km, vm, wdm, adm, w0, wup, a0, aup, k_k, k_a, seg):
    wlog = -_softplus(-(w0 + _dot(jnp.tanh(wdm), wup))) - 0.5
    logw = -jnp.exp(wlog)
    aa = jax.nn.sigmoid(a0 + _dot(adm, aup))
    kkr = km * k_k
    kk = kkr * lax.rsqrt(jnp.maximum(_dot01_r(kkr * kkr, seg), 1e-24))
    k2 = km * (1.0 + (aa - 1.0) * k_a)
    return logw, kk, k2, kk * aa


def _rwkv_out(o, rm, k2, vm, g, lnw, lnb, rk, seg):
    inv = 1.0 / RWKV_HEAD_DIM
    mean = _dot01_r(o, seg) * inv
    d = o - mean
    var = _dot01_r(d * d, seg) * inv
    on = d * lax.rsqrt(var + RWKV_GN_EPS) * lnw + lnb
    bonus = _dot01_r(rm * k2 * rk, seg) * vm
    return ((on + bonus) * _silu(g)).astype(BF16)


def _rwkv_prompt_kernel(r_ref, k_ref, v_ref, g_ref, wd_ref, ad_ref,
                        mur_ref, muk_ref, muv_ref, muwd_ref, muad_ref,
                        w0_ref, wup_ref, a0_ref, aup_ref, kk_ref, ka_ref, lnw_ref, lnb_ref, rk_ref,
                        seg_ref, tri_ref,
                        y_ref, s_ref, cr, ck, cv, cwd, cad):
    R, C = RW_ROWS, WKV_CHUNK
    c = pl.program_id(2)

    @pl.when(c == 0)
    def _init():
        for buf in (cr, ck, cv, cwd, cad):
            buf[...] = jnp.zeros(buf.shape, F32)
        s_ref[...] = jnp.zeros(s_ref.shape, F32)

    row0 = lax.broadcasted_iota(jnp.int32, (R, 1), 0) == 0

    def shift(x_ref, carry, mu_ref):
        x = x_ref[...]
        prev = jnp.where(row0, carry[0:1, :], pltpu.roll(x, 1, 0))
        carry[0:1, :] = x[R - 1:R, :]
        return x + (prev - x) * mu_ref[...]

    rm = shift(r_ref, cr, mur_ref)
    km = shift(k_ref, ck, muk_ref)
    vm = shift(v_ref, cv, muv_ref)
    wdm = shift(wd_ref, cwd, muwd_ref)
    adm = shift(ad_ref, cad, muad_ref)
    seg = seg_ref[...]
    logw, kk, k2, bv = _rwkv_mix(rm, km, vm, wdm, adm, w0_ref[...], wup_ref[...], a0_ref[...], aup_ref[...],
                                 kk_ref[...], ka_ref[...], seg)

    tri = tri_ref[...]
    li = lax.broadcasted_iota(jnp.int32, (C, C), 0)
    si = lax.broadcasted_iota(jnp.int32, (C, C), 1)
    strict = li > si
    incl = li >= si
    eye = jnp.where(li == si, 1.0, 0.0)
    o_chunks = []
    for sc in range(R // C):
        rows = slice(sc * C, (sc + 1) * C)
        lw = logw[rows]
        cs = _dot01(tri, lw)
        cl = cs[C - 1:C, :]
        e_tail = jnp.exp(cl - cs)
        e_neg = jnp.exp(-cs)
        bt = kk[rows] * jnp.exp(cs - lw)
        bb = bv[rows] * e_neg
        kt = k2[rows] * e_neg
        rt = rm[rows] * jnp.exp(cs)
        bh = bv[rows] * e_tail
        kh = k2[rows] * e_tail
        pc = jnp.exp(cl)
        vv = vm[rows]
        outs = []
        for h in range(HB):
            sl = slice(h * 64, (h + 1) * 64)
            lhs2 = jnp.concatenate([bt[:, sl], rt[:, sl]], 0).astype(BF16)
            rhs2 = jnp.concatenate([bb[:, sl], kt[:, sl]], 0).astype(BF16)
            gm = _dot_nt(lhs2, rhs2)
            lb = jnp.where(strict, gm[0:C, 0:C], 0.0)
            lk = jnp.where(strict, gm[0:C, C:2 * C], 0.0)
            rb = jnp.where(incl, gm[C:2 * C, 0:C], 0.0)
            rkm = jnp.where(incl, gm[C:2 * C, C:2 * C], 0.0)
            x = -lb
            t = eye + x
            for _ in range(int(math.log2(C)) - 1):
                x = _dot(x, x)
                t = t + _dot(t, x)
            s0 = s_ref[0, h]
            w1 = _dot_nt(lhs2, s0)
            vh = vv[:, sl]
            u = _dot(t, w1[0:C] + _dot(lk, vh))
            y = w1[C:2 * C] + _dot(jnp.concatenate([-rb, rkm], 1), jnp.concatenate([u, vh], 0))
            s_ref[0, h] = s0 * pc[:, sl] + _dot_tn(jnp.concatenate([-u, vh], 0),
                                                   jnp.concatenate([bh[:, sl], kh[:, sl]], 0))
            outs.append(y)
        o_chunks.append(jnp.concatenate(outs, 1))
    o = jnp.concatenate(o_chunks, 0)
    y_ref[...] = _rwkv_out(o, rm, k2, vm, g_ref[...], lnw_ref[...], lnb_ref[...], rk_ref[...], seg)


def _rwkv_params(mu, w0, w_up, a0, a_up, k_k, k_a, r_k, lnx_w, lnx_b):
    v = lambda t: t.reshape(1, -1)
    return dict(
        mur=v(mu[0:4096]), muk=v(mu[4096:8192]), muv=v(mu[8192:12288]),
        muwd=v(mu[12288:12416]), muad=v(mu[12416:12544]),
        w0=v(w0), wup=w_up.astype(BF16), a0=v(a0), aup=a_up.astype(BF16), kk=v(k_k), ka=v(k_a),
        lnw=v(lnx_w), lnb=v(lnx_b), rk=v(r_k),
        seg=jnp.asarray(np.kron(np.eye(HB), np.ones((64, 64))), BF16))


def _rwkv_prompt(proj, nb, l, rp):
    nr = l // RW_ROWS
    rb = lambda b, h, c: b * nr + c
    w512 = HB * 64
    col = lambda c0: pl.BlockSpec((RW_ROWS, w512), lambda b, h, c: (rb(b, h, c), c0 // w512 + h))
    lora = lambda c0: pl.BlockSpec((RW_ROWS, LORA), lambda b, h, c: (rb(b, h, c), c0 // LORA))
    vec = pl.BlockSpec((1, w512), lambda b, h, c: (0, h))
    vec128 = pl.BlockSpec((1, LORA), lambda b, h, c: (0, 0))
    up = pl.BlockSpec((LORA, w512), lambda b, h, c: (0, h))
    tri = jnp.tril(jnp.ones((WKV_CHUNK, WKV_CHUNK), BF16))
    in_specs = [col(_C_R), col(_C_K), col(_C_V), col(_C_G), lora(_C_WD), lora(_C_AD),
                vec, vec, vec, vec128, vec128,
                vec, up, vec, up, vec, vec, vec, vec, vec,
                pl.BlockSpec((w512, w512), lambda b, h, c: (0, 0)),
                pl.BlockSpec((WKV_CHUNK, WKV_CHUNK), lambda b, h, c: (0, 0))]
    out_specs = [pl.BlockSpec((RW_ROWS, w512), lambda b, h, c: (rb(b, h, c), h)),
                 pl.BlockSpec((1, HB, 64, 64), lambda b, h, c: (b, h, 0, 0))]
    return pl.pallas_call(
        _rwkv_prompt_kernel,
        grid=(nb, RWKV_HEADS // HB, nr),
        in_specs=in_specs,
        out_specs=out_specs,
        out_shape=[jax.ShapeDtypeStruct((nb * l, RWKV_WIDTH), BF16),
                   jax.ShapeDtypeStruct((nb, RWKV_HEADS, 64, 64), F32)],
        scratch_shapes=[pltpu.VMEM((8, w512), F32)] * 3 + [pltpu.VMEM((8, LORA), F32)] * 2,
        compiler_params=_cp(("parallel", "parallel", "arbitrary")),
        name="rwkv_prompt",
    )(proj, proj, proj, proj, proj, proj,
      rp["mur"], rp["muk"], rp["muv"], rp["muwd"], rp["muad"],
      rp["w0"], rp["wup"], rp["a0"], rp["aup"], rp["kk"], rp["ka"], rp["lnw"], rp["lnb"], rp["rk"],
      rp["seg"], tri)


def _trig_kernel(pos_ref, freq_ref, cos_ref, sin_ref):
    ang = pos_ref[...] * freq_ref[...]
    cos_ref[...] = jnp.cos(ang)
    sin_ref[...] = jnp.sin(ang)


def _trig(pos):
    n = pos.shape[0]
    half = RET_QK_DIM // 2
    freq = (ROPE_BASE ** (-jnp.arange(half, dtype=F32) / half)).reshape(1, half)
    posb = jnp.broadcast_to(pos.astype(F32)[:, None], (n, half))
    tn = _pick_tile(n, (256, 128, 8))
    blk = pl.BlockSpec((tn, half), lambda i: (i, 0))
    return pl.pallas_call(
        _trig_kernel, grid=(n // tn,),
        in_specs=[blk, pl.BlockSpec((1, half), lambda i: (0, 0))],
        out_specs=[blk, blk],
        out_shape=[jax.ShapeDtypeStruct((n, half), F32)] * 2,
        name="rope_tables",
    )(posb, freq)


def _rotate(x, cos, sin):
    x1, x2 = x[:, :128], x[:, 128:]
    return jnp.concatenate([x1 * cos - x2 * sin, x1 * sin + x2 * cos], 1)


def _ret_prompt_kernel(lg_ref, q_ref, k_ref, v_ref, g_ref, cos_ref, sin_ref, gnw_ref,
                       y_ref, s_ref):
    L = CHUNK
    hg = pl.program_id(1)
    c = pl.program_id(2)

    @pl.when(c == 0)
    def _init():
        s_ref[...] = jnp.zeros(s_ref.shape, F32)

    cos = cos_ref[...]
    sin = sin_ref[...]
    li = lax.broadcasted_iota(jnp.int32, (L, L), 0)
    si = lax.broadcasted_iota(jnp.int32, (L, L), 1)
    rel = (li - si).astype(F32)
    causal = li >= si
    icol = lax.broadcasted_iota(jnp.int32, (L, 1), 0).astype(F32)
    outs = []
    for j in range(RET_HB):
        lg = lg_ref[hg * RET_HB + j]
        qr = _rotate(q_ref[:, j * 256:(j + 1) * 256], cos, sin) * (RET_QK_DIM ** -0.5)
        kr = _rotate(k_ref[:, j * 256:(j + 1) * 256], cos, sin)
        v = v_ref[:, j * 512:(j + 1) * 512]
        inner = jnp.exp(jnp.where(causal, rel * lg, -jnp.inf))
        sc = _dot_nt(qr, kr) * inner
        s0 = s_ref[0, j]
        y = _dot(sc, v) + _dot(qr, s0) * jnp.exp((icol + 1.0) * lg)
        s_ref[0, j] = s0 * jnp.exp(L * lg) + _dot_tn(kr * jnp.exp((L - 1.0 - icol) * lg), v)
        mu = jnp.mean(y, -1, keepdims=True)
        d = y - mu
        var = jnp.mean(d * d, -1, keepdims=True)
        outs.append(d * lax.rsqrt(var + RET_GN_EPS))
    o = jnp.concatenate(outs, 1) * gnw_ref[...]
    y_ref[...] = (o * _silu(g_ref[...])).astype(BF16)


def _ret_log_g():
    return jnp.log1p(-jnp.exp2(-5.0 - jnp.arange(RET_HEADS, dtype=F32)))


def _ret_prompt(proj, cos, sin, gn_w, nb, l):
    nc = l // CHUNK
    rb = lambda b, h, c, lg: b * nc + c
    wq, wv = RET_HB * RET_QK_DIM, RET_HB * RET_V_DIM
    grid_spec = pltpu.PrefetchScalarGridSpec(
        num_scalar_prefetch=1,
        grid=(nb, RET_HEADS // RET_HB, nc),
        in_specs=[
            pl.BlockSpec((CHUNK, wq), lambda b, h, c, lg: (rb(b, h, c, lg), h)),
            pl.BlockSpec((CHUNK, wq), lambda b, h, c, lg: (rb(b, h, c, lg), RET_QK_WIDTH // wq + h)),
            pl.BlockSpec((CHUNK, wv), lambda b, h, c, lg: (rb(b, h, c, lg), 2 * RET_QK_WIDTH // wv + h)),
            pl.BlockSpec((CHUNK, wv), lambda b, h, c, lg: (rb(b, h, c, lg), (2 * RET_QK_WIDTH + RET_WIDTH) // wv + h)),
            pl.BlockSpec((CHUNK, 128), lambda b, h, c, lg: (c, 0)),
            pl.BlockSpec((CHUNK, 128), lambda b, h, c, lg: (c, 0)),
            pl.BlockSpec((1, wv), lambda b, h, c, lg: (0, h)),
        ],
        out_specs=[pl.BlockSpec((CHUNK, wv), lambda b, h, c, lg: (rb(b, h, c, lg), h)),
                   pl.BlockSpec((1, RET_HB, RET_QK_DIM, RET_V_DIM), lambda b, h, c, lg: (b, h, 0, 0))],
    )
    return pl.pallas_call(
        _ret_prompt_kernel,
        grid_spec=grid_spec,
        out_shape=[jax.ShapeDtypeStruct((nb * l, RET_WIDTH), BF16),
                   jax.ShapeDtypeStruct((nb, RET_HEADS, RET_QK_DIM, RET_V_DIM), F32)],
        compiler_params=_cp(("parallel", "parallel", "arbitrary")),
        name="ret_prompt",
    )(_ret_log_g(), proj, proj, proj, proj, cos, sin, gn_w.reshape(1, -1))


def _ssd_pre_kernel(xs_ref, b_ref, c_ref, csx_ref, csb_ref, csc_ref, dtc_ref,
                    cwx_ref, cwb_ref, cwc_ref, cbx_ref, cbb_ref, cbc_ref, dtb_ref, al_ref,
                    xa_ref, ba_ref, ca_ref, xh_ref, xl_ref, dt_ref, dec_ref):
    def conv(u_ref, cs_ref, w_ref, bias_ref):
        acc = bias_ref[...] + u_ref[...] * w_ref[SSD_CONV - 1:SSD_CONV, :]
        for k in range(SSD_CONV - 1):
            acc = acc + cs_ref[k] * w_ref[k:k + 1, :]
        return _silu(acc)

    xs = conv(xs_ref, csx_ref, cwx_ref, cbx_ref)
    xa_ref[...] = xs
    ba_ref[0] = conv(b_ref, csb_ref, cwb_ref, cbb_ref)
    ca_ref[0] = conv(c_ref, csc_ref, cwc_ref, cbc_ref)
    hi, lo = _split(xs.T, 2)
    xh_ref[...] = hi
    xl_ref[...] = lo
    dt = _softplus(dtc_ref[0] + dtb_ref[0])
    dt_ref[0] = dt
    dec_ref[0] = jnp.exp(dt * (-jnp.exp(al_ref[0])))


def _ssd_pre(proj, rb0, ns, cs_t, dtc, sp):
    g8 = SSD_GROUPS
    in_specs = [
        pl.BlockSpec((ns, 512), lambda g: (rb0, _C_XS // 512 + g)),
        pl.BlockSpec((ns, 128), lambda g: (rb0, _C_B // 128 + g)),
        pl.BlockSpec((ns, 128), lambda g: (rb0, _C_C // 128 + g)),
        pl.BlockSpec((3, ns, 512), lambda g: (0, 0, g)),
        pl.BlockSpec((3, ns, 128), lambda g: (0, 0, 4096 // 128 + g)),
        pl.BlockSpec((3, ns, 128), lambda g: (0, 0, 5120 // 128 + g)),
        pl.BlockSpec((1, ns, 8), lambda g: (g, rb0, 0)),
        pl.BlockSpec((SSD_CONV, 512), lambda g: (0, g)),
        pl.BlockSpec((SSD_CONV, 128), lambda g: (0, g)),
        pl.BlockSpec((SSD_CONV, 128), lambda g: (0, g)),
        pl.BlockSpec((1, 512), lambda g: (0, g)),
        pl.BlockSpec((1, 128), lambda g: (0, g)),
        pl.BlockSpec((1, 128), lambda g: (0, g)),
        pl.BlockSpec((1, 1, 8), lambda g: (g, 0, 0)),
        pl.BlockSpec((1, 1, 8), lambda g: (g, 0, 0)),
    ]
    out_specs = [
        pl.BlockSpec((ns, 512), lambda g: (0, g)),
        pl.BlockSpec((1, ns, 128), lambda g: (g, 0, 0)),
        pl.BlockSpec((1, ns, 128), lambda g: (g, 0, 0)),
        pl.BlockSpec((512, ns), lambda g: (g, 0)),
        pl.BlockSpec((512, ns), lambda g: (g, 0)),
        pl.BlockSpec((1, ns, 8), lambda g: (g, 0, 0)),
        pl.BlockSpec((1, ns, 8), lambda g: (g, 0, 0)),
    ]
    out_shape = [
        jax.ShapeDtypeStruct((ns, SSD_WIDTH), F32),
        jax.ShapeDtypeStruct((g8, ns, SSD_STATE), F32),
        jax.ShapeDtypeStruct((g8, ns, SSD_STATE), F32),
        jax.ShapeDtypeStruct((SSD_WIDTH, ns), BF16),
        jax.ShapeDtypeStruct((SSD_WIDTH, ns), BF16),
        jax.ShapeDtypeStruct((g8, ns, 8), F32),
        jax.ShapeDtypeStruct((g8, ns, 8), F32),
    ]
    return pl.pallas_call(
        _ssd_pre_kernel, grid=(g8,), in_specs=in_specs, out_specs=out_specs, out_shape=out_shape,
        compiler_params=_cp(("parallel",)), name="ssd_sample_pre",
    )(proj, proj, proj, cs_t, cs_t, cs_t, dtc, sp["cwx"], sp["cwb"], sp["cwc"], sp["cbx"], sp["cbb"], sp["cbc"],
      sp["dtbc"], sp["alc"])


def _ssm_state_kernel(dt_ref, dec_ref, s_ref, xh_ref, xl_ref, b_ref, c_ref, so_ref, y_ref, xb, ysc):
    i = pl.program_id(0)
    for j in range(SSM_BT):
        b = i * SSM_BT + j
        e = _onehot_cols(b, LANE)
        xb[...] = (jnp.dot(xh_ref[...], e, preferred_element_type=F32)
                   + jnp.dot(xl_ref[...], e, preferred_element_type=F32))

        def group(g, carry):
            brow = b_ref[j, pl.ds(g, 1), :]
            crow = c_ref[j, pl.ds(g, 1), :]
            for r in range(SSD_HPG):
                h = g * SSD_HPG + r
                xcol = xb[pl.ds(pl.multiple_of(h * 64, 64), 64), :]
                so_ref[j, h] = s_ref[j, h] * dec_ref[b * SSD_HEADS + h] + (xcol * dt_ref[b * SSD_HEADS + h]) * brow
            sg = so_ref[j, pl.ds(g * SSD_HPG, SSD_HPG)].reshape(SSD_HPG * 64, SSD_STATE)
            y8 = _dot_nt(jnp.broadcast_to(crow, (8, SSD_STATE)), sg)
            ysc[pl.ds(g, 1), :] = y8[0:1, :]
            return carry

        lax.fori_loop(0, SSD_GROUPS, group, 0)
        y_ref[j] = ysc[...]


def _ssm_state(dt, dec, s, xh, xl, ba, ca):
    ns = s.shape[0]
    bt = SSM_BT
    smem = pl.BlockSpec(memory_space=pltpu.SMEM)
    sblk = pl.BlockSpec((bt, SSD_HEADS, SSD_HEAD_DIM, SSD_STATE), lambda i: (i, 0, 0, 0))
    full = pl.BlockSpec((SSD_WIDTH, ns), lambda i: (0, 0))
    bc = pl.BlockSpec((bt, SSD_GROUPS, SSD_STATE), lambda i: (i, 0, 0))
    return pl.pallas_call(
        _ssm_state_kernel, grid=(ns // bt,),
        in_specs=[smem, smem, sblk, full, full, bc, bc],
        out_specs=[sblk, pl.BlockSpec((bt, SSD_GROUPS, 512), lambda i: (i, 0, 0))],
        out_shape=[jax.ShapeDtypeStruct(s.shape, F32), jax.ShapeDtypeStruct((ns, SSD_GROUPS, 512), F32)],
        scratch_shapes=[pltpu.VMEM((SSD_WIDTH, LANE), F32), pltpu.VMEM((SSD_GROUPS, 512), F32)],
        compiler_params=_cp(("parallel",)), name="ssm_sample_state",
    )(dt, dec, s, xh, xl, ba, ca)


def _ssd_post_kernel(y_ref, xa_ref, z_ref, dsk_ref, nw_ref, o_ref):
    y = (y_ref[...] + xa_ref[...] * dsk_ref[...]) * _silu(z_ref[...])
    y = y * lax.rsqrt(jnp.mean(y * y, -1, keepdims=True) + RMS_EPS) * nw_ref[...]
    o_ref[...] = y.astype(BF16)


def _ssd_post(y, xa, proj, rb0, sp):
    ns = y.shape[0]
    blk = pl.BlockSpec((ns, 512), lambda g: (0, g))
    vec = pl.BlockSpec((1, 512), lambda g: (0, g))
    return pl.pallas_call(
        _ssd_post_kernel, grid=(SSD_GROUPS,),
        in_specs=[blk, blk, pl.BlockSpec((ns, 512), lambda g: (rb0, _C_Z // 512 + g)), vec, vec],
        out_specs=blk, out_shape=jax.ShapeDtypeStruct((ns, SSD_WIDTH), BF16),
        compiler_params=_cp(("parallel",)), name="ssd_sample_post",
    )(y, xa, proj, sp["dsk"], sp["nw"])


def _wkv_pre_kernel(r_ref, k_ref, v_ref, wd_ref, ad_ref, sr_ref, sk_ref, sv_ref, swd_ref, sad_ref,
                    mur_ref, muk_ref, muv_ref, muwd_ref, muad_ref,
                    w0_ref, wup_ref, a0_ref, aup_ref, kk_ref, ka_ref, seg_ref,
                    ro_ref, wo_ref, ko_ref, bo_ref, kko_ref, vo_ref, vh_ref, vl_ref):
    mix = lambda x_ref, s_ref, mu_ref: x_ref[...] + (s_ref[...] - x_ref[...]) * mu_ref[...]
    rm = mix(r_ref, sr_ref, mur_ref)
    km = mix(k_ref, sk_ref, muk_ref)
    vm = mix(v_ref, sv_ref, muv_ref)
    wdm = mix(wd_ref, swd_ref, muwd_ref)
    adm = mix(ad_ref, sad_ref, muad_ref)
    logw, kk, k2, bv = _rwkv_mix(rm, km, vm, wdm, adm, w0_ref[...], wup_ref[...], a0_ref[...], aup_ref[...],
                                 kk_ref[...], ka_ref[...], seg_ref[...])
    ro_ref[...] = rm
    wo_ref[...] = jnp.exp(logw)
    ko_ref[...] = k2
    bo_ref[...] = bv
    kko_ref[...] = kk
    vo_ref[...] = vm
    hi, lo = _split(vm.T, 2)
    vh_ref[...] = hi
    vl_ref[...] = lo


def _wkv_pre(proj, rb0, shift, rp):
    ns = shift.shape[0]
    w512 = HB * 64
    col = lambda c0: pl.BlockSpec((ns, w512), lambda h: (rb0, c0 // w512 + h))
    lora = lambda c0: pl.BlockSpec((ns, LORA), lambda h: (rb0, c0 // LORA))
    scol = lambda c0: pl.BlockSpec((ns, w512), lambda h: (0, c0 // w512 + h))
    slora = lambda c0: pl.BlockSpec((ns, LORA), lambda h: (0, c0 // LORA))
    vec = pl.BlockSpec((1, w512), lambda h: (0, h))
    vec128 = pl.BlockSpec((1, LORA), lambda h: (0, 0))
    up = pl.BlockSpec((LORA, w512), lambda h: (0, h))
    row = pl.BlockSpec((ns, w512), lambda h: (0, h))
    tr = pl.BlockSpec((w512, ns), lambda h: (h, 0))
    return pl.pallas_call(
        _wkv_pre_kernel, grid=(RWKV_HEADS // HB,),
        in_specs=[col(_C_R), col(_C_K), col(_C_V), lora(_C_WD), lora(_C_AD),
                  scol(0), scol(4096), scol(8192), slora(12288), slora(12416),
                  vec, vec, vec, vec128, vec128, vec, up, vec, up, vec, vec,
                  pl.BlockSpec((w512, w512), lambda h: (0, 0))],
        out_specs=[row] * 6 + [tr, tr],
        out_shape=[jax.ShapeDtypeStruct((ns, RWKV_WIDTH), F32)] * 6
        + [jax.ShapeDtypeStruct((RWKV_WIDTH, ns), BF16)] * 2,
        compiler_params=_cp(("parallel",)), name="wkv_sample_pre",
    )(proj, proj, proj, proj, proj, shift, shift, shift, shift, shift,
      rp["mur"], rp["muk"], rp["muv"], rp["muwd"], rp["muad"],
      rp["w0"], rp["wup"], rp["a0"], rp["aup"], rp["kk"], rp["ka"], rp["seg"])


def _wkv_state_kernel(s_ref, vh_ref, vl_ref, r_ref, w_ref, k_ref, b_ref, kk_ref, so_ref, y_ref, vb, ysc):
    i = pl.program_id(0)
    for j in range(WKV_BT):
        b = i * WKV_BT + j
        e = _onehot_cols(b, LANE)
        vb[...] = (jnp.dot(vh_ref[...], e, preferred_element_type=F32)
                   + jnp.dot(vl_ref[...], e, preferred_element_type=F32))

        def head(h, carry):
            row = lambda ref: ref[j, pl.ds(h, 1), :]
            s = s_ref[j, h]
            kk = row(kk_ref)
            sk = jnp.sum(s * kk, axis=-1, keepdims=True)
            vcol = vb[pl.ds(pl.multiple_of(h * 64, 64), 64), 0:64]
            sn = s * row(w_ref) - sk * row(b_ref) + vcol * row(k_ref)
            so_ref[j, h] = sn
            y8 = _dot_nt(jnp.broadcast_to(row(r_ref), (8, 64)), sn)
            ysc[pl.ds(h, 1), :] = y8[0:1, :]
            return carry

        lax.fori_loop(0, RWKV_HEADS, head, 0)
        y_ref[j] = ysc[...]


def _wkv_state(s, vh, vl, r, w, k, bvec, kk):
    ns = s.shape[0]
    bt = WKV_BT
    sblk = pl.BlockSpec((bt, RWKV_HEADS, 64, 64), lambda i: (i, 0, 0, 0))
    full = pl.BlockSpec((RWKV_WIDTH, ns), lambda i: (0, 0))
    rows = pl.BlockSpec((bt, RWKV_HEADS, 64), lambda i: (i, 0, 0))
    return pl.pallas_call(
        _wkv_state_kernel, grid=(ns // bt,),
        in_specs=[sblk, full, full, rows, rows, rows, rows, rows],
        out_specs=[sblk, rows],
        out_shape=[jax.ShapeDtypeStruct(s.shape, F32), jax.ShapeDtypeStruct((ns, RWKV_HEADS, 64), F32)],
        scratch_shapes=[pltpu.VMEM((RWKV_WIDTH, LANE), F32), pltpu.VMEM((RWKV_HEADS, 64), F32)],
        compiler_params=_cp(("parallel",)), name="wkv_sample_state",
    )(s, vh, vl, r, w, k, bvec, kk)


def _wkv_post_kernel(o_ref, r_ref, k_ref, v_ref, g_ref, lnw_ref, lnb_ref, rk_ref, seg_ref, y_ref):
    y_ref[...] = _rwkv_out(o_ref[...], r_ref[...], k_ref[...], v_ref[...], g_ref[...],
                           lnw_ref[...], lnb_ref[...], rk_ref[...], seg_ref[...])


def _wkv_post(o, r, k2, v, proj, rb0, rp):
    ns = o.shape[0]
    w512 = HB * 64
    row = pl.BlockSpec((ns, w512), lambda h: (0, h))
    vec = pl.BlockSpec((1, w512), lambda h: (0, h))
    return pl.pallas_call(
        _wkv_post_kernel, grid=(RWKV_HEADS // HB,),
        in_specs=[row, row, row, row, pl.BlockSpec((ns, w512), lambda h: (rb0, _C_G // w512 + h)),
                  vec, vec, vec, pl.BlockSpec((w512, w512), lambda h: (0, 0))],
        out_specs=row, out_shape=jax.ShapeDtypeStruct((ns, RWKV_WIDTH), BF16),
        compiler_params=_cp(("parallel",)), name="wkv_sample_post",
    )(o, r, k2, v, proj, rp["lnw"], rp["lnb"], rp["rk"], rp["seg"])


def _ret_pre_kernel(q_ref, k_ref, cos_ref, sin_ref, qo_ref, kh_ref, kl_ref):
    cos = cos_ref[0:1, :]
    sin = sin_ref[0:1, :]
    qo_ref[...] = _rotate(q_ref[...], cos, sin) * (RET_QK_DIM ** -0.5)
    hi, lo = _split(_rotate(k_ref[...], cos, sin).T, 2)
    kh_ref[...] = hi
    kl_ref[...] = lo


def _ret_pre(proj, rb0, ns, cos, sin):
    return pl.pallas_call(
        _ret_pre_kernel, grid=(RET_HEADS,),
        in_specs=[pl.BlockSpec((ns, RET_QK_DIM), lambda h: (rb0, h)),
                  pl.BlockSpec((ns, RET_QK_DIM), lambda h: (rb0, RET_HEADS + h)),
                  pl.BlockSpec((8, 128), lambda h: (0, 0)), pl.BlockSpec((8, 128), lambda h: (0, 0))],
        out_specs=[pl.BlockSpec((ns, RET_QK_DIM), lambda h: (0, h)),
                   pl.BlockSpec((RET_QK_DIM, ns), lambda h: (h, 0)),
                   pl.BlockSpec((RET_QK_DIM, ns), lambda h: (h, 0))],
        out_shape=[jax.ShapeDtypeStruct((ns, RET_QK_WIDTH), F32),
                   jax.ShapeDtypeStruct((RET_QK_WIDTH, ns), BF16),
                   jax.ShapeDtypeStruct((RET_QK_WIDTH, ns), BF16)],
        compiler_params=_cp(("parallel",)), name="ret_sample_pre",
    )(proj, proj, cos, sin)


def _ret_state_kernel(gd_ref, s_ref, kh_ref, kl_ref, q_ref, v_ref, so_ref, y_ref, kb, ysc):
    b = pl.program_id(0)
    e = _onehot_cols(b, LANE)
    kb[...] = (jnp.dot(kh_ref[...], e, preferred_element_type=F32)
               + jnp.dot(kl_ref[...], e, preferred_element_type=F32))

    def head(h, carry):
        gd = gd_ref[h]
        vrow = v_ref[0, pl.ds(h, 1), :]
        for rc in range(RET_QK_DIM // 64):
            kcol = kb[pl.ds(pl.multiple_of(h * RET_QK_DIM + rc * 64, 64), 64), :]
            for lc in range(RET_V_DIM // LANE):
                rs, ls = slice(rc * 64, (rc + 1) * 64), slice(lc * LANE, (lc + 1) * LANE)
                so_ref[0, h, rs, ls] = s_ref[0, h, rs, ls] * gd + kcol * vrow[:, ls]
        qrow = q_ref[0, pl.ds(h, 1), :]
        y8 = _dot(jnp.broadcast_to(qrow, (8, RET_QK_DIM)), so_ref[0, h])
        ysc[pl.ds(h, 1), :] = y8[0:1, :]
        return carry

    lax.fori_loop(0, RET_HEADS, head, 0)
    y_ref[0] = ysc[...]


def _ret_state(s, kh, kl, q, v):
    ns = s.shape[0]
    gd = jnp.exp(_ret_log_g())
    sblk = pl.BlockSpec((1, RET_HEADS, RET_QK_DIM, RET_V_DIM), lambda i, g: (i, 0, 0, 0))
    full = pl.BlockSpec((RET_QK_WIDTH, ns), lambda i, g: (0, 0))
    grid_spec = pltpu.PrefetchScalarGridSpec(
        num_scalar_prefetch=1, grid=(ns,),
        in_specs=[sblk, full, full,
                  pl.BlockSpec((1, RET_HEADS, RET_QK_DIM), lambda i, g: (i, 0, 0)),
                  pl.BlockSpec((1, RET_HEADS, RET_V_DIM), lambda i, g: (i, 0, 0))],
        out_specs=[sblk, pl.BlockSpec((1, RET_HEADS, RET_V_DIM), lambda i, g: (i, 0, 0))],
        scratch_shapes=[pltpu.VMEM((RET_QK_WIDTH, LANE), F32), pltpu.VMEM((RET_HEADS, RET_V_DIM), F32)],
    )
    return pl.pallas_call(
        _ret_state_kernel, grid_spec=grid_spec,
        out_shape=[jax.ShapeDtypeStruct(s.shape, F32), jax.ShapeDtypeStruct((ns, RET_HEADS, RET_V_DIM), F32)],
        compiler_params=_cp(("parallel",)), name="ret_sample_state",
    )(gd, s, kh, kl, q, v)


def _ret_post_kernel(y_ref, g_ref, gnw_ref, o_ref):
    y = y_ref[...]
    mu = jnp.mean(y, -1, keepdims=True)
    d = y - mu
    var = jnp.mean(d * d, -1, keepdims=True)
    o = d * lax.rsqrt(var + RET_GN_EPS) * gnw_ref[...]
    o_ref[...] = (o * _silu(g_ref[...])).astype(BF16)


def _ret_post(y, proj, rb0, gn_w):
    ns = y.shape[0]
    blk = pl.BlockSpec((ns, RET_V_DIM), lambda h: (0, h))
    return pl.pallas_call(
        _ret_post_kernel, grid=(RET_HEADS,),
        in_specs=[blk, pl.BlockSpec((ns, RET_V_DIM), lambda h: (rb0, (2 * RET_QK_WIDTH + RET_WIDTH) // RET_V_DIM + h)),
                  pl.BlockSpec((1, RET_V_DIM), lambda h: (0, h))],
        out_specs=blk, out_shape=jax.ShapeDtypeStruct((ns, RET_WIDTH), BF16),
        compiler_params=_cp(("parallel",)), name="ret_sample_post",
    )(y, proj, gn_w.reshape(1, -1))


def _ab_layer(x, xb, nb, l, ns, conv_s, ssm_s, shift_s, wkv_s, w_in, sp, rp, w_out, ln_w, ln_b):
    mp = nb * l
    rb0 = mp // 128
    w_main = jnp.concatenate([w_in[:, :10240], w_in[:, 10304:22592], w_in[:, 22848:26944],
                              w_in[:, 22592:22848]], axis=1).astype(BF16)
    w_dt = w_in[:, 10240:10368].astype(BF16)
    proj = _matmul(xb, w_main, "ab_in_proj")
    pdt = _matmul(xb, w_dt, "ab_dt_proj")[:, :SSD_HEADS]
    m = proj.shape[0]
    dt3 = pdt.reshape(m, SSD_GROUPS, SSD_HPG)
    dtc = dt3.transpose(1, 0, 2)
    dtr = dt3.transpose(1, 2, 0)

    ya_p, ssm_p = _ssd_prompt(proj, dtc, dtr, nb, l, sp)
    yb_p, wkv_p = _rwkv_prompt(proj, nb, l, rp)
    pp = proj[:mp].reshape(nb, l, AB_MAIN)
    conv_p = pp[:, l - (SSD_CONV - 1):, _C_XS:_C_R]
    shift_p = jnp.concatenate([pp[:, l - 1:, _C_R:_C_G], pp[:, l - 1:, _C_WD:]], axis=-1)

    ps = proj[mp:]
    xa, ba, ca, xh, xl, dt_s, dec_s = _ssd_pre(proj, rb0, ns, conv_s.transpose(1, 0, 2), dtc, sp)
    flat = lambda t: t.transpose(1, 0, 2).reshape(ns * SSD_HEADS)
    ssm_n, y_s = _ssm_state(flat(dt_s), flat(dec_s), ssm_s, xh, xl, ba.transpose(1, 0, 2), ca.transpose(1, 0, 2))
    ya_s = _ssd_post(y_s.reshape(ns, SSD_WIDTH), xa, proj, rb0, sp)
    conv_n = jnp.concatenate([conv_s[:, 1:], ps[:, None, _C_XS:_C_R]], axis=1)

    r_s, w_s, k_s, b_s, kk_s, v_s, vh, vl = _wkv_pre(proj, rb0, shift_s.reshape(ns, SHIFT_DIM), rp)
    h3 = lambda t: t.reshape(ns, RWKV_HEADS, 64)
    wkv_n, o_s = _wkv_state(wkv_s, vh, vl, h3(r_s), h3(w_s), h3(k_s), h3(b_s), h3(kk_s))
    yb_s = _wkv_post(o_s.reshape(ns, RWKV_WIDTH), r_s, k_s, v_s, proj, rb0, rp)
    shift_n = jnp.concatenate([ps[:, None, _C_R:_C_G], ps[:, None, _C_WD:]], axis=-1)

    ycat = jnp.concatenate([jnp.concatenate([ya_p, yb_p], axis=1), jnp.concatenate([ya_s, yb_s], axis=1)], axis=0)
    out = _matmul(ycat, w_out.astype(BF16), "ab_out_proj")
    x_new, xb_new = _deepnorm(x, out, ln_w, ln_b, "ab_deepnorm")
    return x_new, xb_new, (conv_p, ssm_p, shift_p, wkv_p), (conv_n, ssm_n, shift_n, wkv_n)


def _ret_layer(x, xb, nb, l, ns, ret_s, w_in, gn_w, w_out, ln_w, ln_b):
    mp = nb * l
    rb0 = mp // 128
    proj = _matmul(xb, w_in.astype(BF16), "ret_in_proj")
    cos, sin = _trig(jnp.arange(l))
    cos_s, sin_s = _trig(jnp.full((8,), PAST_LEN))
    y_p, ret_p = _ret_prompt(proj, cos, sin, gn_w, nb, l)

    q_s, kh, kl = _ret_pre(proj, rb0, ns, cos_s, sin_s)
    v_s = proj[mp:, 2 * RET_QK_WIDTH:2 * RET_QK_WIDTH + RET_WIDTH].reshape(ns, RET_HEADS, RET_V_DIM)
    ret_n, o_s = _ret_state(ret_s, kh, kl, q_s.reshape(ns, RET_HEADS, RET_QK_DIM), v_s)
    y_s = _ret_post(o_s.reshape(ns, RET_WIDTH), proj, rb0, gn_w)

    out = _matmul(jnp.concatenate([y_p, y_s], axis=0), w_out.astype(BF16), "ret_out_proj")
    x_new, xb_new = _deepnorm(x, out, ln_w, ln_b, "ret_deepnorm")
    return x_new, xb_new, ret_p, ret_n


def kernel(x_prompt, x_sample, state_conv, state_ssm, state_shift, state_wkv, state_ret, ab_w_in, ssd_conv_w, ssd_conv_b, ssd_dt_bias, ssd_a_log, ssd_d, ssd_norm_w, rwkv_mu, rwkv_w0, rwkv_w_up, rwkv_a0, rwkv_a_up, rwkv_k_k, rwkv_k_a, rwkv_r_k, rwkv_lnx_w, rwkv_lnx_b, ab_w_out, ab_ln_w, ab_ln_b, ret_w_in, ret_gn_w, ret_w_out, ret_ln_w, ret_ln_b):
    nb, l, d = x_prompt.shape
    ns = x_sample.shape[0]
    assert x_sample.shape[1] == 1 and l % CHUNK == 0 and ns % LANE == 0 and ns == LANE
    mp = nb * l
    x = jnp.concatenate([x_prompt.reshape(mp, d), x_sample.reshape(ns, d)], axis=0)
    xb = x.astype(BF16)

    sp = _ssd_params(ssd_conv_w[0], ssd_conv_b[0], ssd_dt_bias[0], ssd_a_log[0], ssd_d[0], ssd_norm_w[0])
    rp = _rwkv_params(rwkv_mu[0], rwkv_w0[0], rwkv_w_up[0], rwkv_a0[0], rwkv_a_up[0], rwkv_k_k[0], rwkv_k_a[0],
                      rwkv_r_k[0], rwkv_lnx_w[0], rwkv_lnx_b[0])
    x, xb, pst, sst = _ab_layer(x, xb, nb, l, ns, state_conv[0], state_ssm[0], state_shift[0], state_wkv[0],
                                ab_w_in[0], sp, rp, ab_w_out[0], ab_ln_w[0], ab_ln_b[0])
    x, xb, ret_p, ret_n = _ret_layer(x, xb, nb, l, ns, state_ret[0], ret_w_in[0], ret_gn_w[0], ret_w_out[0],
                                     ret_ln_w[0], ret_ln_b[0])
    y_prompt = x[:mp].reshape(nb, l, d)
    y_sample = x[mp:].reshape(ns, 1, d)
    st = lambda t: t[None]
    return (y_prompt, y_sample,
            st(pst[0]), st(pst[1]), st(pst[2]), st(pst[3]), st(ret_p),
            st(sst[0]), st(sst[1]), st(sst[2]), st(sst[3]), st(ret_n))
```

```python
import functools
import math

import jax
import jax.numpy as jnp
import numpy as np
from jax import lax
from jax.experimental import pallas as pl
from jax.experimental.pallas import tpu as pltpu

F32 = jnp.float32
BF16 = jnp.bfloat16

D_MODEL = 4096
DEPTH = 2
PAST_LEN = 16384
SSD_WIDTH = 4096
SSD_HEAD_DIM = 64
SSD_HEADS = 64
SSD_GROUPS = 8
SSD_HPG = 8
SSD_STATE = 128
SSD_CONV = 4
SSD_CONV_DIM = SSD_WIDTH + 2 * SSD_GROUPS * SSD_STATE
RWKV_WIDTH = 4096
RWKV_HEAD_DIM = 64
RWKV_HEADS = 64
LORA = 128
SHIFT_DIM = 3 * RWKV_WIDTH + 2 * LORA
RET_HEADS = 16
RET_QK_DIM = 256
RET_V_DIM = 512
RET_QK_WIDTH = 4096
RET_WIDTH = 8192
ROPE_BASE = 10000.0
CHUNK = 128
ALPHA = (2 * DEPTH) ** 0.25
LN_EPS = 1e-5
RMS_EPS = 1e-5
RWKV_GN_EPS = 64e-5
RET_GN_EPS = 1e-6

LANE = 128
VMEM_LIMIT = 56 * 1024 * 1024
WKV_CHUNK = 64
RW_ROWS = 128
HB = 8
RET_HB = 2
SSM_BT = 2
WKV_HPS = 2
WKV_UNROLL = 8

_C_Z, _C_XS, _C_B, _C_C = 0, 4096, 8192, 9216
_C_R, _C_K, _C_V, _C_WD, _C_AD, _C_G = 10240, 14336, 18432, 22528, 22656, 22784
AB_MAIN = 26880
AB_DT0 = 10240
AB_TN = 1280


def _cp(sem):
    return pltpu.CompilerParams(dimension_semantics=sem, vmem_limit_bytes=VMEM_LIMIT)


def _silu(x):
    return x * jax.nn.sigmoid(x)


def _softplus(x):
    return jnp.maximum(x, 0.0) + jnp.log1p(jnp.exp(-jnp.abs(x)))


def _dot(a, b):
    return jnp.dot(a.astype(BF16), b.astype(BF16), preferred_element_type=F32)


def _dot_nt(a, b):
    return lax.dot_general(a.astype(BF16), b.astype(BF16), (((1,), (1,)), ((), ())),
                           preferred_element_type=F32)


def _dot_tn(a, b):
    return lax.dot_general(a.astype(BF16), b.astype(BF16), (((0,), (0,)), ((), ())),
                           preferred_element_type=F32)


def _split(x, n):
    parts, r = [], x
    for _ in range(n):
        h = r.astype(BF16)
        parts.append(h)
        r = r - h.astype(F32)
    return parts


def _dot01(m01, x, n=3):
    return sum(jnp.dot(m01, p, preferred_element_type=F32) for p in _split(x, n))


def _dot01_r(x, m01, n=2):
    return sum(jnp.dot(p, m01, preferred_element_type=F32) for p in _split(x, n))


def _onehot_cols(b, n):
    rows = lax.broadcasted_iota(jnp.int32, (LANE, n), 0)
    return jnp.where(rows == b, 1.0, 0.0).astype(BF16)


def _mm_kernel(x_ref, w_ref, o_ref):
    o_ref[...] = jnp.dot(x_ref[...], w_ref[...], preferred_element_type=F32)


def _pick_tile(n, prefs):
    for t in prefs:
        if n % t == 0:
            return t
    return n


def _matmul(x, w, name):
    m, k = x.shape
    n = w.shape[1]
    tm = _pick_tile(m, (640, 512, 256, 128))
    tn = _pick_tile(n, (1280, 1024, 512, 256, 128) if k <= 4096 else (512, 256, 128))
    return pl.pallas_call(
        _mm_kernel,
        grid=(n // tn, m // tm),
        in_specs=[pl.BlockSpec((tm, k), lambda j, i: (i, 0)),
                  pl.BlockSpec((k, tn), lambda j, i: (0, j))],
        out_specs=pl.BlockSpec((tm, tn), lambda j, i: (i, j)),
        out_shape=jax.ShapeDtypeStruct((m, n), F32),
        compiler_params=_cp(("parallel", "parallel")),
        name=name,
    )(x, w)


def _mm_wt_kernel(st_ref, x_ref, wt_ref, o_ref):
    del st_ref
    o_ref[...] = lax.dot_general(x_ref[...], wt_ref[...], (((1,), (1,)), ((), ())), preferred_element_type=F32)


def _matmul_wt(x, wt, row_starts, tn, name):
    m, k = x.shape
    nt = len(row_starts)
    tm = _pick_tile(m, (640, 512, 256, 128))
    starts = jnp.asarray(row_starts, jnp.int32)
    grid_spec = pltpu.PrefetchScalarGridSpec(
        num_scalar_prefetch=1,
        grid=(nt, m // tm),
        in_specs=[pl.BlockSpec((tm, k), lambda j, i, st: (i, 0)),
                  pl.BlockSpec((pl.Element(tn), pl.Element(k)), lambda j, i, st: (pl.multiple_of(st[j], 64), 0))],
        out_specs=pl.BlockSpec((tm, tn), lambda j, i, st: (i, j)),
    )
    return pl.pallas_call(
        _mm_wt_kernel, grid_spec=grid_spec,
        out_shape=jax.ShapeDtypeStruct((m, nt * tn), F32),
        compiler_params=_cp(("parallel", "parallel")),
        name=name,
    )(starts, x, wt)


def _mm2_kernel(a_ref, b_ref, w_ref, o_ref):
    ka = a_ref.shape[1]
    o_ref[...] = (jnp.dot(a_ref[...], w_ref[0:ka, :], preferred_element_type=F32)
                  + jnp.dot(b_ref[...], w_ref[ka:, :], preferred_element_type=F32))


def _matmul2(a, b, w, name):
    m, ka = a.shape
    kb = b.shape[1]
    n = w.shape[1]
    tm = _pick_tile(m, (640, 512, 256, 128))
    tn = _pick_tile(n, (512, 256, 128))
    return pl.pallas_call(
        _mm2_kernel,
        grid=(n // tn, m // tm),
        in_specs=[pl.BlockSpec((tm, ka), lambda j, i: (i, 0)),
                  pl.BlockSpec((tm, kb), lambda j, i: (i, 0)),
                  pl.BlockSpec((ka + kb, tn), lambda j, i: (0, j))],
        out_specs=pl.BlockSpec((tm, tn), lambda j, i: (i, j)),
        out_shape=jax.ShapeDtypeStruct((m, n), F32),
        compiler_params=_cp(("parallel", "parallel")),
        name=name,
    )(a, b, w)


def _ln_kernel(x_ref, o_ref, w_ref, b_ref, y_ref, yb_ref):
    h = ALPHA * x_ref[...] + o_ref[...]
    mu = jnp.mean(h, -1, keepdims=True)
    d = h - mu
    var = jnp.mean(d * d, -1, keepdims=True)
    y = d * lax.rsqrt(var + LN_EPS) * w_ref[...] + b_ref[...]
    y_ref[...] = y
    yb_ref[...] = y.astype(BF16)


def _deepnorm(x, o, w, b, name):
    m, d = x.shape
    tm = _pick_tile(m, (128, 64, 8))
    row = pl.BlockSpec((tm, d), lambda i: (i, 0))
    vec = pl.BlockSpec((1, d), lambda i: (0, 0))
    return pl.pallas_call(
        _ln_kernel,
        grid=(m // tm,),
        in_specs=[row, row, vec, vec],
        out_specs=[row, row],
        out_shape=[jax.ShapeDtypeStruct((m, d), F32), jax.ShapeDtypeStruct((m, d), BF16)],
        compiler_params=_cp(("parallel",)),
        name=name,
    )(x, o, w.reshape(1, d), b.reshape(1, d))


def _ssd_prompt_kernel(z_ref, xs_ref, b_ref, c_ref, dtc_ref, dtr_ref,
                       cwx_ref, cwb_ref, cwc_ref, cbx_ref, cbb_ref, cbc_ref,
                       dtbc_ref, dtbr_ref, alc_ref, alr_ref, dsk_ref, nw_ref, tri_ref,
                       y_ref, s_ref, bufx, bufb, bufc):
    L = CHUNK
    c = pl.program_id(2)

    @pl.when(c == 0)
    def _init():
        for buf in (bufx, bufb, bufc):
            buf[0:8, :] = jnp.zeros((8, buf.shape[1]), F32)
        s_ref[...] = jnp.zeros(s_ref.shape, F32)

    def conv(u_ref, buf, w_ref, bias_ref):
        buf[8:8 + L, :] = u_ref[...]
        acc = bias_ref[...] + buf[5:5 + L, :] * w_ref[0:1, :]
        for k in range(1, SSD_CONV):
            acc = acc + buf[5 + k:5 + k + L, :] * w_ref[k:k + 1, :]
        buf[0:8, :] = buf[L:L + 8, :]
        return _silu(acc)

    xs = conv(xs_ref, bufx, cwx_ref, cbx_ref)
    bm = conv(b_ref, bufb, cwb_ref, cbb_ref)
    cm = conv(c_ref, bufc, cwc_ref, cbc_ref)
    dtc = _softplus(dtc_ref[0] + dtbc_ref[0])
    dtr = _softplus(dtr_ref[0] + dtbr_ref[0])
    adt_c = dtc * (-jnp.exp(alc_ref[0]))
    adt_r = dtr * (-jnp.exp(alr_ref[0]))
    tri = tri_ref[...]
    cum_c = _dot01(tri, adt_c)
    cum_r = sum(lax.dot_general(p, tri, (((1,), (1,)), ((), ())), preferred_element_type=F32)
                for p in _split(adt_r, 3))

    li = lax.broadcasted_iota(jnp.int32, (L, L), 0)
    si = lax.broadcasted_iota(jnp.int32, (L, L), 1)
    causal = li >= si
    cb = _dot_nt(cm, bm)
    heads = range(SSD_HPG)
    cc = [cum_c[:, r:r + 1] for r in heads]
    clast = [cc[r][L - 1:L, :] for r in heads]
    xdt = [xs[:, r * 64:(r + 1) * 64] * dtc[:, r:r + 1] for r in heads]
    s_old = [s_ref[0, r] for r in heads]
    mm = [cb * jnp.exp(jnp.where(causal, cc[r] - cum_r[r:r + 1, :], -jnp.inf)) for r in heads]
    y_in = [_dot(mm[r], xdt[r]) for r in heads]
    y_st = [_dot_nt(cm, s_old[r]) for r in heads]
    s_in = [_dot_tn(xdt[r] * jnp.exp(clast[r] - cc[r]), bm) for r in heads]
    for r in heads:
        s_ref[0, r] = s_old[r] * jnp.exp(clast[r]) + s_in[r]
    y = jnp.concatenate([y_in[r] + y_st[r] * jnp.exp(cc[r]) for r in heads], axis=1) + xs * dsk_ref[...]
    y = y * _silu(z_ref[...])
    y = y * lax.rsqrt(jnp.mean(y * y, -1, keepdims=True) + RMS_EPS) * nw_ref[...]
    y_ref[...] = y.astype(BF16)


def _ssd_params(conv_w, conv_b, dt_bias, a_log, d_skip, norm_w):
    g = SSD_GROUPS
    return dict(
        cwx=conv_w[:, :4096], cwb=conv_w[:, 4096:5120], cwc=conv_w[:, 5120:],
        cbx=conv_b[:4096].reshape(1, -1), cbb=conv_b[4096:5120].reshape(1, -1), cbc=conv_b[5120:].reshape(1, -1),
        dtbc=dt_bias.reshape(g, 1, 8), dtbr=dt_bias.reshape(g, 8, 1),
        alc=a_log.reshape(g, 1, 8), alr=a_log.reshape(g, 8, 1),
        dsk=jnp.repeat(d_skip, SSD_HEAD_DIM).reshape(1, -1), nw=norm_w.reshape(1, -1))


def _ssd_prompt(proj, dtc, dtr, nb, l, sp):
    nc = l // CHUNK
    rb = lambda b, g, c: b * nc + c
    tri = jnp.tril(jnp.ones((CHUNK, CHUNK), BF16))
    in_specs = [
        pl.BlockSpec((CHUNK, 512), lambda b, g, c: (rb(b, g, c), _C_Z // 512 + g)),
        pl.BlockSpec((CHUNK, 512), lambda b, g, c: (rb(b, g, c), _C_XS // 512 + g)),
        pl.BlockSpec((CHUNK, 128), lambda b, g, c: (rb(b, g, c), _C_B // 128 + g)),
        pl.BlockSpec((CHUNK, 128), lambda b, g, c: (rb(b, g, c), _C_C // 128 + g)),
        pl.BlockSpec((1, CHUNK, 8), lambda b, g, c: (g, rb(b, g, c), 0)),
        pl.BlockSpec((1, 8, CHUNK), lambda b, g, c: (g, 0, rb(b, g, c))),
        pl.BlockSpec((SSD_CONV, 512), lambda b, g, c: (0, g)),
        pl.BlockSpec((SSD_CONV, 128), lambda b, g, c: (0, g)),
        pl.BlockSpec((SSD_CONV, 128), lambda b, g, c: (0, g)),
        pl.BlockSpec((1, 512), lambda b, g, c: (0, g)),
        pl.BlockSpec((1, 128), lambda b, g, c: (0, g)),
        pl.BlockSpec((1, 128), lambda b, g, c: (0, g)),
        pl.BlockSpec((1, 1, 8), lambda b, g, c: (g, 0, 0)),
        pl.BlockSpec((1, 8, 1), lambda b, g, c: (g, 0, 0)),
        pl.BlockSpec((1, 1, 8), lambda b, g, c: (g, 0, 0)),
        pl.BlockSpec((1, 8, 1), lambda b, g, c: (g, 0, 0)),
        pl.BlockSpec((1, 512), lambda b, g, c: (0, g)),
        pl.BlockSpec((1, 512), lambda b, g, c: (0, g)),
        pl.BlockSpec((CHUNK, CHUNK), lambda b, g, c: (0, 0)),
    ]
    out_specs = [pl.BlockSpec((CHUNK, 512), lambda b, g, c: (rb(b, g, c), g)),
                 pl.BlockSpec((1, SSD_HPG, SSD_HEAD_DIM, SSD_STATE), lambda b, g, c: (b, g, 0, 0))]
    return pl.pallas_call(
        _ssd_prompt_kernel,
        grid=(nb, SSD_GROUPS, nc),
        in_specs=in_specs,
        out_specs=out_specs,
        out_shape=[jax.ShapeDtypeStruct((proj.shape[0], SSD_WIDTH), BF16),
                   jax.ShapeDtypeStruct((nb, SSD_HEADS, SSD_HEAD_DIM, SSD_STATE), F32)],
        scratch_shapes=[pltpu.VMEM((CHUNK + 8, 512), F32), pltpu.VMEM((CHUNK + 8, 128), F32),
                        pltpu.VMEM((CHUNK + 8, 128), F32)],
        compiler_params=_cp(("parallel", "parallel", "arbitrary")),
        name="ssd_prompt",
    )(proj, proj, proj, proj, dtc, dtr, sp["cwx"], sp["cwb"], sp["cwc"], sp["cbx"], sp["cbb"], sp["cbc"],
      sp["dtbc"], sp["dtbr"], sp["alc"], sp["alr"], sp["dsk"], sp["nw"], tri)


def _rwkv_mix(rm, km, vm, wdm, adm, w0, wup, a0, aup, k_k, k_a, seg):
    wlog = -_softplus(-(w0 + _dot(jnp.tanh(wdm), wup))) - 0.5
    logw = -jnp.exp(wlog)
    aa = jax.nn.sigmoid(a0 + _dot(adm, aup))
    kkr = km * k_k
    kk = kkr * lax.rsqrt(jnp.maximum(_dot01_r(kkr * kkr, seg), 1e-24))
    k2 = km * (1.0 + (aa - 1.0) * k_a)
    return logw, kk, k2, kk * aa


def _rwkv_out(o, rm, k2, vm, g, lnw, lnb, rk, seg):
    inv = 1.0 / RWKV_HEAD_DIM
    mean = _dot01_r(o, seg) * inv
    d = o - mean
    var = _dot01_r(d * d, seg) * inv
    on = d * lax.rsqrt(var + RWKV_GN_EPS) * lnw + lnb
    bonus = _dot01_r(rm * k2 * rk, seg) * vm
    return ((on + bonus) * _silu(g)).astype(BF16)


def _rwkv_prompt_kernel(r_ref, k_ref, v_ref, g_ref, wd_ref, ad_ref,
                        mur_ref, muk_ref, muv_ref, muwd_ref, muad_ref,
                        w0_ref, wup_ref, a0_ref, aup_ref, kk_ref, ka_ref, lnw_ref, lnb_ref, rk_ref,
                        seg_ref, tri_ref,
                        y_ref, s_ref, cr, ck, cv, cwd, cad):
    R, C = RW_ROWS, WKV_CHUNK
    c = pl.program_id(2)

    @pl.when(c == 0)
    def _init():
        for buf in (cr, ck, cv, cwd, cad):
            buf[...] = jnp.zeros(buf.shape, F32)
        s_ref[...] = jnp.zeros(s_ref.shape, F32)

    row0 = lax.broadcasted_iota(jnp.int32, (R, 1), 0) == 0

    def shift(x_ref, carry, mu_ref):
        x = x_ref[...]
        prev = jnp.where(row0, carry[0:1, :], pltpu.roll(x, 1, 0))
        carry[0:1, :] = x[R - 1:R, :]
        return x + (prev - x) * mu_ref[...]

    rm = shift(r_ref, cr, mur_ref)
    km = shift(k_ref, ck, muk_ref)
    vm = shift(v_ref, cv, muv_ref)
    wdm = shift(wd_ref, cwd, muwd_ref)
    adm = shift(ad_ref, cad, muad_ref)
    seg = seg_ref[...]
    logw, kk, k2, bv = _rwkv_mix(rm, km, vm, wdm, adm, w0_ref[...], wup_ref[...], a0_ref[...], aup_ref[...],
                                 kk_ref[...], ka_ref[...], seg)

    tri = tri_ref[...]
    li = lax.broadcasted_iota(jnp.int32, (C, C), 0)
    si = lax.broadcasted_iota(jnp.int32, (C, C), 1)
    strict = li > si
    incl = li >= si
    eye = jnp.where(li == si, 1.0, 0.0)
    nsc = R // C
    hs = lambda a, h: a[:, h * 64:(h + 1) * 64]
    prep = []
    for sc in range(nsc):
        rows = slice(sc * C, (sc + 1) * C)
        lw = logw[rows]
        cs = _dot01(tri, lw)
        cl = cs[C - 1:C, :]
        e_tail = jnp.exp(cl - cs)
        e_neg = jnp.exp(-cs)
        prep.append(dict(
            bt=kk[rows] * jnp.exp(cs - lw),
            bb=bv[rows] * e_neg,
            kt=k2[rows] * e_neg,
            rt=rm[rows] * jnp.exp(cs),
            bh=bv[rows] * e_tail,
            kh=k2[rows] * e_tail,
            pc=jnp.exp(cl), v=vm[rows]))
    keys = [(sc, h) for sc in range(nsc) for h in range(HB)]
    part = lambda name: {k: hs(prep[k[0]][name], k[1]) for k in keys}
    bt, bb, kt, rt, bh, kh, vh = (part(n) for n in ("bt", "bb", "kt", "rt", "bh", "kh", "v"))
    gm = {k: _dot_nt(jnp.concatenate([bt[k], rt[k]], 0), jnp.concatenate([bb[k], kt[k]], 0)) for k in keys}
    lk = {k: jnp.where(strict, gm[k][0:C, C:2 * C], 0.0) for k in keys}
    rb = {k: jnp.where(incl, gm[k][C:2 * C, 0:C], 0.0) for k in keys}
    rkm = {k: jnp.where(incl, gm[k][C:2 * C, C:2 * C], 0.0) for k in keys}
    x = {k: jnp.where(strict, -gm[k][0:C, 0:C], 0.0) for k in keys}
    t = {k: eye + x[k] for k in keys}
    for _ in range(int(math.log2(C)) - 1):
        x = {k: _dot(x[k], x[k]) for k in keys}
        t = {k: t[k] + _dot(t[k], x[k]) for k in keys}
    lkv = {k: _dot(lk[k], vh[k]) for k in keys}
    tt = {k: _dot(t[k], jnp.concatenate([bt[k], lkv[k]], 1)) for k in keys}
    rr = {k: _dot(rb[k], tt[k]) for k in keys}
    rkv = {k: _dot(rkm[k], vh[k]) for k in keys}
    rq = {k: rt[k] - rr[k][:, 0:64] for k in keys}
    yc = {k: rkv[k] - rr[k][:, 64:128] for k in keys}
    mq = {k: _dot_tn(tt[k][:, 0:64], bh[k]) for k in keys}
    nn = {k: _dot_tn(jnp.concatenate([vh[k], -tt[k][:, 64:128]], 0),
                     jnp.concatenate([kh[k], bh[k]], 0)) for k in keys}
    st = [s_ref[0, h] for h in range(HB)]
    o_chunks = []
    for sc in range(nsc):
        ys = [_dot_nt(rq[sc, h], st[h]) + yc[sc, h] for h in range(HB)]
        st = [st[h] * hs(prep[sc]["pc"], h) - _dot(st[h], mq[sc, h]) + nn[sc, h] for h in range(HB)]
        o_chunks.append(jnp.concatenate(ys, 1))
    for h in range(HB):
        s_ref[0, h] = st[h]
    o = jnp.concatenate(o_chunks, 0)
    y_ref[...] = _rwkv_out(o, rm, k2, vm, g_ref[...], lnw_ref[...], lnb_ref[...], rk_ref[...], seg)


def _rwkv_params(mu, w0, w_up, a0, a_up, k_k, k_a, r_k, lnx_w, lnx_b):
    v = lambda t: t.reshape(1, -1)
    return dict(
        mur=v(mu[0:4096]), muk=v(mu[4096:8192]), muv=v(mu[8192:12288]),
        muwd=v(mu[12288:12416]), muad=v(mu[12416:12544]),
        w0=v(w0), wup=w_up.astype(BF16), a0=v(a0), aup=a_up.astype(BF16), kk=v(k_k), ka=v(k_a),
        lnw=v(lnx_w), lnb=v(lnx_b), rk=v(r_k),
        seg=jnp.asarray(np.kron(np.eye(HB), np.ones((64, 64))), BF16))


def _rwkv_prompt(proj, nb, l, rp):
    nr = l // RW_ROWS
    rb = lambda b, h, c: b * nr + c
    w512 = HB * 64
    col = lambda c0: pl.BlockSpec((RW_ROWS, w512), lambda b, h, c: (rb(b, h, c), c0 // w512 + h))
    lora = lambda c0: pl.BlockSpec((RW_ROWS, LORA), lambda b, h, c: (rb(b, h, c), c0 // LORA))
    vec = pl.BlockSpec((1, w512), lambda b, h, c: (0, h))
    vec128 = pl.BlockSpec((1, LORA), lambda b, h, c: (0, 0))
    up = pl.BlockSpec((LORA, w512), lambda b, h, c: (0, h))
    tri = jnp.tril(jnp.ones((WKV_CHUNK, WKV_CHUNK), BF16))
    gate = pl.BlockSpec((pl.Element(RW_ROWS), pl.Element(w512)),
                        lambda b, h, c: (rb(b, h, c) * RW_ROWS, pl.multiple_of(_C_G + h * w512, LANE)))
    in_specs = [col(_C_R), col(_C_K), col(_C_V), gate, lora(_C_WD), lora(_C_AD),
                vec, vec, vec, vec128, vec128,
                vec, up, vec, up, vec, vec, vec, vec, vec,
                pl.BlockSpec((w512, w512), lambda b, h, c: (0, 0)),
                pl.BlockSpec((WKV_CHUNK, WKV_CHUNK), lambda b, h, c: (0, 0))]
    out_specs = [pl.BlockSpec((RW_ROWS, w512), lambda b, h, c: (rb(b, h, c), h)),
                 pl.BlockSpec((1, HB, 64, 64), lambda b, h, c: (b, h, 0, 0))]
    return pl.pallas_call(
        _rwkv_prompt_kernel,
        grid=(nb, RWKV_HEADS // HB, nr),
        in_specs=in_specs,
        out_specs=out_specs,
        out_shape=[jax.ShapeDtypeStruct((proj.shape[0], RWKV_WIDTH), BF16),
                   jax.ShapeDtypeStruct((nb, RWKV_HEADS, 64, 64), F32)],
        scratch_shapes=[pltpu.VMEM((8, w512), F32)] * 3 + [pltpu.VMEM((8, LORA), F32)] * 2,
        compiler_params=_cp(("parallel", "parallel", "arbitrary")),
        name="rwkv_prompt",
    )(proj, proj, proj, proj, proj, proj,
      rp["mur"], rp["muk"], rp["muv"], rp["muwd"], rp["muad"],
      rp["w0"], rp["wup"], rp["a0"], rp["aup"], rp["kk"], rp["ka"], rp["lnw"], rp["lnb"], rp["rk"],
      rp["seg"], tri)


def _trig_kernel(pos_ref, freq_ref, cos_ref, sin_ref):
    ang = pos_ref[...] * freq_ref[...]
    cos_ref[...] = jnp.cos(ang)
    sin_ref[...] = jnp.sin(ang)


def _trig(pos):
    n = pos.shape[0]
    half = RET_QK_DIM // 2
    freq = (ROPE_BASE ** (-jnp.arange(half, dtype=F32) / half)).reshape(1, half)
    posb = jnp.broadcast_to(pos.astype(F32)[:, None], (n, half))
    tn = _pick_tile(n, (256, 128, 8))
    blk = pl.BlockSpec((tn, half), lambda i: (i, 0))
    return pl.pallas_call(
        _trig_kernel, grid=(n // tn,),
        in_specs=[blk, pl.BlockSpec((1, half), lambda i: (0, 0))],
        out_specs=[blk, blk],
        out_shape=[jax.ShapeDtypeStruct((n, half), F32)] * 2,
        name="rope_tables",
    )(posb, freq)


def _rotate(x, cos, sin):
    x1, x2 = x[:, :128], x[:, 128:]
    return jnp.concatenate([x1 * cos - x2 * sin, x1 * sin + x2 * cos], 1)


def _ret_prompt_kernel(lg_ref, q_ref, k_ref, v_ref, g_ref, cos_ref, sin_ref, gnw_ref,
                       y_ref, s_ref):
    L = CHUNK
    hg = pl.program_id(1)
    c = pl.program_id(2)

    @pl.when(c == 0)
    def _init():
        s_ref[...] = jnp.zeros(s_ref.shape, F32)

    cos = cos_ref[...]
    sin = sin_ref[...]
    li = lax.broadcasted_iota(jnp.int32, (L, L), 0)
    si = lax.broadcasted_iota(jnp.int32, (L, L), 1)
    rel = (li - si).astype(F32)
    causal = li >= si
    icol = lax.broadcasted_iota(jnp.int32, (L, 1), 0).astype(F32)
    heads = range(RET_HB)
    lg = [lg_ref[hg * RET_HB + j] for j in heads]
    qr = [(_rotate(q_ref[:, j * 256:(j + 1) * 256], cos, sin) * (RET_QK_DIM ** -0.5)).astype(BF16) for j in heads]
    kr = [_rotate(k_ref[:, j * 256:(j + 1) * 256], cos, sin) for j in heads]
    v = [v_ref[:, j * 512:(j + 1) * 512].astype(BF16) for j in heads]
    s0 = [s_ref[0, j] for j in heads]
    qk = [_dot_nt(qr[j], kr[j]) for j in heads]
    y_st = [_dot(qr[j], s0[j]) for j in heads]
    s_in = [_dot_tn(kr[j] * jnp.exp((L - 1.0 - icol) * lg[j]), v[j]) for j in heads]
    sc = [qk[j] * jnp.exp(jnp.where(causal, rel * lg[j], -jnp.inf)) for j in heads]
    y_in = [_dot(sc[j], v[j]) for j in heads]
    outs = []
    for j in heads:
        s_ref[0, j] = s0[j] * jnp.exp(L * lg[j]) + s_in[j]
        y = y_in[j] + y_st[j] * jnp.exp((icol + 1.0) * lg[j])
        mu = jnp.mean(y, -1, keepdims=True)
        d = y - mu
        var = jnp.mean(d * d, -1, keepdims=True)
        outs.append(d * lax.rsqrt(var + RET_GN_EPS))
    o = jnp.concatenate(outs, 1) * gnw_ref[...]
    y_ref[...] = (o * _silu(g_ref[...])).astype(BF16)


def _ret_log_g():
    return jnp.log1p(-jnp.exp2(-5.0 - jnp.arange(RET_HEADS, dtype=F32)))


def _ret_prompt(proj, cos, sin, gn_w, nb, l):
    nc = l // CHUNK
    rb = lambda b, h, c, lg: b * nc + c
    wq, wv = RET_HB * RET_QK_DIM, RET_HB * RET_V_DIM
    grid_spec = pltpu.PrefetchScalarGridSpec(
        num_scalar_prefetch=1,
        grid=(nb, RET_HEADS // RET_HB, nc),
        in_specs=[
            pl.BlockSpec((CHUNK, wq), lambda b, h, c, lg: (rb(b, h, c, lg), h)),
            pl.BlockSpec((CHUNK, wq), lambda b, h, c, lg: (rb(b, h, c, lg), RET_QK_WIDTH // wq + h)),
            pl.BlockSpec((CHUNK, wv), lambda b, h, c, lg: (rb(b, h, c, lg), 2 * RET_QK_WIDTH // wv + h)),
            pl.BlockSpec((CHUNK, wv), lambda b, h, c, lg: (rb(b, h, c, lg), (2 * RET_QK_WIDTH + RET_WIDTH) // wv + h)),
            pl.BlockSpec((CHUNK, 128), lambda b, h, c, lg: (c, 0)),
            pl.BlockSpec((CHUNK, 128), lambda b, h, c, lg: (c, 0)),
            pl.BlockSpec((1, wv), lambda b, h, c, lg: (0, h)),
        ],
        out_specs=[pl.BlockSpec((CHUNK, wv), lambda b, h, c, lg: (rb(b, h, c, lg), h)),
                   pl.BlockSpec((1, RET_HB, RET_QK_DIM, RET_V_DIM), lambda b, h, c, lg: (b, h, 0, 0))],
    )
    return pl.pallas_call(
        _ret_prompt_kernel,
        grid_spec=grid_spec,
        out_shape=[jax.ShapeDtypeStruct((proj.shape[0], RET_WIDTH), BF16),
                   jax.ShapeDtypeStruct((nb, RET_HEADS, RET_QK_DIM, RET_V_DIM), F32)],
        compiler_params=_cp(("parallel", "parallel", "arbitrary")),
        name="ret_prompt",
    )(_ret_log_g(), proj, proj, proj, proj, cos, sin, gn_w.reshape(1, -1))


def _ssd_pre_kernel(xs_ref, b_ref, c_ref, csx_ref, csb_ref, csc_ref, dtc_ref,
                    cwx_ref, cwb_ref, cwc_ref, cbx_ref, cbb_ref, cbc_ref, dtb_ref, al_ref,
                    xa_ref, ba_ref, ca_ref, xh_ref, xl_ref, dt_ref, dec_ref):
    def conv(u_ref, cs_ref, w_ref, bias_ref):
        acc = bias_ref[...] + u_ref[...] * w_ref[SSD_CONV - 1:SSD_CONV, :]
        for k in range(SSD_CONV - 1):
            acc = acc + cs_ref[k] * w_ref[k:k + 1, :]
        return _silu(acc)

    xs = conv(xs_ref, csx_ref, cwx_ref, cbx_ref)
    xa_ref[...] = xs
    ba_ref[0] = conv(b_ref, csb_ref, cwb_ref, cbb_ref)
    ca_ref[0] = conv(c_ref, csc_ref, cwc_ref, cbc_ref)
    hi, lo = _split(xs.T, 2)
    xh_ref[...] = hi
    xl_ref[...] = lo
    dt = _softplus(dtc_ref[0] + dtb_ref[0])
    dt_ref[0] = dt
    dec_ref[0] = jnp.exp(dt * (-jnp.exp(al_ref[0])))


def _ssd_pre(proj, rb0, ns, cs_t, dtc, sp):
    g8 = SSD_GROUPS
    in_specs = [
        pl.BlockSpec((ns, 512), lambda g: (rb0, _C_XS // 512 + g)),
        pl.BlockSpec((ns, 128), lambda g: (rb0, _C_B // 128 + g)),
        pl.BlockSpec((ns, 128), lambda g: (rb0, _C_C // 128 + g)),
        pl.BlockSpec((3, ns, 512), lambda g: (0, 0, g)),
        pl.BlockSpec((3, ns, 128), lambda g: (0, 0, 4096 // 128 + g)),
        pl.BlockSpec((3, ns, 128), lambda g: (0, 0, 5120 // 128 + g)),
        pl.BlockSpec((1, ns, 8), lambda g: (g, rb0, 0)),
        pl.BlockSpec((SSD_CONV, 512), lambda g: (0, g)),
        pl.BlockSpec((SSD_CONV, 128), lambda g: (0, g)),
        pl.BlockSpec((SSD_CONV, 128), lambda g: (0, g)),
        pl.BlockSpec((1, 512), lambda g: (0, g)),
        pl.BlockSpec((1, 128), lambda g: (0, g)),
        pl.BlockSpec((1, 128), lambda g: (0, g)),
        pl.BlockSpec((1, 1, 8), lambda g: (g, 0, 0)),
        pl.BlockSpec((1, 1, 8), lambda g: (g, 0, 0)),
    ]
    out_specs = [
        pl.BlockSpec((ns, 512), lambda g: (0, g)),
        pl.BlockSpec((1, ns, 128), lambda g: (g, 0, 0)),
        pl.BlockSpec((1, ns, 128), lambda g: (g, 0, 0)),
        pl.BlockSpec((512, ns), lambda g: (g, 0)),
        pl.BlockSpec((512, ns), lambda g: (g, 0)),
        pl.BlockSpec((1, ns, 8), lambda g: (g, 0, 0)),
        pl.BlockSpec((1, ns, 8), lambda g: (g, 0, 0)),
    ]
    out_shape = [
        jax.ShapeDtypeStruct((ns, SSD_WIDTH), F32),
        jax.ShapeDtypeStruct((g8, ns, SSD_STATE), F32),
        jax.ShapeDtypeStruct((g8, ns, SSD_STATE), F32),
        jax.ShapeDtypeStruct((SSD_WIDTH, ns), BF16),
        jax.ShapeDtypeStruct((SSD_WIDTH, ns), BF16),
        jax.ShapeDtypeStruct((g8, ns, 8), F32),
        jax.ShapeDtypeStruct((g8, ns, 8), F32),
    ]
    return pl.pallas_call(
        _ssd_pre_kernel, grid=(g8,), in_specs=in_specs, out_specs=out_specs, out_shape=out_shape,
        compiler_params=_cp(("parallel",)), name="ssd_sample_pre",
    )(proj, proj, proj, cs_t, cs_t, cs_t, dtc, sp["cwx"], sp["cwb"], sp["cwc"], sp["cbx"], sp["cbb"], sp["cbc"],
      sp["dtbc"], sp["alc"])


def _ssm_state_kernel(dt_ref, dec_ref, s_ref, xh_ref, xl_ref, b_ref, c_ref, so_ref, y_ref, xb, ysc):
    i = pl.program_id(0)
    for j in range(SSM_BT):
        b = i * SSM_BT + j
        e = _onehot_cols(b, LANE)
        xb[...] = (jnp.dot(xh_ref[...], e, preferred_element_type=F32)
                   + jnp.dot(xl_ref[...], e, preferred_element_type=F32))

        def group(g, carry):
            brow = b_ref[j, pl.ds(g, 1), :]
            crow = c_ref[j, pl.ds(g, 1), :]
            for r in range(SSD_HPG):
                h = g * SSD_HPG + r
                xcol = xb[pl.ds(pl.multiple_of(h * 64, 64), 64), :]
                so_ref[j, h] = s_ref[j, h] * dec_ref[b * SSD_HEADS + h] + (xcol * dt_ref[b * SSD_HEADS + h]) * brow
            sg = so_ref[j, pl.ds(g * SSD_HPG, SSD_HPG)].reshape(SSD_HPG * 64, SSD_STATE)
            y8 = _dot_nt(jnp.broadcast_to(crow, (8, SSD_STATE)), sg)
            ysc[pl.ds(g, 1), :] = y8[0:1, :]
            return carry

        lax.fori_loop(0, SSD_GROUPS, group, 0)
        y_ref[j] = ysc[...]


def _ssm_state(dt, dec, s, xh, xl, ba, ca):
    ns = s.shape[0]
    bt = SSM_BT
    smem = pl.BlockSpec(memory_space=pltpu.SMEM)
    sblk = pl.BlockSpec((bt, SSD_HEADS, SSD_HEAD_DIM, SSD_STATE), lambda i: (i, 0, 0, 0))
    full = pl.BlockSpec((SSD_WIDTH, ns), lambda i: (0, 0))
    bc = pl.BlockSpec((bt, SSD_GROUPS, SSD_STATE), lambda i: (i, 0, 0))
    return pl.pallas_call(
        _ssm_state_kernel, grid=(ns // bt,),
        in_specs=[smem, smem, sblk, full, full, bc, bc],
        out_specs=[sblk, pl.BlockSpec((bt, SSD_GROUPS, 512), lambda i: (i, 0, 0))],
        out_shape=[jax.ShapeDtypeStruct(s.shape, F32), jax.ShapeDtypeStruct((ns, SSD_GROUPS, 512), F32)],
        scratch_shapes=[pltpu.VMEM((SSD_WIDTH, LANE), F32), pltpu.VMEM((SSD_GROUPS, 512), F32)],
        compiler_params=_cp(("parallel",)), name="ssm_sample_state",
    )(dt, dec, s, xh, xl, ba, ca)


def _ssd_post_kernel(y_ref, xa_ref, z_ref, dsk_ref, nw_ref, dst_ref, o_ref):
    del dst_ref
    y = (y_ref[...] + xa_ref[...] * dsk_ref[...]) * _silu(z_ref[...])
    y = y * lax.rsqrt(jnp.mean(y * y, -1, keepdims=True) + RMS_EPS) * nw_ref[...]
    o_ref[...] = y.astype(BF16)


def _ssd_post(y, xa, proj, rb0, sp, dst):
    ns = y.shape[0]
    blk = pl.BlockSpec((ns, 512), lambda g: (0, g))
    vec = pl.BlockSpec((1, 512), lambda g: (0, g))
    return pl.pallas_call(
        _ssd_post_kernel, grid=(SSD_GROUPS,),
        in_specs=[blk, blk, pl.BlockSpec((ns, 512), lambda g: (rb0, _C_Z // 512 + g)), vec, vec,
                  pl.BlockSpec(memory_space=pl.ANY)],
        out_specs=pl.BlockSpec((ns, 512), lambda g: (rb0, g)),
        out_shape=jax.ShapeDtypeStruct(dst.shape, BF16),
        input_output_aliases={5: 0},
        compiler_params=_cp(("parallel",)), name="ssd_sample_post",
    )(y, xa, proj, sp["dsk"], sp["nw"], dst)


def _wkv_pre_kernel(r_ref, k_ref, v_ref, wd_ref, ad_ref, sr_ref, sk_ref, sv_ref, swd_ref, sad_ref,
                    mur_ref, muk_ref, muv_ref, muwd_ref, muad_ref,
                    w0_ref, wup_ref, a0_ref, aup_ref, kk_ref, ka_ref, seg_ref,
                    ro_ref, ko_ref, vo_ref, rt_ref, wt_ref, kt_ref, bt_ref, kkt_ref, vt_ref):
    mix = lambda x_ref, s_ref, mu_ref: x_ref[...] + (s_ref[...] - x_ref[...]) * mu_ref[...]
    rm = mix(r_ref, sr_ref, mur_ref)
    km = mix(k_ref, sk_ref, muk_ref)
    vm = mix(v_ref, sv_ref, muv_ref)
    wdm = mix(wd_ref, swd_ref, muwd_ref)
    adm = mix(ad_ref, sad_ref, muad_ref)
    logw, kk, k2, bv = _rwkv_mix(rm, km, vm, wdm, adm, w0_ref[...], wup_ref[...], a0_ref[...], aup_ref[...],
                                 kk_ref[...], ka_ref[...], seg_ref[...])
    ro_ref[...] = rm
    ko_ref[...] = k2
    vo_ref[...] = vm
    rt_ref[...] = rm.T
    wt_ref[...] = jnp.exp(logw).T
    kt_ref[...] = k2.T
    bt_ref[...] = bv.T
    kkt_ref[...] = kk.T
    vt_ref[...] = vm.T


def _wkv_pre(proj, rb0, shift, rp):
    ns = shift.shape[0]
    w512 = HB * 64
    col = lambda c0: pl.BlockSpec((ns, w512), lambda h: (rb0, c0 // w512 + h))
    lora = lambda c0: pl.BlockSpec((ns, LORA), lambda h: (rb0, c0 // LORA))
    scol = lambda c0: pl.BlockSpec((ns, w512), lambda h: (0, c0 // w512 + h))
    slora = lambda c0: pl.BlockSpec((ns, LORA), lambda h: (0, c0 // LORA))
    vec = pl.BlockSpec((1, w512), lambda h: (0, h))
    vec128 = pl.BlockSpec((1, LORA), lambda h: (0, 0))
    up = pl.BlockSpec((LORA, w512), lambda h: (0, h))
    row = pl.BlockSpec((ns, w512), lambda h: (0, h))
    tr = pl.BlockSpec((w512, ns), lambda h: (h, 0))
    return pl.pallas_call(
        _wkv_pre_kernel, grid=(RWKV_HEADS // HB,),
        in_specs=[col(_C_R), col(_C_K), col(_C_V), lora(_C_WD), lora(_C_AD),
                  scol(0), scol(4096), scol(8192), slora(12288), slora(12416),
                  vec, vec, vec, vec128, vec128, vec, up, vec, up, vec, vec,
                  pl.BlockSpec((w512, w512), lambda h: (0, 0))],
        out_specs=[row] * 3 + [tr] * 6,
        out_shape=[jax.ShapeDtypeStruct((ns, RWKV_WIDTH), F32)] * 3
        + [jax.ShapeDtypeStruct((RWKV_WIDTH, ns), F32)] * 6,
        compiler_params=_cp(("parallel",)), name="wkv_sample_pre",
    )(proj, proj, proj, proj, proj, shift, shift, shift, shift, shift,
      rp["mur"], rp["muk"], rp["muv"], rp["muwd"], rp["muad"],
      rp["w0"], rp["wup"], rp["a0"], rp["aup"], rp["kk"], rp["ka"], rp["seg"])


def _wkv_state_kernel(s_ref, r_ref, w_ref, k_ref, b_ref, kk_ref, v_ref, so_ref, y_ref):
    for hh in range(WKV_HPS):
        ch = slice(hh * 64, (hh + 1) * 64)
        r, w, k, bv, kk = r_ref[ch, :], w_ref[ch, :], k_ref[ch, :], b_ref[ch, :], kk_ref[ch, :]

        def vrow(vi, carry):
            s = s_ref[hh, vi]
            sk = jnp.sum(s * kk, axis=0, keepdims=True)
            sn = s * w - sk * bv + v_ref[pl.ds(hh * 64 + vi, 1), :] * k
            so_ref[hh, vi] = sn
            y_ref[pl.ds(hh * 64 + vi, 1), :] = jnp.sum(sn * r, axis=0, keepdims=True)
            return carry

        lax.fori_loop(0, 64, vrow, 0, unroll=WKV_UNROLL)


def _wkv_state(s, r, w, k, bvec, kk, v):
    ns = s.shape[-1]
    hps = WKV_HPS
    sblk = pl.BlockSpec((hps, 64, 64, ns), lambda i: (i, 0, 0, 0))
    ch = pl.BlockSpec((hps * 64, ns), lambda i: (i, 0))
    return pl.pallas_call(
        _wkv_state_kernel, grid=(RWKV_HEADS // hps,),
        in_specs=[sblk, ch, ch, ch, ch, ch, ch],
        out_specs=[sblk, ch],
        out_shape=[jax.ShapeDtypeStruct(s.shape, F32), jax.ShapeDtypeStruct((RWKV_WIDTH, ns), F32)],
        compiler_params=_cp(("parallel",)), name="wkv_sample_state",
    )(s, r, w, k, bvec, kk, v)


def _wkv_post_kernel(o_ref, r_ref, k_ref, v_ref, g_ref, lnw_ref, lnb_ref, rk_ref, seg_ref, dst_ref, y_ref):
    del dst_ref
    y_ref[...] = _rwkv_out(o_ref[...].T, r_ref[...], k_ref[...], v_ref[...], g_ref[...],
                           lnw_ref[...], lnb_ref[...], rk_ref[...], seg_ref[...])


def _wkv_post(o_t, r, k2, v, proj, rb0, rp, dst):
    ns = o_t.shape[1]
    w512 = HB * 64
    row = pl.BlockSpec((ns, w512), lambda h: (0, h))
    vec = pl.BlockSpec((1, w512), lambda h: (0, h))
    return pl.pallas_call(
        _wkv_post_kernel, grid=(RWKV_HEADS // HB,),
        in_specs=[pl.BlockSpec((w512, ns), lambda h: (h, 0)), row, row, row,
                  pl.BlockSpec((pl.Element(ns), pl.Element(w512)), lambda h: (rb0 * ns, pl.multiple_of(_C_G + h * w512, LANE))),
                  vec, vec, vec, pl.BlockSpec((w512, w512), lambda h: (0, 0)),
                  pl.BlockSpec(memory_space=pl.ANY)],
        out_specs=pl.BlockSpec((ns, w512), lambda h: (rb0, h)),
        out_shape=jax.ShapeDtypeStruct(dst.shape, BF16),
        input_output_aliases={9: 0},
        compiler_params=_cp(("parallel",)), name="wkv_sample_post",
    )(o_t, r, k2, v, proj, rp["lnw"], rp["lnb"], rp["rk"], rp["seg"], dst)


def _ret_pre_kernel(q_ref, k_ref, cos_ref, sin_ref, qo_ref, kh_ref, kl_ref):
    cos = cos_ref[0:1, :]
    sin = sin_ref[0:1, :]
    qo_ref[...] = _rotate(q_ref[...], cos, sin) * (RET_QK_DIM ** -0.5)
    hi, lo = _split(_rotate(k_ref[...], cos, sin).T, 2)
    kh_ref[...] = hi
    kl_ref[...] = lo


def _ret_pre(proj, rb0, ns, cos, sin):
    return pl.pallas_call(
        _ret_pre_kernel, grid=(RET_HEADS,),
        in_specs=[pl.BlockSpec((ns, RET_QK_DIM), lambda h: (rb0, h)),
                  pl.BlockSpec((ns, RET_QK_DIM), lambda h: (rb0, RET_HEADS + h)),
                  pl.BlockSpec((8, 128), lambda h: (0, 0)), pl.BlockSpec((8, 128), lambda h: (0, 0))],
        out_specs=[pl.BlockSpec((ns, RET_QK_DIM), lambda h: (0, h)),
                   pl.BlockSpec((RET_QK_DIM, ns), lambda h: (h, 0)),
                   pl.BlockSpec((RET_QK_DIM, ns), lambda h: (h, 0))],
        out_shape=[jax.ShapeDtypeStruct((ns, RET_QK_WIDTH), F32),
                   jax.ShapeDtypeStruct((RET_QK_WIDTH, ns), BF16),
                   jax.ShapeDtypeStruct((RET_QK_WIDTH, ns), BF16)],
        compiler_params=_cp(("parallel",)), name="ret_sample_pre",
    )(proj, proj, cos, sin)


def _ret_state_kernel(gd_ref, s_ref, kh_ref, kl_ref, q_ref, v_ref, so_ref, y_ref, kb, ysc):
    b = pl.program_id(0)
    e = _onehot_cols(b, LANE)
    kb[...] = (jnp.dot(kh_ref[...], e, preferred_element_type=F32)
               + jnp.dot(kl_ref[...], e, preferred_element_type=F32))

    def head(h, carry):
        gd = gd_ref[h]
        vrow = v_ref[0, pl.ds(h, 1), :]
        for rc in range(RET_QK_DIM // 64):
            kcol = kb[pl.ds(pl.multiple_of(h * RET_QK_DIM + rc * 64, 64), 64), :]
            for lc in range(RET_V_DIM // LANE):
                rs, ls = slice(rc * 64, (rc + 1) * 64), slice(lc * LANE, (lc + 1) * LANE)
                so_ref[0, h, rs, ls] = s_ref[0, h, rs, ls] * gd + kcol * vrow[:, ls]
        qrow = q_ref[0, pl.ds(h, 1), :]
        y8 = _dot(jnp.broadcast_to(qrow, (8, RET_QK_DIM)), so_ref[0, h])
        ysc[pl.ds(h, 1), :] = y8[0:1, :]
        return carry

    lax.fori_loop(0, RET_HEADS, head, 0)
    y_ref[0] = ysc[...]


def _ret_state(s, kh, kl, q, v):
    ns = s.shape[0]
    gd = jnp.exp(_ret_log_g())
    sblk = pl.BlockSpec((1, RET_HEADS, RET_QK_DIM, RET_V_DIM), lambda i, g: (i, 0, 0, 0))
    full = pl.BlockSpec((RET_QK_WIDTH, ns), lambda i, g: (0, 0))
    grid_spec = pltpu.PrefetchScalarGridSpec(
        num_scalar_prefetch=1, grid=(ns,),
        in_specs=[sblk, full, full,
                  pl.BlockSpec((1, RET_HEADS, RET_QK_DIM), lambda i, g: (i, 0, 0)),
                  pl.BlockSpec((1, RET_HEADS, RET_V_DIM), lambda i, g: (i, 0, 0))],
        out_specs=[sblk, pl.BlockSpec((1, RET_HEADS, RET_V_DIM), lambda i, g: (i, 0, 0))],
        scratch_shapes=[pltpu.VMEM((RET_QK_WIDTH, LANE), F32), pltpu.VMEM((RET_HEADS, RET_V_DIM), F32)],
    )
    return pl.pallas_call(
        _ret_state_kernel, grid_spec=grid_spec,
        out_shape=[jax.ShapeDtypeStruct(s.shape, F32), jax.ShapeDtypeStruct((ns, RET_HEADS, RET_V_DIM), F32)],
        compiler_params=_cp(("parallel",)), name="ret_sample_state",
    )(gd, s, kh, kl, q, v)


def _ret_post_kernel(y_ref, g_ref, gnw_ref, dst_ref, o_ref):
    del dst_ref
    y = y_ref[...]
    mu = jnp.mean(y, -1, keepdims=True)
    d = y - mu
    var = jnp.mean(d * d, -1, keepdims=True)
    o = d * lax.rsqrt(var + RET_GN_EPS) * gnw_ref[...]
    o_ref[...] = (o * _silu(g_ref[...])).astype(BF16)


def _ret_post(y, proj, rb0, gn_w, dst):
    ns = y.shape[0]
    blk = pl.BlockSpec((ns, RET_V_DIM), lambda h: (0, h))
    return pl.pallas_call(
        _ret_post_kernel, grid=(RET_HEADS,),
        in_specs=[blk, pl.BlockSpec((ns, RET_V_DIM), lambda h: (rb0, (2 * RET_QK_WIDTH + RET_WIDTH) // RET_V_DIM + h)),
                  pl.BlockSpec((1, RET_V_DIM), lambda h: (0, h)), pl.BlockSpec(memory_space=pl.ANY)],
        out_specs=pl.BlockSpec((ns, RET_V_DIM), lambda h: (rb0, h)),
        out_shape=jax.ShapeDtypeStruct(dst.shape, BF16),
        input_output_aliases={3: 0},
        compiler_params=_cp(("parallel",)), name="ret_sample_post",
    )(y, proj, gn_w.reshape(1, -1), dst)


def _ab_layer(x, xb, nb, l, ns, conv_s, ssm_s, shift_s, wkv_s, w_in, sp, rp, w_out, ln_w, ln_b):
    mp = nb * l
    rb0 = mp // 128
    wt = w_in.T.astype(BF16)
    n_lo = AB_DT0 // AB_TN
    starts = [j * AB_TN for j in range(n_lo)] + [AB_DT0 + SSD_HEADS + j * AB_TN
                                                 for j in range((AB_MAIN - AB_DT0) // AB_TN)]
    proj = _matmul_wt(xb, wt, starts, AB_TN, "ab_in_proj")
    pdt = _matmul_wt(xb, wt, [AB_DT0], LANE, "ab_dt_proj")[:, :SSD_HEADS]
    m = proj.shape[0]
    dt3 = pdt.reshape(m, SSD_GROUPS, SSD_HPG)
    dtc = dt3.transpose(1, 0, 2)
    dtr = dt3.transpose(1, 2, 0)

    ya, ssm_p = _ssd_prompt(proj, dtc, dtr, nb, l, sp)
    yb, wkv_p = _rwkv_prompt(proj, nb, l, rp)
    tail = lambda n, c0, c1: jnp.stack([proj[(b + 1) * l - n:(b + 1) * l, c0:c1] for b in range(nb)])
    conv_p = tail(SSD_CONV - 1, _C_XS, _C_R)
    shift_p = tail(1, _C_R, _C_G)

    xa, ba, ca, xh, xl, dt_s, dec_s = _ssd_pre(proj, rb0, ns, conv_s.transpose(1, 0, 2), dtc, sp)
    flat = lambda t: t.transpose(1, 0, 2).reshape(ns * SSD_HEADS)
    ssm_n, y_s = _ssm_state(flat(dt_s), flat(dec_s), ssm_s, xh, xl, ba.transpose(1, 0, 2), ca.transpose(1, 0, 2))
    ya = _ssd_post(y_s.reshape(ns, SSD_WIDTH), xa, proj, rb0, sp, ya)
    conv_n = jnp.concatenate([conv_s[:, 1:], proj[mp:, None, _C_XS:_C_R]], axis=1)

    r_s, k_s, v_s, r_t, w_t, k_t, b_t, kk_t, v_t = _wkv_pre(proj, rb0, shift_s.reshape(ns, SHIFT_DIM), rp)
    wkv_t, o_t = _wkv_state(wkv_s.transpose(1, 2, 3, 0), r_t, w_t, k_t, b_t, kk_t, v_t)
    wkv_n = wkv_t.transpose(3, 0, 1, 2)
    yb = _wkv_post(o_t, r_s, k_s, v_s, proj, rb0, rp, yb)
    shift_n = proj[mp:, None, _C_R:_C_G]

    out = _matmul2(ya, yb, w_out.astype(BF16), "ab_out_proj")
    x_new, xb_new = _deepnorm(x, out, ln_w, ln_b, "ab_deepnorm")
    return x_new, xb_new, (conv_p, ssm_p, shift_p, wkv_p), (conv_n, ssm_n, shift_n, wkv_n)


def _ret_layer(x, xb, nb, l, ns, ret_s, w_in, gn_w, w_out, ln_w, ln_b):
    mp = nb * l
    rb0 = mp // 128
    proj = _matmul(xb, w_in.astype(BF16), "ret_in_proj")
    cos, sin = _trig(jnp.arange(l))
    cos_s, sin_s = _trig(jnp.full((8,), PAST_LEN))
    y, ret_p = _ret_prompt(proj, cos, sin, gn_w, nb, l)

    q_s, kh, kl = _ret_pre(proj, rb0, ns, cos_s, sin_s)
    v_s = proj[mp:, 2 * RET_QK_WIDTH:2 * RET_QK_WIDTH + RET_WIDTH].reshape(ns, RET_HEADS, RET_V_DIM)
    ret_n, o_s = _ret_state(ret_s, kh, kl, q_s.reshape(ns, RET_HEADS, RET_QK_DIM), v_s)
    y = _ret_post(o_s.reshape(ns, RET_WIDTH), proj, rb0, gn_w, y)

    out = _matmul(y, w_out.astype(BF16), "ret_out_proj")
    x_new, xb_new = _deepnorm(x, out, ln_w, ln_b, "ret_deepnorm")
    return x_new, xb_new, ret_p, ret_n


def kernel(x_prompt, x_sample, state_conv, state_ssm, state_shift, state_wkv, state_ret, ab_w_in, ssd_conv_w, ssd_conv_b, ssd_dt_bias, ssd_a_log, ssd_d, ssd_norm_w, rwkv_mu, rwkv_w0, rwkv_w_up, rwkv_a0, rwkv_a_up, rwkv_k_k, rwkv_k_a, rwkv_r_k, rwkv_lnx_w, rwkv_lnx_b, ab_w_out, ab_ln_w, ab_ln_b, ret_w_in, ret_gn_w, ret_w_out, ret_ln_w, ret_ln_b):
    nb, l, d = x_prompt.shape
    ns = x_sample.shape[0]
    assert x_sample.shape[1] == 1 and l % CHUNK == 0 and ns % LANE == 0 and ns == LANE
    mp = nb * l
    x = jnp.concatenate([x_prompt.reshape(mp, d), x_sample.reshape(ns, d)], axis=0)
    xb = x.astype(BF16)

    sp = _ssd_params(ssd_conv_w[0], ssd_conv_b[0], ssd_dt_bias[0], ssd_a_log[0], ssd_d[0], ssd_norm_w[0])
    rp = _rwkv_params(rwkv_mu[0], rwkv_w0[0], rwkv_w_up[0], rwkv_a0[0], rwkv_a_up[0], rwkv_k_k[0], rwkv_k_a[0],
                      rwkv_r_k[0], rwkv_lnx_w[0], rwkv_lnx_b[0])
    x, xb, pst, sst = _ab_layer(x, xb, nb, l, ns, state_conv[0], state_ssm[0], state_shift[0], state_wkv[0],
                                ab_w_in[0], sp, rp, ab_w_out[0], ab_ln_w[0], ab_ln_b[0])
    x, xb, ret_p, ret_n = _ret_layer(x, xb, nb, l, ns, state_ret[0], ret_w_in[0], ret_gn_w[0], ret_w_out[0],
                                     ret_ln_w[0], ret_ln_b[0])
    y_prompt = x[:mp].reshape(nb, l, d)
    y_sample = x[mp:].reshape(ns, 1, d)
    st = lambda t: t[None]
    return (y_prompt, y_sample,
            st(pst[0]), st(pst[1]), st(pst[2]), st(pst[3]), st(ret_p),
            st(sst[0]), st(sst[1]), st(sst[2]), st(sst[3]), st(ret_n))
```

```python
import functools
import math

import jax
import jax.numpy as jnp
import numpy as np
from jax import lax
from jax.experimental import pallas as pl
from jax.experimental.pallas import tpu as pltpu

F32 = jnp.float32
BF16 = jnp.bfloat16

D_MODEL = 4096
DEPTH = 2
PAST_LEN = 16384
SSD_WIDTH = 4096
SSD_HEAD_DIM = 64
SSD_HEADS = 64
SSD_GROUPS = 8
SSD_HPG = 8
SSD_STATE = 128
SSD_CONV = 4
SSD_CONV_DIM = SSD_WIDTH + 2 * SSD_GROUPS * SSD_STATE
RWKV_WIDTH = 4096
RWKV_HEAD_DIM = 64
RWKV_HEADS = 64
LORA = 128
SHIFT_DIM = 3 * RWKV_WIDTH + 2 * LORA
RET_HEADS = 16
RET_QK_DIM = 256
RET_V_DIM = 512
RET_QK_WIDTH = 4096
RET_WIDTH = 8192
ROPE_BASE = 10000.0
CHUNK = 128
ALPHA = (2 * DEPTH) ** 0.25
LN_EPS = 1e-5
RMS_EPS = 1e-5
RWKV_GN_EPS = 64e-5
RET_GN_EPS = 1e-6

LANE = 128
VMEM_LIMIT = 56 * 1024 * 1024
WKV_CHUNK = 64
RW_ROWS = 256
HB = 8
RET_HB = 4
SSM_BT = 2
WKV_HPS = 2
WKV_UNROLL = 8

_C_Z, _C_XS, _C_B, _C_C = 0, 4096, 8192, 9216
_C_R, _C_K, _C_V, _C_WD, _C_AD, _C_G = 10240, 14336, 18432, 22528, 22656, 22784
AB_MAIN = 26880
AB_DT0 = 10240
AB_TN = 1280


def _cp(sem):
    return pltpu.CompilerParams(dimension_semantics=sem, vmem_limit_bytes=VMEM_LIMIT)


def _silu(x):
    return x * jax.nn.sigmoid(x)


def _softplus(x):
    return jnp.maximum(x, 0.0) + jnp.log1p(jnp.exp(-jnp.abs(x)))


def _dot(a, b):
    return jnp.dot(a.astype(BF16), b.astype(BF16), preferred_element_type=F32)


def _dot_nt(a, b):
    return lax.dot_general(a.astype(BF16), b.astype(BF16), (((1,), (1,)), ((), ())),
                           preferred_element_type=F32)


def _dot_tn(a, b):
    return lax.dot_general(a.astype(BF16), b.astype(BF16), (((0,), (0,)), ((), ())),
                           preferred_element_type=F32)


def _split(x, n):
    parts, r = [], x
    for _ in range(n):
        h = r.astype(BF16)
        parts.append(h)
        r = r - h.astype(F32)
    return parts


def _dot01(m01, x, n=3):
    return sum(jnp.dot(m01, p, preferred_element_type=F32) for p in _split(x, n))


def _dot01_r(x, m01, n=2):
    return sum(jnp.dot(p, m01, preferred_element_type=F32) for p in _split(x, n))


def _segsum(x, seg):
    r, w = x.shape
    nt = w // LANE
    tall = jnp.concatenate([x[:, i * LANE:(i + 1) * LANE] for i in range(nt)], axis=0)
    s = _dot01_r(tall, seg)
    return jnp.concatenate([s[i * r:(i + 1) * r] for i in range(nt)], axis=1)


def _onehot_cols(b, n):
    rows = lax.broadcasted_iota(jnp.int32, (LANE, n), 0)
    return jnp.where(rows == b, 1.0, 0.0).astype(BF16)


def _mm_kernel(x_ref, w_ref, o_ref):
    o_ref[...] = jnp.dot(x_ref[...], w_ref[...], preferred_element_type=F32)


def _pick_tile(n, prefs):
    for t in prefs:
        if n % t == 0:
            return t
    return n


def _matmul(x, w, name):
    m, k = x.shape
    n = w.shape[1]
    tm = _pick_tile(m, (640, 512, 256, 128))
    tn = _pick_tile(n, (1280, 1024, 512, 256, 128) if k <= 4096 else (512, 256, 128))
    return pl.pallas_call(
        _mm_kernel,
        grid=(n // tn, m // tm),
        in_specs=[pl.BlockSpec((tm, k), lambda j, i: (i, 0)),
                  pl.BlockSpec((k, tn), lambda j, i: (0, j))],
        out_specs=pl.BlockSpec((tm, tn), lambda j, i: (i, j)),
        out_shape=jax.ShapeDtypeStruct((m, n), F32),
        compiler_params=_cp(("parallel", "parallel")),
        name=name,
    )(x, w)


def _mm_wt_kernel(st_ref, x_ref, wt_ref, o_ref):
    del st_ref
    o_ref[...] = lax.dot_general(x_ref[...], wt_ref[...], (((1,), (1,)), ((), ())), preferred_element_type=F32)


def _matmul_wt(x, wt, row_starts, tn, name):
    m, k = x.shape
    nt = len(row_starts)
    tm = _pick_tile(m, (640, 512, 256, 128))
    starts = jnp.asarray(row_starts, jnp.int32)
    grid_spec = pltpu.PrefetchScalarGridSpec(
        num_scalar_prefetch=1,
        grid=(nt, m // tm),
        in_specs=[pl.BlockSpec((tm, k), lambda j, i, st: (i, 0)),
                  pl.BlockSpec((pl.Element(tn), pl.Element(k)), lambda j, i, st: (pl.multiple_of(st[j], 64), 0))],
        out_specs=pl.BlockSpec((tm, tn), lambda j, i, st: (i, j)),
    )
    return pl.pallas_call(
        _mm_wt_kernel, grid_spec=grid_spec,
        out_shape=jax.ShapeDtypeStruct((m, nt * tn), F32),
        compiler_params=_cp(("parallel", "parallel")),
        name=name,
    )(starts, x, wt)


def _mm2_kernel(a_ref, b_ref, w_ref, o_ref):
    ka = a_ref.shape[1]
    o_ref[...] = (jnp.dot(a_ref[...], w_ref[0:ka, :], preferred_element_type=F32)
                  + jnp.dot(b_ref[...], w_ref[ka:, :], preferred_element_type=F32))


def _matmul2(a, b, w, name):
    m, ka = a.shape
    kb = b.shape[1]
    n = w.shape[1]
    tm = _pick_tile(m, (640, 512, 256, 128))
    tn = _pick_tile(n, (512, 256, 128))
    return pl.pallas_call(
        _mm2_kernel,
        grid=(n // tn, m // tm),
        in_specs=[pl.BlockSpec((tm, ka), lambda j, i: (i, 0)),
                  pl.BlockSpec((tm, kb), lambda j, i: (i, 0)),
                  pl.BlockSpec((ka + kb, tn), lambda j, i: (0, j))],
        out_specs=pl.BlockSpec((tm, tn), lambda j, i: (i, j)),
        out_shape=jax.ShapeDtypeStruct((m, n), F32),
        compiler_params=_cp(("parallel", "parallel")),
        name=name,
    )(a, b, w)


def _post_norm(x, o, w, b):
    h = ALPHA * x + o
    mu = jnp.mean(h, -1, keepdims=True)
    d = h - mu
    var = jnp.mean(d * d, -1, keepdims=True)
    return d * lax.rsqrt(var + LN_EPS) * w + b


def _ln_first_kernel(xp_ref, xs_ref, o_ref, w_ref, b_ref, y_ref, yb_ref):
    is_sample = pl.program_id(0) == pl.num_programs(0) - 1
    x = jnp.where(is_sample, xs_ref[...], xp_ref[...])
    y = _post_norm(x, o_ref[...], w_ref[...], b_ref[...])
    y_ref[...] = y
    yb_ref[...] = y.astype(BF16)


def _deepnorm_first(xp, xs, o, w, b, name):
    mp, d = xp.shape
    ns = xs.shape[0]
    npt = mp // ns
    row = pl.BlockSpec((ns, d), lambda i: (i, 0))
    vec = pl.BlockSpec((1, d), lambda i: (0, 0))
    return pl.pallas_call(
        _ln_first_kernel,
        grid=(npt + 1,),
        in_specs=[pl.BlockSpec((ns, d), lambda i: (jnp.minimum(i, npt - 1), 0)),
                  pl.BlockSpec((ns, d), lambda i: (0, 0)), row, vec, vec],
        out_specs=[row, row],
        out_shape=[jax.ShapeDtypeStruct((mp + ns, d), F32), jax.ShapeDtypeStruct((mp + ns, d), BF16)],
        compiler_params=_cp(("parallel",)),
        name=name,
    )(xp, xs, o, w.reshape(1, d), b.reshape(1, d))


def _ln_last_kernel(x_ref, o_ref, w_ref, b_ref, yp_ref, ys_ref):
    is_sample = pl.program_id(0) == pl.num_programs(0) - 1
    y = _post_norm(x_ref[...], o_ref[...], w_ref[...], b_ref[...])

    @pl.when(jnp.logical_not(is_sample))
    def _prompt():
        yp_ref[...] = y

    @pl.when(is_sample)
    def _sample():
        ys_ref[...] = y


def _deepnorm_last(x, o, w, b, ns, name):
    m, d = x.shape
    npt = m // ns - 1
    row = pl.BlockSpec((ns, d), lambda i: (i, 0))
    vec = pl.BlockSpec((1, d), lambda i: (0, 0))
    return pl.pallas_call(
        _ln_last_kernel,
        grid=(npt + 1,),
        in_specs=[row, row, vec, vec],
        out_specs=[pl.BlockSpec((ns, d), lambda i: (jnp.minimum(i, npt - 1), 0)),
                   pl.BlockSpec((ns, d), lambda i: (0, 0))],
        out_shape=[jax.ShapeDtypeStruct((npt * ns, d), F32), jax.ShapeDtypeStruct((ns, d), F32)],
        compiler_params=_cp(("arbitrary",)),
        name=name,
    )(x, o, w.reshape(1, d), b.reshape(1, d))


def _ssd_prompt_kernel(z_ref, xs_ref, b_ref, c_ref, dtc_ref, dtr_ref,
                       cwx_ref, cwb_ref, cwc_ref, cbx_ref, cbb_ref, cbc_ref,
                       dtbc_ref, dtbr_ref, alc_ref, alr_ref, dsk_ref, nw_ref, tri_ref, rep64_ref, rep128_ref,
                       y_ref, s_ref, bufx, bufb, bufc):
    L = CHUNK
    c = pl.program_id(2)

    @pl.when(c == 0)
    def _init():
        for buf in (bufx, bufb, bufc):
            buf[0:8, :] = jnp.zeros((8, buf.shape[1]), F32)
        s_ref[...] = jnp.zeros(s_ref.shape, F32)

    def conv(u_ref, buf, w_ref, bias_ref):
        buf[8:8 + L, :] = u_ref[...]
        acc = bias_ref[...] + buf[5:5 + L, :] * w_ref[0:1, :]
        for k in range(1, SSD_CONV):
            acc = acc + buf[5 + k:5 + k + L, :] * w_ref[k:k + 1, :]
        buf[0:8, :] = buf[L:L + 8, :]
        return _silu(acc)

    xs = conv(xs_ref, bufx, cwx_ref, cbx_ref)
    bm = conv(b_ref, bufb, cwb_ref, cbb_ref)
    cm = conv(c_ref, bufc, cwc_ref, cbc_ref)
    dtc = _softplus(dtc_ref[0] + dtbc_ref[0])
    dtr = _softplus(dtr_ref[0] + dtbr_ref[0])
    adt_c = dtc * (-jnp.exp(alc_ref[0]))
    adt_r = dtr * (-jnp.exp(alr_ref[0]))
    tri = tri_ref[...]
    cum_c = _dot01(tri, adt_c)
    cum_r = sum(lax.dot_general(p, tri, (((1,), (1,)), ((), ())), preferred_element_type=F32)
                for p in _split(adt_r, 3))

    li = lax.broadcasted_iota(jnp.int32, (L, L), 0)
    si = lax.broadcasted_iota(jnp.int32, (L, L), 1)
    causal = li >= si
    cb = _dot_nt(cm, bm)
    dt_x = _dot01_r(dtc, rep64_ref[...], 3)
    cum_x = _dot01_r(cum_c, rep64_ref[...], 3)
    cum_b = _dot01_r(cum_c, rep128_ref[...], 3)
    xdt = xs * dt_x
    xdt_tail = xdt * jnp.exp(cum_x[L - 1:L, :] - cum_x)
    lo = lax.broadcasted_iota(jnp.int32, (1, LANE), 1) < 64
    pairs = range(SSD_HPG // 2)
    tile = lambda a, p: a[:, p * LANE:(p + 1) * LANE]
    s_old = [s_ref[0, r] for r in range(SSD_HPG)]
    decay = [jnp.exp(jnp.where(causal, tile(cum_b, r) - cum_r[r:r + 1, :], -jnp.inf)) for r in range(SSD_HPG)]
    x_lo = [jnp.where(lo, tile(xdt, p), 0.0) for p in pairs]
    x_hi = [jnp.where(lo, 0.0, tile(xdt, p)) for p in pairs]
    y_in = [_dot(cb * decay[2 * p], x_lo[p]) + _dot(cb * decay[2 * p + 1], x_hi[p]) for p in pairs]
    y_st = [_dot_nt(cm, jnp.concatenate([s_old[2 * p], s_old[2 * p + 1]], 0)) for p in pairs]
    s_in = [_dot_tn(tile(xdt_tail, p), bm) for p in pairs]
    for r in range(SSD_HPG):
        half = s_in[r // 2][(r % 2) * 64:(r % 2 + 1) * 64]
        s_ref[0, r] = s_old[r] * jnp.exp(cum_c[L - 1:L, r:r + 1]) + half
    y = jnp.concatenate(y_in, axis=1) + jnp.concatenate(y_st, axis=1) * jnp.exp(cum_x) + xs * dsk_ref[...]
    y = y * _silu(z_ref[...])
    y = y * lax.rsqrt(jnp.mean(y * y, -1, keepdims=True) + RMS_EPS) * nw_ref[...]
    y_ref[...] = y.astype(BF16)


def _ssd_params(conv_w, conv_b, dt_bias, a_log, d_skip, norm_w):
    g = SSD_GROUPS
    return dict(
        cwx=conv_w[:, :4096], cwb=conv_w[:, 4096:5120], cwc=conv_w[:, 5120:],
        cbx=conv_b[:4096].reshape(1, -1), cbb=conv_b[4096:5120].reshape(1, -1), cbc=conv_b[5120:].reshape(1, -1),
        dtbc=dt_bias.reshape(g, 1, 8), dtbr=dt_bias.reshape(g, 8, 1),
        alc=a_log.reshape(g, 1, 8), alr=a_log.reshape(g, 8, 1),
        dsk=jnp.repeat(d_skip, SSD_HEAD_DIM).reshape(1, -1), nw=norm_w.reshape(1, -1))


def _ssd_prompt(proj, dtc, dtr, nb, l, sp):
    nc = l // CHUNK
    rb = lambda b, g, c: b * nc + c
    tri = jnp.tril(jnp.ones((CHUNK, CHUNK), BF16))
    in_specs = [
        pl.BlockSpec((CHUNK, 512), lambda b, g, c: (rb(b, g, c), _C_Z // 512 + g)),
        pl.BlockSpec((CHUNK, 512), lambda b, g, c: (rb(b, g, c), _C_XS // 512 + g)),
        pl.BlockSpec((CHUNK, 128), lambda b, g, c: (rb(b, g, c), _C_B // 128 + g)),
        pl.BlockSpec((CHUNK, 128), lambda b, g, c: (rb(b, g, c), _C_C // 128 + g)),
        pl.BlockSpec((1, CHUNK, 8), lambda b, g, c: (g, rb(b, g, c), 0)),
        pl.BlockSpec((1, 8, CHUNK), lambda b, g, c: (g, 0, rb(b, g, c))),
        pl.BlockSpec((SSD_CONV, 512), lambda b, g, c: (0, g)),
        pl.BlockSpec((SSD_CONV, 128), lambda b, g, c: (0, g)),
        pl.BlockSpec((SSD_CONV, 128), lambda b, g, c: (0, g)),
        pl.BlockSpec((1, 512), lambda b, g, c: (0, g)),
        pl.BlockSpec((1, 128), lambda b, g, c: (0, g)),
        pl.BlockSpec((1, 128), lambda b, g, c: (0, g)),
        pl.BlockSpec((1, 1, 8), lambda b, g, c: (g, 0, 0)),
        pl.BlockSpec((1, 8, 1), lambda b, g, c: (g, 0, 0)),
        pl.BlockSpec((1, 1, 8), lambda b, g, c: (g, 0, 0)),
        pl.BlockSpec((1, 8, 1), lambda b, g, c: (g, 0, 0)),
        pl.BlockSpec((1, 512), lambda b, g, c: (0, g)),
        pl.BlockSpec((1, 512), lambda b, g, c: (0, g)),
        pl.BlockSpec((CHUNK, CHUNK), lambda b, g, c: (0, 0)),
        pl.BlockSpec((SSD_HPG, SSD_HPG * 64), lambda b, g, c: (0, 0)),
        pl.BlockSpec((SSD_HPG, SSD_HPG * LANE), lambda b, g, c: (0, 0)),
    ]
    rep64 = jnp.asarray(np.kron(np.eye(SSD_HPG), np.ones((1, 64))), BF16)
    rep128 = jnp.asarray(np.kron(np.eye(SSD_HPG), np.ones((1, LANE))), BF16)
    out_specs = [pl.BlockSpec((CHUNK, 512), lambda b, g, c: (rb(b, g, c), g)),
                 pl.BlockSpec((1, SSD_HPG, SSD_HEAD_DIM, SSD_STATE), lambda b, g, c: (b, g, 0, 0))]
    return pl.pallas_call(
        _ssd_prompt_kernel,
        grid=(nb, SSD_GROUPS, nc),
        in_specs=in_specs,
        out_specs=out_specs,
        out_shape=[jax.ShapeDtypeStruct((proj.shape[0], SSD_WIDTH), BF16),
                   jax.ShapeDtypeStruct((nb, SSD_HEADS, SSD_HEAD_DIM, SSD_STATE), F32)],
        scratch_shapes=[pltpu.VMEM((CHUNK + 8, 512), F32), pltpu.VMEM((CHUNK + 8, 128), F32),
                        pltpu.VMEM((CHUNK + 8, 128), F32)],
        compiler_params=_cp(("parallel", "parallel", "arbitrary")),
        name="ssd_prompt",
    )(proj, proj, proj, proj, dtc, dtr, sp["cwx"], sp["cwb"], sp["cwc"], sp["cbx"], sp["cbb"], sp["cbc"],
      sp["dtbc"], sp["dtbr"], sp["alc"], sp["alr"], sp["dsk"], sp["nw"], tri, rep64, rep128)


def _rwkv_mix(rm, km, vm, wdm, adm, w0, wup, a0, aup, k_k, k_a, seg):
    wlog = -_softplus(-(w0 + _dot(jnp.tanh(wdm), wup))) - 0.5
    logw = -jnp.exp(wlog)
    aa = jax.nn.sigmoid(a0 + _dot(adm, aup))
    kkr = km * k_k
    kk = kkr * lax.rsqrt(jnp.maximum(_segsum(kkr * kkr, seg), 1e-24))
    k2 = km * (1.0 + (aa - 1.0) * k_a)
    return logw, kk, k2, kk * aa


def _rwkv_out(o, rm, k2, vm, g, lnw, lnb, rk, seg):
    inv = 1.0 / RWKV_HEAD_DIM
    mean = _segsum(o, seg) * inv
    d = o - mean
    var = _segsum(d * d, seg) * inv
    on = d * lax.rsqrt(var + RWKV_GN_EPS) * lnw + lnb
    bonus = _segsum(rm * k2 * rk, seg) * vm
    return ((on + bonus) * _silu(g)).astype(BF16)


def _rwkv_prompt_kernel(r_ref, k_ref, v_ref, g_ref, wd_ref, ad_ref,
                        mur_ref, muk_ref, muv_ref, muwd_ref, muad_ref,
                        w0_ref, wup_ref, a0_ref, aup_ref, kk_ref, ka_ref, lnw_ref, lnb_ref, rk_ref,
                        seg_ref, tri_ref,
                        y_ref, s_ref, cr, ck, cv, cwd, cad):
    R, C = RW_ROWS, WKV_CHUNK
    c = pl.program_id(2)

    @pl.when(c == 0)
    def _init():
        for buf in (cr, ck, cv, cwd, cad):
            buf[...] = jnp.zeros(buf.shape, F32)
        s_ref[...] = jnp.zeros(s_ref.shape, F32)

    row0 = lax.broadcasted_iota(jnp.int32, (R, 1), 0) == 0

    def shift(x_ref, carry, mu_ref):
        x = x_ref[...]
        prev = jnp.where(row0, carry[0:1, :], pltpu.roll(x, 1, 0))
        carry[0:1, :] = x[R - 1:R, :]
        return x + (prev - x) * mu_ref[...]

    rm = shift(r_ref, cr, mur_ref)
    km = shift(k_ref, ck, muk_ref)
    vm = shift(v_ref, cv, muv_ref)
    wdm = shift(wd_ref, cwd, muwd_ref)
    adm = shift(ad_ref, cad, muad_ref)
    seg = seg_ref[...]
    logw, kk, k2, bv = _rwkv_mix(rm, km, vm, wdm, adm, w0_ref[...], wup_ref[...], a0_ref[...], aup_ref[...],
                                 kk_ref[...], ka_ref[...], seg)

    tri = tri_ref[...]
    li = lax.broadcasted_iota(jnp.int32, (C, LANE), 0)
    lane = lax.broadcasted_iota(jnp.int32, (C, LANE), 1)
    si = lane % 64
    strict = li > si
    incl = li >= si
    eye = jnp.where(li == si, 1.0, 0.0)
    lo = lane < 64
    rlo = lax.broadcasted_iota(jnp.int32, (LANE, LANE), 0) < 64
    llo = lax.broadcasted_iota(jnp.int32, (LANE, LANE), 1) < 64
    same = rlo == llo

    def bd(a):
        ab = a.astype(BF16)
        zero = jnp.zeros_like(ab)
        return jnp.concatenate([jnp.where(lo, ab, zero), jnp.where(lo, zero, ab)], axis=0)

    nsc = R // C
    prep = []
    for sc in range(nsc):
        rows = slice(sc * C, (sc + 1) * C)
        lw = logw[rows]
        cs = _dot01(tri, lw)
        cl = cs[C - 1:C, :]
        e_tail = jnp.exp(cl - cs)
        e_neg = jnp.exp(-cs)
        prep.append(dict(
            bt=kk[rows] * jnp.exp(cs - lw),
            bb=bv[rows] * e_neg,
            kt=k2[rows] * e_neg,
            rt=rm[rows] * jnp.exp(cs),
            bh=bv[rows] * e_tail,
            kh=k2[rows] * e_tail,
            pc=jnp.exp(cl), v=vm[rows]))
    npair = HB // 2
    keys = [(sc, p) for sc in range(nsc) for p in range(npair)]
    part = lambda name: {k: prep[k[0]][name][:, k[1] * LANE:(k[1] + 1) * LANE] for k in keys}
    bt, bb, kt, rt, bh, kh, vh, pc = (part(n) for n in ("bt", "bb", "kt", "rt", "bh", "kh", "v", "pc"))
    lhs = {k: jnp.concatenate([bt[k], rt[k]], 0) for k in keys}
    gb = {k: _dot_nt(lhs[k], bd(bb[k])) for k in keys}
    gk = {k: _dot_nt(lhs[k], bd(kt[k])) for k in keys}
    lk = {k: jnp.where(strict, gk[k][0:C], 0.0) for k in keys}
    rb = {k: jnp.where(incl, gb[k][C:2 * C], 0.0) for k in keys}
    rkm = {k: jnp.where(incl, gk[k][C:2 * C], 0.0) for k in keys}
    x = {k: jnp.where(strict, -gb[k][0:C], 0.0) for k in keys}
    t = {k: eye + x[k] for k in keys}
    for _ in range(int(math.log2(C)) - 1):
        x = {k: _dot(x[k], bd(x[k])) for k in keys}
        t = {k: t[k] + _dot(t[k], bd(x[k])) for k in keys}
    bdv = {k: bd(vh[k]) for k in keys}
    lkv = {k: _dot(lk[k], bdv[k]) for k in keys}
    tb = {k: _dot(t[k], bd(bt[k])) for k in keys}
    tlv = {k: _dot(t[k], bd(lkv[k])) for k in keys}
    rq = {k: rt[k] - _dot(rb[k], bd(tb[k])) for k in keys}
    yc = {k: _dot(rkm[k], bdv[k]) - _dot(rb[k], bd(tlv[k])) for k in keys}
    mq = {k: jnp.where(same, _dot_tn(tb[k], bh[k]), 0.0).astype(BF16) for k in keys}
    nf = {k: _dot_tn(jnp.concatenate([vh[k], -tlv[k]], 0), jnp.concatenate([kh[k], bh[k]], 0)) for k in keys}
    vlo = lax.broadcasted_iota(jnp.int32, (64, LANE), 1) < 64
    nn = {k: jnp.where(vlo, nf[k][0:64], nf[k][64:128]) for k in keys}
    st = [jnp.concatenate([s_ref[0, 2 * p], s_ref[0, 2 * p + 1]], axis=1) for p in range(npair)]
    o_chunks = []
    for sc in range(nsc):
        ys = [_dot_nt(rq[sc, p], bd(st[p])) + yc[sc, p] for p in range(npair)]
        st = [st[p] * pc[sc, p] - _dot(st[p], mq[sc, p]) + nn[sc, p] for p in range(npair)]
        o_chunks.append(jnp.concatenate(ys, 1))
    for p in range(npair):
        s_ref[0, 2 * p] = st[p][:, 0:64]
        s_ref[0, 2 * p + 1] = st[p][:, 64:128]
    o = jnp.concatenate(o_chunks, 0)
    y_ref[...] = _rwkv_out(o, rm, k2, vm, g_ref[...], lnw_ref[...], lnb_ref[...], rk_ref[...], seg)


def _rwkv_params(mu, w0, w_up, a0, a_up, k_k, k_a, r_k, lnx_w, lnx_b):
    v = lambda t: t.reshape(1, -1)
    return dict(
        mur=v(mu[0:4096]), muk=v(mu[4096:8192]), muv=v(mu[8192:12288]),
        muwd=v(mu[12288:12416]), muad=v(mu[12416:12544]),
        w0=v(w0), wup=w_up.astype(BF16), a0=v(a0), aup=a_up.astype(BF16), kk=v(k_k), ka=v(k_a),
        lnw=v(lnx_w), lnb=v(lnx_b), rk=v(r_k),
        seg=jnp.asarray(np.kron(np.eye(LANE // 64), np.ones((64, 64))), BF16))


def _rwkv_prompt(proj, nb, l, rp):
    nr = l // RW_ROWS
    rb = lambda b, h, c: b * nr + c
    w512 = HB * 64
    col = lambda c0: pl.BlockSpec((RW_ROWS, w512), lambda b, h, c: (rb(b, h, c), c0 // w512 + h))
    lora = lambda c0: pl.BlockSpec((RW_ROWS, LORA), lambda b, h, c: (rb(b, h, c), c0 // LORA))
    vec = pl.BlockSpec((1, w512), lambda b, h, c: (0, h))
    vec128 = pl.BlockSpec((1, LORA), lambda b, h, c: (0, 0))
    up = pl.BlockSpec((LORA, w512), lambda b, h, c: (0, h))
    tri = jnp.tril(jnp.ones((WKV_CHUNK, WKV_CHUNK), BF16))
    gate = pl.BlockSpec((pl.Element(RW_ROWS), pl.Element(w512)),
                        lambda b, h, c: (rb(b, h, c) * RW_ROWS, pl.multiple_of(_C_G + h * w512, LANE)))
    in_specs = [col(_C_R), col(_C_K), col(_C_V), gate, lora(_C_WD), lora(_C_AD),
                vec, vec, vec, vec128, vec128,
                vec, up, vec, up, vec, vec, vec, vec, vec,
                pl.BlockSpec((LANE, LANE), lambda b, h, c: (0, 0)),
                pl.BlockSpec((WKV_CHUNK, WKV_CHUNK), lambda b, h, c: (0, 0))]
    out_specs = [pl.BlockSpec((RW_ROWS, w512), lambda b, h, c: (rb(b, h, c), h)),
                 pl.BlockSpec((1, HB, 64, 64), lambda b, h, c: (b, h, 0, 0))]
    return pl.pallas_call(
        _rwkv_prompt_kernel,
        grid=(nb, RWKV_HEADS // HB, nr),
        in_specs=in_specs,
        out_specs=out_specs,
        out_shape=[jax.ShapeDtypeStruct((proj.shape[0], RWKV_WIDTH), BF16),
                   jax.ShapeDtypeStruct((nb, RWKV_HEADS, 64, 64), F32)],
        scratch_shapes=[pltpu.VMEM((8, w512), F32)] * 3 + [pltpu.VMEM((8, LORA), F32)] * 2,
        compiler_params=_cp(("parallel", "parallel", "arbitrary")),
        name="rwkv_prompt",
    )(proj, proj, proj, proj, proj, proj,
      rp["mur"], rp["muk"], rp["muv"], rp["muwd"], rp["muad"],
      rp["w0"], rp["wup"], rp["a0"], rp["aup"], rp["kk"], rp["ka"], rp["lnw"], rp["lnb"], rp["rk"],
      rp["seg"], tri)


def _trig_kernel(pos_ref, freq_ref, cos_ref, sin_ref):
    ang = pos_ref[...] * freq_ref[...]
    cos_ref[...] = jnp.cos(ang)
    sin_ref[...] = jnp.sin(ang)


def _trig(pos):
    n = pos.shape[0]
    half = RET_QK_DIM // 2
    freq = (ROPE_BASE ** (-jnp.arange(half, dtype=F32) / half)).reshape(1, half)
    posb = jnp.broadcast_to(pos.astype(F32)[:, None], (n, half))
    tn = _pick_tile(n, (256, 128, 8))
    blk = pl.BlockSpec((tn, half), lambda i: (i, 0))
    return pl.pallas_call(
        _trig_kernel, grid=(n // tn,),
        in_specs=[blk, pl.BlockSpec((1, half), lambda i: (0, 0))],
        out_specs=[blk, blk],
        out_shape=[jax.ShapeDtypeStruct((n, half), F32)] * 2,
        name="rope_tables",
    )(posb, freq)


def _rotate(x, cos, sin):
    x1, x2 = x[:, :128], x[:, 128:]
    return jnp.concatenate([x1 * cos - x2 * sin, x1 * sin + x2 * cos], 1)


def _ret_prompt_kernel(lg_ref, q_ref, k_ref, v_ref, g_ref, cos_ref, sin_ref, gnw_ref,
                       y_ref, s_ref):
    L = CHUNK
    hg = pl.program_id(1)
    c = pl.program_id(2)

    @pl.when(c == 0)
    def _init():
        s_ref[...] = jnp.zeros(s_ref.shape, F32)

    cos = cos_ref[...]
    sin = sin_ref[...]
    li = lax.broadcasted_iota(jnp.int32, (L, L), 0)
    si = lax.broadcasted_iota(jnp.int32, (L, L), 1)
    rel = (li - si).astype(F32)
    causal = li >= si
    icol = lax.broadcasted_iota(jnp.int32, (L, 1), 0).astype(F32)
    heads = range(RET_HB)
    lg = [lg_ref[hg * RET_HB + j] for j in heads]
    qr = [(_rotate(q_ref[:, j * 256:(j + 1) * 256], cos, sin) * (RET_QK_DIM ** -0.5)).astype(BF16) for j in heads]
    kr = [_rotate(k_ref[:, j * 256:(j + 1) * 256], cos, sin) for j in heads]
    v = [v_ref[:, j * 512:(j + 1) * 512].astype(BF16) for j in heads]
    s0 = [s_ref[0, j] for j in heads]
    qk = [_dot_nt(qr[j], kr[j]) for j in heads]
    y_st = [_dot(qr[j], s0[j]) for j in heads]
    s_in = [_dot_tn(kr[j] * jnp.exp((L - 1.0 - icol) * lg[j]), v[j]) for j in heads]
    sc = [qk[j] * jnp.exp(jnp.where(causal, rel * lg[j], -jnp.inf)) for j in heads]
    y_in = [_dot(sc[j], v[j]) for j in heads]
    outs = []
    for j in heads:
        s_ref[0, j] = s0[j] * jnp.exp(L * lg[j]) + s_in[j]
        y = y_in[j] + y_st[j] * jnp.exp((icol + 1.0) * lg[j])
        mu = jnp.mean(y, -1, keepdims=True)
        d = y - mu
        var = jnp.mean(d * d, -1, keepdims=True)
        outs.append(d * lax.rsqrt(var + RET_GN_EPS))
    o = jnp.concatenate(outs, 1) * gnw_ref[...]
    y_ref[...] = (o * _silu(g_ref[...])).astype(BF16)


def _ret_log_g():
    return jnp.log1p(-jnp.exp2(-5.0 - jnp.arange(RET_HEADS, dtype=F32)))


def _ret_prompt(proj, cos, sin, gn_w, nb, l):
    nc = l // CHUNK
    rb = lambda b, h, c, lg: b * nc + c
    wq, wv = RET_HB * RET_QK_DIM, RET_HB * RET_V_DIM
    grid_spec = pltpu.PrefetchScalarGridSpec(
        num_scalar_prefetch=1,
        grid=(nb, RET_HEADS // RET_HB, nc),
        in_specs=[
            pl.BlockSpec((CHUNK, wq), lambda b, h, c, lg: (rb(b, h, c, lg), h)),
            pl.BlockSpec((CHUNK, wq), lambda b, h, c, lg: (rb(b, h, c, lg), RET_QK_WIDTH // wq + h)),
            pl.BlockSpec((CHUNK, wv), lambda b, h, c, lg: (rb(b, h, c, lg), 2 * RET_QK_WIDTH // wv + h)),
            pl.BlockSpec((CHUNK, wv), lambda b, h, c, lg: (rb(b, h, c, lg), (2 * RET_QK_WIDTH + RET_WIDTH) // wv + h)),
            pl.BlockSpec((CHUNK, 128), lambda b, h, c, lg: (c, 0)),
            pl.BlockSpec((CHUNK, 128), lambda b, h, c, lg: (c, 0)),
            pl.BlockSpec((1, wv), lambda b, h, c, lg: (0, h)),
        ],
        out_specs=[pl.BlockSpec((CHUNK, wv), lambda b, h, c, lg: (rb(b, h, c, lg), h)),
                   pl.BlockSpec((1, RET_HB, RET_QK_DIM, RET_V_DIM), lambda b, h, c, lg: (b, h, 0, 0))],
    )
    return pl.pallas_call(
        _ret_prompt_kernel,
        grid_spec=grid_spec,
        out_shape=[jax.ShapeDtypeStruct((proj.shape[0], RET_WIDTH), BF16),
                   jax.ShapeDtypeStruct((nb, RET_HEADS, RET_QK_DIM, RET_V_DIM), F32)],
        compiler_params=_cp(("parallel", "parallel", "arbitrary")),
        name="ret_prompt",
    )(_ret_log_g(), proj, proj, proj, proj, cos, sin, gn_w.reshape(1, -1))


def _ssd_pre_kernel(xs_ref, b_ref, c_ref, csx_ref, csb_ref, csc_ref, dtc_ref,
                    cwx_ref, cwb_ref, cwc_ref, cbx_ref, cbb_ref, cbc_ref, dtb_ref, al_ref,
                    xa_ref, ba_ref, ca_ref, dt_ref, dec_ref):
    def conv(u_ref, cs_ref, w_ref, bias_ref):
        acc = bias_ref[...] + u_ref[...] * w_ref[SSD_CONV - 1:SSD_CONV, :]
        for k in range(SSD_CONV - 1):
            acc = acc + cs_ref[k] * w_ref[k:k + 1, :]
        return _silu(acc)

    xs = conv(xs_ref, csx_ref, cwx_ref, cbx_ref)
    xa_ref[...] = xs
    ba_ref[0] = conv(b_ref, csb_ref, cwb_ref, cbb_ref)
    ca_ref[0] = conv(c_ref, csc_ref, cwc_ref, cbc_ref)
    dt = _softplus(dtc_ref[0] + dtb_ref[0])
    dt_ref[0] = dt
    dec_ref[0] = jnp.exp(dt * (-jnp.exp(al_ref[0])))


def _ssd_pre(proj, rb0, ns, cs_t, dtc, sp):
    g8 = SSD_GROUPS
    in_specs = [
        pl.BlockSpec((ns, 512), lambda g: (rb0, _C_XS // 512 + g)),
        pl.BlockSpec((ns, 128), lambda g: (rb0, _C_B // 128 + g)),
        pl.BlockSpec((ns, 128), lambda g: (rb0, _C_C // 128 + g)),
        pl.BlockSpec((3, ns, 512), lambda g: (0, 0, g)),
        pl.BlockSpec((3, ns, 128), lambda g: (0, 0, 4096 // 128 + g)),
        pl.BlockSpec((3, ns, 128), lambda g: (0, 0, 5120 // 128 + g)),
        pl.BlockSpec((1, ns, 8), lambda g: (g, rb0, 0)),
        pl.BlockSpec((SSD_CONV, 512), lambda g: (0, g)),
        pl.BlockSpec((SSD_CONV, 128), lambda g: (0, g)),
        pl.BlockSpec((SSD_CONV, 128), lambda g: (0, g)),
        pl.BlockSpec((1, 512), lambda g: (0, g)),
        pl.BlockSpec((1, 128), lambda g: (0, g)),
        pl.BlockSpec((1, 128), lambda g: (0, g)),
        pl.BlockSpec((1, 1, 8), lambda g: (g, 0, 0)),
        pl.BlockSpec((1, 1, 8), lambda g: (g, 0, 0)),
    ]
    out_specs = [
        pl.BlockSpec((ns, 512), lambda g: (0, g)),
        pl.BlockSpec((1, ns, 128), lambda g: (g, 0, 0)),
        pl.BlockSpec((1, ns, 128), lambda g: (g, 0, 0)),
        pl.BlockSpec((1, ns, 8), lambda g: (g, 0, 0)),
        pl.BlockSpec((1, ns, 8), lambda g: (g, 0, 0)),
    ]
    out_shape = [
        jax.ShapeDtypeStruct((ns, SSD_WIDTH), F32),
        jax.ShapeDtypeStruct((g8, ns, SSD_STATE), F32),
        jax.ShapeDtypeStruct((g8, ns, SSD_STATE), F32),
        jax.ShapeDtypeStruct((g8, ns, 8), F32),
        jax.ShapeDtypeStruct((g8, ns, 8), F32),
    ]
    return pl.pallas_call(
        _ssd_pre_kernel, grid=(g8,), in_specs=in_specs, out_specs=out_specs, out_shape=out_shape,
        compiler_params=_cp(("parallel",)), name="ssd_sample_pre",
    )(proj, proj, proj, cs_t, cs_t, cs_t, dtc, sp["cwx"], sp["cwb"], sp["cwc"], sp["cbx"], sp["cbb"], sp["cbc"],
      sp["dtbc"], sp["alc"])


def _outer_rows(x, y):
    hi = lambda t: t.astype(BF16).astype(F32)
    xh, yh = hi(x), hi(y)
    rx = lax.broadcasted_iota(jnp.int32, (8, x.shape[1]), 0)
    ry = lax.broadcasted_iota(jnp.int32, (8, y.shape[1]), 0)
    lhs = jnp.where(rx == 1, x - xh, jnp.where((rx == 0) | (rx == 2), xh, 0.0))
    rhs = jnp.where(ry == 2, y - yh, jnp.where(ry < 2, yh, 0.0))
    return lhs.astype(BF16), rhs.astype(BF16)


def _ssm_state_kernel(dt_ref, dec_ref, s_ref, x_ref, b_ref, c_ref, so_ref, y_ref):
    i = pl.program_id(0)
    hw = SSD_HPG * SSD_HEAD_DIM
    for j in range(SSM_BT):
        b = i * SSM_BT + j
        for g in range(SSD_GROUPS):
            lhs, rhs = _outer_rows(x_ref[j, :, g * hw:(g + 1) * hw], b_ref[j, g:g + 1, :])
            xb = _dot_tn(lhs, rhs)
            new = []
            for r in range(SSD_HPG):
                h = g * SSD_HPG + r
                sn = (s_ref[j, h] * dec_ref[b * SSD_HEADS + h]
                      + xb[r * 64:(r + 1) * 64] * dt_ref[b * SSD_HEADS + h])
                so_ref[j, h] = sn
                new.append(sn)
            crow = jnp.broadcast_to(c_ref[j, g:g + 1, :], (8, SSD_STATE))
            y_ref[j, g:g + 1, :] = _dot_nt(crow, jnp.concatenate(new, 0))[0:1, :]


def _ssm_state(dt, dec, s, xa, ba, ca):
    ns = s.shape[0]
    bt = SSM_BT
    smem = pl.BlockSpec(memory_space=pltpu.SMEM)
    sblk = pl.BlockSpec((bt, SSD_HEADS, SSD_HEAD_DIM, SSD_STATE), lambda i: (i, 0, 0, 0))
    bc = pl.BlockSpec((bt, SSD_GROUPS, SSD_STATE), lambda i: (i, 0, 0))
    return pl.pallas_call(
        _ssm_state_kernel, grid=(ns // bt,),
        in_specs=[smem, smem, sblk, pl.BlockSpec((bt, 1, SSD_WIDTH), lambda i: (i, 0, 0)), bc, bc],
        out_specs=[sblk, pl.BlockSpec((bt, SSD_GROUPS, 512), lambda i: (i, 0, 0))],
        out_shape=[jax.ShapeDtypeStruct(s.shape, F32), jax.ShapeDtypeStruct((ns, SSD_GROUPS, 512), F32)],
        compiler_params=_cp(("parallel",)), name="ssm_sample_state",
    )(dt, dec, s, xa, ba, ca)


def _ssd_post_kernel(y_ref, xa_ref, z_ref, dsk_ref, nw_ref, dst_ref, o_ref):
    del dst_ref
    y = (y_ref[...] + xa_ref[...] * dsk_ref[...]) * _silu(z_ref[...])
    y = y * lax.rsqrt(jnp.mean(y * y, -1, keepdims=True) + RMS_EPS) * nw_ref[...]
    o_ref[...] = y.astype(BF16)


def _ssd_post(y, xa, proj, rb0, sp, dst):
    ns = y.shape[0]
    blk = pl.BlockSpec((ns, 512), lambda g: (0, g))
    vec = pl.BlockSpec((1, 512), lambda g: (0, g))
    return pl.pallas_call(
        _ssd_post_kernel, grid=(SSD_GROUPS,),
        in_specs=[blk, blk, pl.BlockSpec((ns, 512), lambda g: (rb0, _C_Z // 512 + g)), vec, vec,
                  pl.BlockSpec(memory_space=pl.ANY)],
        out_specs=pl.BlockSpec((ns, 512), lambda g: (rb0, g)),
        out_shape=jax.ShapeDtypeStruct(dst.shape, BF16),
        input_output_aliases={5: 0},
        compiler_params=_cp(("parallel",)), name="ssd_sample_post",
    )(y, xa, proj, sp["dsk"], sp["nw"], dst)


def _wkv_pre_kernel(r_ref, k_ref, v_ref, wd_ref, ad_ref, sr_ref, sk_ref, sv_ref, swd_ref, sad_ref,
                    mur_ref, muk_ref, muv_ref, muwd_ref, muad_ref,
                    w0_ref, wup_ref, a0_ref, aup_ref, kk_ref, ka_ref, seg_ref,
                    ro_ref, ko_ref, vo_ref, rt_ref, wt_ref, kt_ref, bt_ref, kkt_ref, vt_ref):
    mix = lambda x_ref, s_ref, mu_ref: x_ref[...] + (s_ref[...] - x_ref[...]) * mu_ref[...]
    rm = mix(r_ref, sr_ref, mur_ref)
    km = mix(k_ref, sk_ref, muk_ref)
    vm = mix(v_ref, sv_ref, muv_ref)
    wdm = mix(wd_ref, swd_ref, muwd_ref)
    adm = mix(ad_ref, sad_ref, muad_ref)
    logw, kk, k2, bv = _rwkv_mix(rm, km, vm, wdm, adm, w0_ref[...], wup_ref[...], a0_ref[...], aup_ref[...],
                                 kk_ref[...], ka_ref[...], seg_ref[...])
    ro_ref[...] = rm
    ko_ref[...] = k2
    vo_ref[...] = vm
    rt_ref[...] = rm.T
    wt_ref[...] = jnp.exp(logw).T
    kt_ref[...] = k2.T
    bt_ref[...] = bv.T
    kkt_ref[...] = kk.T
    vt_ref[...] = vm.T


def _wkv_pre(proj, rb0, shift, rp):
    ns = shift.shape[0]
    w512 = HB * 64
    col = lambda c0: pl.BlockSpec((ns, w512), lambda h: (rb0, c0 // w512 + h))
    lora = lambda c0: pl.BlockSpec((ns, LORA), lambda h: (rb0, c0 // LORA))
    scol = lambda c0: pl.BlockSpec((ns, w512), lambda h: (0, c0 // w512 + h))
    slora = lambda c0: pl.BlockSpec((ns, LORA), lambda h: (0, c0 // LORA))
    vec = pl.BlockSpec((1, w512), lambda h: (0, h))
    vec128 = pl.BlockSpec((1, LORA), lambda h: (0, 0))
    up = pl.BlockSpec((LORA, w512), lambda h: (0, h))
    row = pl.BlockSpec((ns, w512), lambda h: (0, h))
    tr = pl.BlockSpec((w512, ns), lambda h: (h, 0))
    return pl.pallas_call(
        _wkv_pre_kernel, grid=(RWKV_HEADS // HB,),
        in_specs=[col(_C_R), col(_C_K), col(_C_V), lora(_C_WD), lora(_C_AD),
                  scol(0), scol(4096), scol(8192), slora(12288), slora(12416),
                  vec, vec, vec, vec128, vec128, vec, up, vec, up, vec, vec,
                  pl.BlockSpec((LANE, LANE), lambda h: (0, 0))],
        out_specs=[row] * 3 + [tr] * 6,
        out_shape=[jax.ShapeDtypeStruct((ns, RWKV_WIDTH), F32)] * 3
        + [jax.ShapeDtypeStruct((RWKV_WIDTH, ns), F32)] * 6,
        compiler_params=_cp(("parallel",)), name="wkv_sample_pre",
    )(proj, proj, proj, proj, proj, shift, shift, shift, shift, shift,
      rp["mur"], rp["muk"], rp["muv"], rp["muwd"], rp["muad"],
      rp["w0"], rp["wup"], rp["a0"], rp["aup"], rp["kk"], rp["ka"], rp["seg"])


def _wkv_state_kernel(s_ref, r_ref, w_ref, k_ref, b_ref, kk_ref, v_ref, so_ref, y_ref):
    for hh in range(WKV_HPS):
        ch = slice(hh * 64, (hh + 1) * 64)
        r, w, k, bv, kk = r_ref[ch, :], w_ref[ch, :], k_ref[ch, :], b_ref[ch, :], kk_ref[ch, :]

        def vrow(vi, carry):
            s = s_ref[hh, vi]
            sk = jnp.sum(s * kk, axis=0, keepdims=True)
            sn = s * w - sk * bv + v_ref[pl.ds(hh * 64 + vi, 1), :] * k
            so_ref[hh, vi] = sn
            y_ref[pl.ds(hh * 64 + vi, 1), :] = jnp.sum(sn * r, axis=0, keepdims=True)
            return carry

        lax.fori_loop(0, 64, vrow, 0, unroll=WKV_UNROLL)


def _wkv_state(s, r, w, k, bvec, kk, v):
    ns = s.shape[-1]
    hps = WKV_HPS
    sblk = pl.BlockSpec((hps, 64, 64, ns), lambda i: (i, 0, 0, 0))
    ch = pl.BlockSpec((hps * 64, ns), lambda i: (i, 0))
    return pl.pallas_call(
        _wkv_state_kernel, grid=(RWKV_HEADS // hps,),
        in_specs=[sblk, ch, ch, ch, ch, ch, ch],
        out_specs=[sblk, ch],
        out_shape=[jax.ShapeDtypeStruct(s.shape, F32), jax.ShapeDtypeStruct((RWKV_WIDTH, ns), F32)],
        compiler_params=_cp(("parallel",)), name="wkv_sample_state",
    )(s, r, w, k, bvec, kk, v)


def _wkv_post_kernel(o_ref, r_ref, k_ref, v_ref, g_ref, lnw_ref, lnb_ref, rk_ref, seg_ref, dst_ref, y_ref):
    del dst_ref
    y_ref[...] = _rwkv_out(o_ref[...].T, r_ref[...], k_ref[...], v_ref[...], g_ref[...],
                           lnw_ref[...], lnb_ref[...], rk_ref[...], seg_ref[...])


def _wkv_post(o_t, r, k2, v, proj, rb0, rp, dst):
    ns = o_t.shape[1]
    w512 = HB * 64
    row = pl.BlockSpec((ns, w512), lambda h: (0, h))
    vec = pl.BlockSpec((1, w512), lambda h: (0, h))
    return pl.pallas_call(
        _wkv_post_kernel, grid=(RWKV_HEADS // HB,),
        in_specs=[pl.BlockSpec((w512, ns), lambda h: (h, 0)), row, row, row,
                  pl.BlockSpec((pl.Element(ns), pl.Element(w512)), lambda h: (rb0 * ns, pl.multiple_of(_C_G + h * w512, LANE))),
                  vec, vec, vec, pl.BlockSpec((LANE, LANE), lambda h: (0, 0)),
                  pl.BlockSpec(memory_space=pl.ANY)],
        out_specs=pl.BlockSpec((ns, w512), lambda h: (rb0, h)),
        out_shape=jax.ShapeDtypeStruct(dst.shape, BF16),
        input_output_aliases={9: 0},
        compiler_params=_cp(("parallel",)), name="wkv_sample_post",
    )(o_t, r, k2, v, proj, rp["lnw"], rp["lnb"], rp["rk"], rp["seg"], dst)


def _ret_pre_kernel(q_ref, k_ref, cos_ref, sin_ref, qo_ref, kh_ref, kl_ref):
    cos = cos_ref[0:1, :]
    sin = sin_ref[0:1, :]
    qo_ref[...] = _rotate(q_ref[...], cos, sin) * (RET_QK_DIM ** -0.5)
    hi, lo = _split(_rotate(k_ref[...], cos, sin).T, 2)
    kh_ref[...] = hi
    kl_ref[...] = lo


def _ret_pre(proj, rb0, ns, cos, sin):
    return pl.pallas_call(
        _ret_pre_kernel, grid=(RET_HEADS,),
        in_specs=[pl.BlockSpec((ns, RET_QK_DIM), lambda h: (rb0, h)),
                  pl.BlockSpec((ns, RET_QK_DIM), lambda h: (rb0, RET_HEADS + h)),
                  pl.BlockSpec((8, 128), lambda h: (0, 0)), pl.BlockSpec((8, 128), lambda h: (0, 0))],
        out_specs=[pl.BlockSpec((ns, RET_QK_DIM), lambda h: (0, h)),
                   pl.BlockSpec((RET_QK_DIM, ns), lambda h: (h, 0)),
                   pl.BlockSpec((RET_QK_DIM, ns), lambda h: (h, 0))],
        out_shape=[jax.ShapeDtypeStruct((ns, RET_QK_WIDTH), F32),
                   jax.ShapeDtypeStruct((RET_QK_WIDTH, ns), BF16),
                   jax.ShapeDtypeStruct((RET_QK_WIDTH, ns), BF16)],
        compiler_params=_cp(("parallel",)), name="ret_sample_pre",
    )(proj, proj, cos, sin)


def _ret_state_kernel(gd_ref, s_ref, kh_ref, kl_ref, q_ref, v_ref, so_ref, y_ref, kb, ysc):
    b = pl.program_id(0)
    e = _onehot_cols(b, LANE)
    kb[...] = (jnp.dot(kh_ref[...], e, preferred_element_type=F32)
               + jnp.dot(kl_ref[...], e, preferred_element_type=F32))

    def head(h, carry):
        gd = gd_ref[h]
        vrow = v_ref[0, pl.ds(h, 1), :]
        for rc in range(RET_QK_DIM // 64):
            kcol = kb[pl.ds(pl.multiple_of(h * RET_QK_DIM + rc * 64, 64), 64), :]
            for lc in range(RET_V_DIM // LANE):
                rs, ls = slice(rc * 64, (rc + 1) * 64), slice(lc * LANE, (lc + 1) * LANE)
                so_ref[0, h, rs, ls] = s_ref[0, h, rs, ls] * gd + kcol * vrow[:, ls]
        qrow = q_ref[0, pl.ds(h, 1), :]
        y8 = _dot(jnp.broadcast_to(qrow, (8, RET_QK_DIM)), so_ref[0, h])
        ysc[pl.ds(h, 1), :] = y8[0:1, :]
        return carry

    lax.fori_loop(0, RET_HEADS, head, 0)
    y_ref[0] = ysc[...]


def _ret_state(s, kh, kl, q, v):
    ns = s.shape[0]
    gd = jnp.exp(_ret_log_g())
    sblk = pl.BlockSpec((1, RET_HEADS, RET_QK_DIM, RET_V_DIM), lambda i, g: (i, 0, 0, 0))
    full = pl.BlockSpec((RET_QK_WIDTH, ns), lambda i, g: (0, 0))
    grid_spec = pltpu.PrefetchScalarGridSpec(
        num_scalar_prefetch=1, grid=(ns,),
        in_specs=[sblk, full, full,
                  pl.BlockSpec((1, RET_HEADS, RET_QK_DIM), lambda i, g: (i, 0, 0)),
                  pl.BlockSpec((1, RET_HEADS, RET_V_DIM), lambda i, g: (i, 0, 0))],
        out_specs=[sblk, pl.BlockSpec((1, RET_HEADS, RET_V_DIM), lambda i, g: (i, 0, 0))],
        scratch_shapes=[pltpu.VMEM((RET_QK_WIDTH, LANE), F32), pltpu.VMEM((RET_HEADS, RET_V_DIM), F32)],
    )
    return pl.pallas_call(
        _ret_state_kernel, grid_spec=grid_spec,
        out_shape=[jax.ShapeDtypeStruct(s.shape, F32), jax.ShapeDtypeStruct((ns, RET_HEADS, RET_V_DIM), F32)],
        compiler_params=_cp(("parallel",)), name="ret_sample_state",
    )(gd, s, kh, kl, q, v)


def _ret_post_kernel(y_ref, g_ref, gnw_ref, dst_ref, o_ref):
    del dst_ref
    y = y_ref[...]
    mu = jnp.mean(y, -1, keepdims=True)
    d = y - mu
    var = jnp.mean(d * d, -1, keepdims=True)
    o = d * lax.rsqrt(var + RET_GN_EPS) * gnw_ref[...]
    o_ref[...] = (o * _silu(g_ref[...])).astype(BF16)


def _ret_post(y, proj, rb0, gn_w, dst):
    ns = y.shape[0]
    blk = pl.BlockSpec((ns, RET_V_DIM), lambda h: (0, h))
    return pl.pallas_call(
        _ret_post_kernel, grid=(RET_HEADS,),
        in_specs=[blk, pl.BlockSpec((ns, RET_V_DIM), lambda h: (rb0, (2 * RET_QK_WIDTH + RET_WIDTH) // RET_V_DIM + h)),
                  pl.BlockSpec((1, RET_V_DIM), lambda h: (0, h)), pl.BlockSpec(memory_space=pl.ANY)],
        out_specs=pl.BlockSpec((ns, RET_V_DIM), lambda h: (rb0, h)),
        out_shape=jax.ShapeDtypeStruct(dst.shape, BF16),
        input_output_aliases={3: 0},
        compiler_params=_cp(("parallel",)), name="ret_sample_post",
    )(y, proj, gn_w.reshape(1, -1), dst)


def _ab_layer(xp, xs, xb, nb, l, ns, conv_s, ssm_s, shift_s, wkv_s, w_in, sp, rp, w_out, ln_w, ln_b):
    mp = nb * l
    rb0 = mp // 128
    wt = w_in.T.astype(BF16)
    n_lo = AB_DT0 // AB_TN
    starts = [j * AB_TN for j in range(n_lo)] + [AB_DT0 + SSD_HEADS + j * AB_TN
                                                 for j in range((AB_MAIN - AB_DT0) // AB_TN)]
    proj = _matmul_wt(xb, wt, starts, AB_TN, "ab_in_proj")
    pdt = _matmul_wt(xb, wt, [AB_DT0], LANE, "ab_dt_proj")[:, :SSD_HEADS]
    m = proj.shape[0]
    dt3 = pdt.reshape(m, SSD_GROUPS, SSD_HPG)
    dtc = dt3.transpose(1, 0, 2)
    dtr = dt3.transpose(1, 2, 0)

    ya, ssm_p = _ssd_prompt(proj, dtc, dtr, nb, l, sp)
    yb, wkv_p = _rwkv_prompt(proj, nb, l, rp)
    tail = lambda n, c0, c1: jnp.stack([proj[(b + 1) * l - n:(b + 1) * l, c0:c1] for b in range(nb)])
    conv_p = tail(SSD_CONV - 1, _C_XS, _C_R)
    shift_p = tail(1, _C_R, _C_G)

    xa, ba, ca, dt_s, dec_s = _ssd_pre(proj, rb0, ns, conv_s.transpose(1, 0, 2), dtc, sp)
    flat = lambda t: t.transpose(1, 0, 2).reshape(ns * SSD_HEADS)
    ssm_n, y_s = _ssm_state(flat(dt_s), flat(dec_s), ssm_s, xa.reshape(ns, 1, SSD_WIDTH),
                            ba.transpose(1, 0, 2), ca.transpose(1, 0, 2))
    ya = _ssd_post(y_s.reshape(ns, SSD_WIDTH), xa, proj, rb0, sp, ya)
    conv_n = jnp.concatenate([conv_s[:, 1:], proj[mp:, None, _C_XS:_C_R]], axis=1)

    r_s, k_s, v_s, r_t, w_t, k_t, b_t, kk_t, v_t = _wkv_pre(proj, rb0, shift_s.reshape(ns, SHIFT_DIM), rp)
    wkv_t, o_t = _wkv_state(wkv_s.transpose(1, 2, 3, 0), r_t, w_t, k_t, b_t, kk_t, v_t)
    wkv_n = wkv_t.transpose(3, 0, 1, 2)
    yb = _wkv_post(o_t, r_s, k_s, v_s, proj, rb0, rp, yb)
    shift_n = proj[mp:, None, _C_R:_C_G]

    out = _matmul2(ya, yb, w_out.astype(BF16), "ab_out_proj")
    x_new, xb_new = _deepnorm_first(xp, xs, out, ln_w, ln_b, "ab_deepnorm")
    return x_new, xb_new, (conv_p, ssm_p, shift_p, wkv_p), (conv_n, ssm_n, shift_n, wkv_n)


def _ret_layer(x, xb, nb, l, ns, ret_s, w_in, gn_w, w_out, ln_w, ln_b):
    mp = nb * l
    rb0 = mp // 128
    proj = _matmul(xb, w_in.astype(BF16), "ret_in_proj")
    cos, sin = _trig(jnp.arange(l))
    cos_s, sin_s = _trig(jnp.full((8,), PAST_LEN))
    y, ret_p = _ret_prompt(proj, cos, sin, gn_w, nb, l)

    q_s, kh, kl = _ret_pre(proj, rb0, ns, cos_s, sin_s)
    v_s = proj[mp:, 2 * RET_QK_WIDTH:2 * RET_QK_WIDTH + RET_WIDTH].reshape(ns, RET_HEADS, RET_V_DIM)
    ret_n, o_s = _ret_state(ret_s, kh, kl, q_s.reshape(ns, RET_HEADS, RET_QK_DIM), v_s)
    y = _ret_post(o_s.reshape(ns, RET_WIDTH), proj, rb0, gn_w, y)

    out = _matmul(y, w_out.astype(BF16), "ret_out_proj")
    y_p, y_s = _deepnorm_last(x, out, ln_w, ln_b, ns, "ret_deepnorm")
    return y_p, y_s, ret_p, ret_n


def kernel(x_prompt, x_sample, state_conv, state_ssm, state_shift, state_wkv, state_ret, ab_w_in, ssd_conv_w, ssd_conv_b, ssd_dt_bias, ssd_a_log, ssd_d, ssd_norm_w, rwkv_mu, rwkv_w0, rwkv_w_up, rwkv_a0, rwkv_a_up, rwkv_k_k, rwkv_k_a, rwkv_r_k, rwkv_lnx_w, rwkv_lnx_b, ab_w_out, ab_ln_w, ab_ln_b, ret_w_in, ret_gn_w, ret_w_out, ret_ln_w, ret_ln_b):
    nb, l, d = x_prompt.shape
    ns = x_sample.shape[0]
    assert x_sample.shape[1] == 1 and l % CHUNK == 0 and ns % LANE == 0 and ns == LANE
    mp = nb * l
    xp, xs = x_prompt.reshape(mp, d), x_sample.reshape(ns, d)
    xb = jnp.concatenate([xp.astype(BF16), xs.astype(BF16)], axis=0)

    sp = _ssd_params(ssd_conv_w[0], ssd_conv_b[0], ssd_dt_bias[0], ssd_a_log[0], ssd_d[0], ssd_norm_w[0])
    rp = _rwkv_params(rwkv_mu[0], rwkv_w0[0], rwkv_w_up[0], rwkv_a0[0], rwkv_a_up[0], rwkv_k_k[0], rwkv_k_a[0],
                      rwkv_r_k[0], rwkv_lnx_w[0], rwkv_lnx_b[0])
    x, xb, pst, sst = _ab_layer(xp, xs, xb, nb, l, ns, state_conv[0], state_ssm[0], state_shift[0], state_wkv[0],
                                ab_w_in[0], sp, rp, ab_w_out[0], ab_ln_w[0], ab_ln_b[0])
    y_p, y_s, ret_p, ret_n = _ret_layer(x, xb, nb, l, ns, state_ret[0], ret_w_in[0], ret_gn_w[0], ret_w_out[0],
                                        ret_ln_w[0], ret_ln_b[0])
    y_prompt = y_p.reshape(nb, l, d)
    y_sample = y_s.reshape(ns, 1, d)
    st = lambda t: t[None]
    return (y_prompt, y_sample,
            st(pst[0]), st(pst[1]), st(pst[2]), st(pst[3]), st(ret_p),
            st(sst[0]), st(sst[1]), st(sst[2]), st(sst[3]), st(ret_n))
```

```python
import functools
import math

import jax
import jax.numpy as jnp
import numpy as np
from jax import lax
from jax.experimental import pallas as pl
from jax.experimental.pallas import tpu as pltpu

F32 = jnp.float32
BF16 = jnp.bfloat16

D_MODEL = 4096
DEPTH = 2
PAST_LEN = 16384
SSD_WIDTH = 4096
SSD_HEAD_DIM = 64
SSD_HEADS = 64
SSD_GROUPS = 8
SSD_HPG = 8
SSD_STATE = 128
SSD_CONV = 4
SSD_CONV_DIM = SSD_WIDTH + 2 * SSD_GROUPS * SSD_STATE
RWKV_WIDTH = 4096
RWKV_HEAD_DIM = 64
RWKV_HEADS = 64
LORA = 128
SHIFT_DIM = 3 * RWKV_WIDTH + 2 * LORA
RET_HEADS = 16
RET_QK_DIM = 256
RET_V_DIM = 512
RET_QK_WIDTH = 4096
RET_WIDTH = 8192
ROPE_BASE = 10000.0
CHUNK = 128
ALPHA = (2 * DEPTH) ** 0.25
LN_EPS = 1e-5
RMS_EPS = 1e-5
RWKV_GN_EPS = 64e-5
RET_GN_EPS = 1e-6

LANE = 128
VMEM_LIMIT = 56 * 1024 * 1024
WKV_CHUNK = 64
RW_ROWS = 256
HB = 8
RET_HB = 4
SSM_BT = 2
SSD_CPS = 2
WKV_HPS = 2
WKV_UNROLL = 8

_C_Z, _C_XS, _C_B, _C_C = 0, 4096, 8192, 9216
_C_R, _C_K, _C_V, _C_WD, _C_AD, _C_G = 10240, 14336, 18432, 22528, 22656, 22784
AB_MAIN = 26880
AB_DT0 = 10240
AB_TN = 1280


def _cp(sem):
    return pltpu.CompilerParams(dimension_semantics=sem, vmem_limit_bytes=VMEM_LIMIT)


def _silu(x):
    return x * jax.nn.sigmoid(x)


def _softplus(x):
    return jnp.maximum(x, 0.0) + jnp.log1p(jnp.exp(-jnp.abs(x)))


def _dot(a, b):
    return jnp.dot(a.astype(BF16), b.astype(BF16), preferred_element_type=F32)


def _dot_nt(a, b):
    return lax.dot_general(a.astype(BF16), b.astype(BF16), (((1,), (1,)), ((), ())),
                           preferred_element_type=F32)


def _dot_tn(a, b):
    return lax.dot_general(a.astype(BF16), b.astype(BF16), (((0,), (0,)), ((), ())),
                           preferred_element_type=F32)


def _split(x, n):
    parts, r = [], x
    for _ in range(n):
        h = r.astype(BF16)
        parts.append(h)
        r = r - h.astype(F32)
    return parts


def _dot01(m01, x, n=3):
    return sum(jnp.dot(m01, p, preferred_element_type=F32) for p in _split(x, n))


def _dot01_r(x, m01, n=2):
    return sum(jnp.dot(p, m01, preferred_element_type=F32) for p in _split(x, n))


def _segsum(x, seg):
    r, w = x.shape
    nt = w // LANE
    tall = jnp.concatenate([x[:, i * LANE:(i + 1) * LANE] for i in range(nt)], axis=0)
    s = _dot01_r(tall, seg)
    return jnp.concatenate([s[i * r:(i + 1) * r] for i in range(nt)], axis=1)


def _onehot_cols(b, n):
    rows = lax.broadcasted_iota(jnp.int32, (LANE, n), 0)
    return jnp.where(rows == b, 1.0, 0.0).astype(BF16)


def _mm_kernel(x_ref, w_ref, o_ref):
    o_ref[...] = jnp.dot(x_ref[...], w_ref[...], preferred_element_type=F32)


def _pick_tile(n, prefs):
    for t in prefs:
        if n % t == 0:
            return t
    return n


def _matmul(x, w, name):
    m, k = x.shape
    n = w.shape[1]
    tm = _pick_tile(m, (640, 512, 256, 128))
    tn = _pick_tile(n, (1280, 1024, 512, 256, 128) if k <= 4096 else (512, 256, 128))
    return pl.pallas_call(
        _mm_kernel,
        grid=(n // tn, m // tm),
        in_specs=[pl.BlockSpec((tm, k), lambda j, i: (i, 0)),
                  pl.BlockSpec((k, tn), lambda j, i: (0, j))],
        out_specs=pl.BlockSpec((tm, tn), lambda j, i: (i, j)),
        out_shape=jax.ShapeDtypeStruct((m, n), F32),
        compiler_params=_cp(("parallel", "parallel")),
        name=name,
    )(x, w)


def _mm_wt_kernel(st_ref, x_ref, wt_ref, o_ref):
    del st_ref
    o_ref[...] = lax.dot_general(x_ref[...], wt_ref[...], (((1,), (1,)), ((), ())), preferred_element_type=F32)


def _matmul_wt(x, wt, row_starts, tn, name):
    m, k = x.shape
    nt = len(row_starts)
    tm = _pick_tile(m, (640, 512, 256, 128))
    starts = jnp.asarray(row_starts, jnp.int32)
    grid_spec = pltpu.PrefetchScalarGridSpec(
        num_scalar_prefetch=1,
        grid=(nt, m // tm),
        in_specs=[pl.BlockSpec((tm, k), lambda j, i, st: (i, 0)),
                  pl.BlockSpec((pl.Element(tn), pl.Element(k)), lambda j, i, st: (pl.multiple_of(st[j], 64), 0))],
        out_specs=pl.BlockSpec((tm, tn), lambda j, i, st: (i, j)),
    )
    return pl.pallas_call(
        _mm_wt_kernel, grid_spec=grid_spec,
        out_shape=jax.ShapeDtypeStruct((m, nt * tn), F32),
        compiler_params=_cp(("parallel", "parallel")),
        name=name,
    )(starts, x, wt)


def _mm2_kernel(a_ref, b_ref, w_ref, o_ref):
    ka = a_ref.shape[1]
    o_ref[...] = (jnp.dot(a_ref[...], w_ref[0:ka, :], preferred_element_type=F32)
                  + jnp.dot(b_ref[...], w_ref[ka:, :], preferred_element_type=F32))


def _matmul2(a, b, w, name):
    m, ka = a.shape
    kb = b.shape[1]
    n = w.shape[1]
    tm = _pick_tile(m, (640, 512, 256, 128))
    tn = _pick_tile(n, (512, 256, 128))
    return pl.pallas_call(
        _mm2_kernel,
        grid=(n // tn, m // tm),
        in_specs=[pl.BlockSpec((tm, ka), lambda j, i: (i, 0)),
                  pl.BlockSpec((tm, kb), lambda j, i: (i, 0)),
                  pl.BlockSpec((ka + kb, tn), lambda j, i: (0, j))],
        out_specs=pl.BlockSpec((tm, tn), lambda j, i: (i, j)),
        out_shape=jax.ShapeDtypeStruct((m, n), F32),
        compiler_params=_cp(("parallel", "parallel")),
        name=name,
    )(a, b, w)


def _post_norm(x, o, w, b):
    h = ALPHA * x + o
    mu = jnp.mean(h, -1, keepdims=True)
    d = h - mu
    var = jnp.mean(d * d, -1, keepdims=True)
    return d * lax.rsqrt(var + LN_EPS) * w + b


def _ln_first_kernel(xp_ref, xs_ref, o_ref, w_ref, b_ref, y_ref, yb_ref):
    is_sample = pl.program_id(0) == pl.num_programs(0) - 1
    x = jnp.where(is_sample, xs_ref[...], xp_ref[...])
    y = _post_norm(x, o_ref[...], w_ref[...], b_ref[...])
    y_ref[...] = y
    yb_ref[...] = y.astype(BF16)


def _deepnorm_first(xp, xs, o, w, b, name):
    mp, d = xp.shape
    ns = xs.shape[0]
    npt = mp // ns
    row = pl.BlockSpec((ns, d), lambda i: (i, 0))
    vec = pl.BlockSpec((1, d), lambda i: (0, 0))
    return pl.pallas_call(
        _ln_first_kernel,
        grid=(npt + 1,),
        in_specs=[pl.BlockSpec((ns, d), lambda i: (jnp.minimum(i, npt - 1), 0)),
                  pl.BlockSpec((ns, d), lambda i: (0, 0)), row, vec, vec],
        out_specs=[row, row],
        out_shape=[jax.ShapeDtypeStruct((mp + ns, d), F32), jax.ShapeDtypeStruct((mp + ns, d), BF16)],
        compiler_params=_cp(("parallel",)),
        name=name,
    )(xp, xs, o, w.reshape(1, d), b.reshape(1, d))


def _ln_last_kernel(x_ref, o_ref, w_ref, b_ref, yp_ref, ys_ref):
    is_sample = pl.program_id(0) == pl.num_programs(0) - 1
    y = _post_norm(x_ref[...], o_ref[...], w_ref[...], b_ref[...])

    @pl.when(jnp.logical_not(is_sample))
    def _prompt():
        yp_ref[...] = y

    @pl.when(is_sample)
    def _sample():
        ys_ref[...] = y


def _deepnorm_last(x, o, w, b, ns, name):
    m, d = x.shape
    npt = m // ns - 1
    row = pl.BlockSpec((ns, d), lambda i: (i, 0))
    vec = pl.BlockSpec((1, d), lambda i: (0, 0))
    return pl.pallas_call(
        _ln_last_kernel,
        grid=(npt + 1,),
        in_specs=[row, row, vec, vec],
        out_specs=[pl.BlockSpec((ns, d), lambda i: (jnp.minimum(i, npt - 1), 0)),
                   pl.BlockSpec((ns, d), lambda i: (0, 0))],
        out_shape=[jax.ShapeDtypeStruct((npt * ns, d), F32), jax.ShapeDtypeStruct((ns, d), F32)],
        compiler_params=_cp(("arbitrary",)),
        name=name,
    )(x, o, w.reshape(1, d), b.reshape(1, d))


def _ssd_prompt_kernel(z_ref, xs_ref, b_ref, c_ref, dtc_ref, dtr_ref,
                       cwx_ref, cwb_ref, cwc_ref, cbx_ref, cbb_ref, cbc_ref,
                       dtbc_ref, dtbr_ref, alc_ref, alr_ref, dsk_ref, nw_ref, tri_ref, rep64_ref, rep128_ref,
                       wi_ref, y_ref, s_ref, wo_ref, bufx, bufb, bufc):
    L = CHUNK
    c = pl.program_id(2)
    wo_ref[...] = wi_ref[...].astype(BF16)

    @pl.when(c == 0)
    def _init():
        for buf in (bufx, bufb, bufc):
            buf[0:8, :] = jnp.zeros((8, buf.shape[1]), F32)
        s_ref[...] = jnp.zeros(s_ref.shape, F32)

    tri = tri_ref[...]
    li = lax.broadcasted_iota(jnp.int32, (L, L), 0)
    si = lax.broadcasted_iota(jnp.int32, (L, L), 1)
    causal = li >= si
    lo = lax.broadcasted_iota(jnp.int32, (1, LANE), 1) < 64
    pairs = range(SSD_HPG // 2)
    tile = lambda a, p: a[:, p * LANE:(p + 1) * LANE]
    for ci in range(SSD_CPS):
        rows = slice(ci * L, (ci + 1) * L)

        def conv(u_ref, buf, w_ref, bias_ref):
            buf[8:8 + L, :] = u_ref[rows, :]
            acc = bias_ref[...] + buf[5:5 + L, :] * w_ref[0:1, :]
            for k in range(1, SSD_CONV):
                acc = acc + buf[5 + k:5 + k + L, :] * w_ref[k:k + 1, :]
            buf[0:8, :] = buf[L:L + 8, :]
            return _silu(acc)

        xs = conv(xs_ref, bufx, cwx_ref, cbx_ref)
        bm = conv(b_ref, bufb, cwb_ref, cbb_ref)
        cm = conv(c_ref, bufc, cwc_ref, cbc_ref)
        dtc = _softplus(dtc_ref[0, rows, :] + dtbc_ref[0])
        dtr = _softplus(dtr_ref[0, :, rows] + dtbr_ref[0])
        adt_c = dtc * (-jnp.exp(alc_ref[0]))
        adt_r = dtr * (-jnp.exp(alr_ref[0]))
        cum_c = _dot01(tri, adt_c)
        cum_r = sum(lax.dot_general(p, tri, (((1,), (1,)), ((), ())), preferred_element_type=F32)
                    for p in _split(adt_r, 3))
        cb = _dot_nt(cm, bm)
        dt_x = _dot01_r(dtc, rep64_ref[...], 3)
        cum_x = _dot01_r(cum_c, rep64_ref[...], 3)
        cum_b = _dot01_r(cum_c, rep128_ref[...], 3)
        xdt = xs * dt_x
        xdt_tail = xdt * jnp.exp(cum_x[L - 1:L, :] - cum_x)
        s_old = [s_ref[0, r] for r in range(SSD_HPG)]
        decay = [jnp.exp(jnp.where(causal, tile(cum_b, r) - cum_r[r:r + 1, :], -jnp.inf))
                 for r in range(SSD_HPG)]
        x_lo = [jnp.where(lo, tile(xdt, p), 0.0) for p in pairs]
        x_hi = [jnp.where(lo, 0.0, tile(xdt, p)) for p in pairs]
        y_in = [_dot(cb * decay[2 * p], x_lo[p]) + _dot(cb * decay[2 * p + 1], x_hi[p]) for p in pairs]
        y_st = [_dot_nt(cm, jnp.concatenate([s_old[2 * p], s_old[2 * p + 1]], 0)) for p in pairs]
        s_in = [_dot_tn(tile(xdt_tail, p), bm) for p in pairs]
        for r in range(SSD_HPG):
            half = s_in[r // 2][(r % 2) * 64:(r % 2 + 1) * 64]
            s_ref[0, r] = s_old[r] * jnp.exp(cum_c[L - 1:L, r:r + 1]) + half
        y = jnp.concatenate(y_in, axis=1) + jnp.concatenate(y_st, axis=1) * jnp.exp(cum_x) + xs * dsk_ref[...]
        y = y * _silu(z_ref[rows, :])
        y = y * lax.rsqrt(jnp.mean(y * y, -1, keepdims=True) + RMS_EPS) * nw_ref[...]
        y_ref[rows, :] = y.astype(BF16)


def _ssd_params(conv_w, conv_b, dt_bias, a_log, d_skip, norm_w):
    g = SSD_GROUPS
    return dict(
        cwx=conv_w[:, :4096], cwb=conv_w[:, 4096:5120], cwc=conv_w[:, 5120:],
        cbx=conv_b[:4096].reshape(1, -1), cbb=conv_b[4096:5120].reshape(1, -1), cbc=conv_b[5120:].reshape(1, -1),
        dtbc=dt_bias.reshape(g, 1, 8), dtbr=dt_bias.reshape(g, 8, 1),
        alc=a_log.reshape(g, 1, 8), alr=a_log.reshape(g, 8, 1),
        dsk=jnp.repeat(d_skip, SSD_HEAD_DIM).reshape(1, -1), nw=norm_w.reshape(1, -1))


def _ride_specs(w, nsteps, lin):
    r, n = w.shape
    assert r % nsteps == 0 and (r // nsteps) % 16 == 0
    blk = pl.BlockSpec((r // nsteps, n), lambda *idx: (lin(*idx), 0))
    return blk, blk, jax.ShapeDtypeStruct((r, n), BF16)


def _ssd_prompt(proj, dtc, dtr, nb, l, sp, w_ride):
    rows = CHUNK * SSD_CPS
    nc = l // rows
    rb = lambda b, g, c: b * nc + c
    tri = jnp.tril(jnp.ones((CHUNK, CHUNK), BF16))
    wi_spec, wo_spec, wo_shape = _ride_specs(w_ride, nb * SSD_GROUPS * nc,
                                             lambda b, g, c: (b * SSD_GROUPS + g) * nc + c)
    in_specs = [
        pl.BlockSpec((rows, 512), lambda b, g, c: (rb(b, g, c), _C_Z // 512 + g)),
        pl.BlockSpec((rows, 512), lambda b, g, c: (rb(b, g, c), _C_XS // 512 + g)),
        pl.BlockSpec((rows, 128), lambda b, g, c: (rb(b, g, c), _C_B // 128 + g)),
        pl.BlockSpec((rows, 128), lambda b, g, c: (rb(b, g, c), _C_C // 128 + g)),
        pl.BlockSpec((1, rows, 8), lambda b, g, c: (g, rb(b, g, c), 0)),
        pl.BlockSpec((1, 8, rows), lambda b, g, c: (g, 0, rb(b, g, c))),
        pl.BlockSpec((SSD_CONV, 512), lambda b, g, c: (0, g)),
        pl.BlockSpec((SSD_CONV, 128), lambda b, g, c: (0, g)),
        pl.BlockSpec((SSD_CONV, 128), lambda b, g, c: (0, g)),
        pl.BlockSpec((1, 512), lambda b, g, c: (0, g)),
        pl.BlockSpec((1, 128), lambda b, g, c: (0, g)),
        pl.BlockSpec((1, 128), lambda b, g, c: (0, g)),
        pl.BlockSpec((1, 1, 8), lambda b, g, c: (g, 0, 0)),
        pl.BlockSpec((1, 8, 1), lambda b, g, c: (g, 0, 0)),
        pl.BlockSpec((1, 1, 8), lambda b, g, c: (g, 0, 0)),
        pl.BlockSpec((1, 8, 1), lambda b, g, c: (g, 0, 0)),
        pl.BlockSpec((1, 512), lambda b, g, c: (0, g)),
        pl.BlockSpec((1, 512), lambda b, g, c: (0, g)),
        pl.BlockSpec((CHUNK, CHUNK), lambda b, g, c: (0, 0)),
        pl.BlockSpec((SSD_HPG, SSD_HPG * 64), lambda b, g, c: (0, 0)),
        pl.BlockSpec((SSD_HPG, SSD_HPG * LANE), lambda b, g, c: (0, 0)),
        wi_spec,
    ]
    rep64 = jnp.asarray(np.kron(np.eye(SSD_HPG), np.ones((1, 64))), BF16)
    rep128 = jnp.asarray(np.kron(np.eye(SSD_HPG), np.ones((1, LANE))), BF16)
    out_specs = [pl.BlockSpec((rows, 512), lambda b, g, c: (rb(b, g, c), g)),
                 pl.BlockSpec((1, SSD_HPG, SSD_HEAD_DIM, SSD_STATE), lambda b, g, c: (b, g, 0, 0)),
                 wo_spec]
    return pl.pallas_call(
        _ssd_prompt_kernel,
        grid=(nb, SSD_GROUPS, nc),
        in_specs=in_specs,
        out_specs=out_specs,
        out_shape=[jax.ShapeDtypeStruct((proj.shape[0], SSD_WIDTH), BF16),
                   jax.ShapeDtypeStruct((nb, SSD_HEADS, SSD_HEAD_DIM, SSD_STATE), F32),
                   wo_shape],
        scratch_shapes=[pltpu.VMEM((CHUNK + 8, 512), F32), pltpu.VMEM((CHUNK + 8, 128), F32),
                        pltpu.VMEM((CHUNK + 8, 128), F32)],
        compiler_params=_cp(("parallel", "parallel", "arbitrary")),
        name="ssd_prompt",
    )(proj, proj, proj, proj, dtc, dtr, sp["cwx"], sp["cwb"], sp["cwc"], sp["cbx"], sp["cbb"], sp["cbc"],
      sp["dtbc"], sp["dtbr"], sp["alc"], sp["alr"], sp["dsk"], sp["nw"], tri, rep64, rep128, w_ride)


def _rwkv_mix(rm, km, vm, wdm, adm, w0, wup, a0, aup, k_k, k_a, seg):
    wlog = -_softplus(-(w0 + _dot(jnp.tanh(wdm), wup))) - 0.5
    logw = -jnp.exp(wlog)
    aa = jax.nn.sigmoid(a0 + _dot(adm, aup))
    kkr = km * k_k
    kk = kkr * lax.rsqrt(jnp.maximum(_segsum(kkr * kkr, seg), 1e-24))
    k2 = km * (1.0 + (aa - 1.0) * k_a)
    return logw, kk, k2, kk * aa


def _rwkv_out(o, rm, k2, vm, g, lnw, lnb, rk, seg):
    inv = 1.0 / RWKV_HEAD_DIM
    mean = _segsum(o, seg) * inv
    d = o - mean
    var = _segsum(d * d, seg) * inv
    on = d * lax.rsqrt(var + RWKV_GN_EPS) * lnw + lnb
    bonus = _segsum(rm * k2 * rk, seg) * vm
    return ((on + bonus) * _silu(g)).astype(BF16)


def _rwkv_prompt_kernel(r_ref, k_ref, v_ref, g_ref, wd_ref, ad_ref,
                        mur_ref, muk_ref, muv_ref, muwd_ref, muad_ref,
                        w0_ref, wup_ref, a0_ref, aup_ref, kk_ref, ka_ref, lnw_ref, lnb_ref, rk_ref,
                        seg_ref, tri_ref, wi_ref,
                        y_ref, s_ref, wo_ref, cr, ck, cv, cwd, cad):
    R, C = RW_ROWS, WKV_CHUNK
    wo_ref[...] = wi_ref[...].astype(BF16)
    c = pl.program_id(2)

    @pl.when(c == 0)
    def _init():
        for buf in (cr, ck, cv, cwd, cad):
            buf[...] = jnp.zeros(buf.shape, F32)
        s_ref[...] = jnp.zeros(s_ref.shape, F32)

    row0 = lax.broadcasted_iota(jnp.int32, (R, 1), 0) == 0

    def shift(x_ref, carry, mu_ref):
        x = x_ref[...]
        prev = jnp.where(row0, carry[0:1, :], pltpu.roll(x, 1, 0))
        carry[0:1, :] = x[R - 1:R, :]
        return x + (prev - x) * mu_ref[...]

    rm = shift(r_ref, cr, mur_ref)
    km = shift(k_ref, ck, muk_ref)
    vm = shift(v_ref, cv, muv_ref)
    wdm = shift(wd_ref, cwd, muwd_ref)
    adm = shift(ad_ref, cad, muad_ref)
    seg = seg_ref[...]
    logw, kk, k2, bv = _rwkv_mix(rm, km, vm, wdm, adm, w0_ref[...], wup_ref[...], a0_ref[...], aup_ref[...],
                                 kk_ref[...], ka_ref[...], seg)

    tri = tri_ref[...]
    li = lax.broadcasted_iota(jnp.int32, (C, LANE), 0)
    lane = lax.broadcasted_iota(jnp.int32, (C, LANE), 1)
    si = lane % 64
    strict = li > si
    incl = li >= si
    eye = jnp.where(li == si, 1.0, 0.0)
    lo = lane < 64
    rlo = lax.broadcasted_iota(jnp.int32, (LANE, LANE), 0) < 64
    llo = lax.broadcasted_iota(jnp.int32, (LANE, LANE), 1) < 64
    same = rlo == llo

    def bd(a):
        ab = a.astype(BF16)
        zero = jnp.zeros_like(ab)
        return jnp.concatenate([jnp.where(lo, ab, zero), jnp.where(lo, zero, ab)], axis=0)

    nsc = R // C
    prep = []
    for sc in range(nsc):
        rows = slice(sc * C, (sc + 1) * C)
        lw = logw[rows]
        cs = _dot01(tri, lw)
        cl = cs[C - 1:C, :]
        e_tail = jnp.exp(cl - cs)
        e_neg = jnp.exp(-cs)
        prep.append(dict(
            bt=kk[rows] * jnp.exp(cs - lw),
            bb=bv[rows] * e_neg,
            kt=k2[rows] * e_neg,
            rt=rm[rows] * jnp.exp(cs),
            bh=bv[rows] * e_tail,
            kh=k2[rows] * e_tail,
            pc=jnp.exp(cl), v=vm[rows]))
    npair = HB // 2
    keys = [(sc, p) for sc in range(nsc) for p in range(npair)]
    part = lambda name: {k: prep[k[0]][name][:, k[1] * LANE:(k[1] + 1) * LANE] for k in keys}
    bt, bb, kt, rt, bh, kh, vh, pc = (part(n) for n in ("bt", "bb", "kt", "rt", "bh", "kh", "v", "pc"))
    lhs = {k: jnp.concatenate([bt[k], rt[k]], 0) for k in keys}
    gb = {k: _dot_nt(lhs[k], bd(bb[k])) for k in keys}
    gk = {k: _dot_nt(lhs[k], bd(kt[k])) for k in keys}
    lk = {k: jnp.where(strict, gk[k][0:C], 0.0) for k in keys}
    rb = {k: jnp.where(incl, gb[k][C:2 * C], 0.0) for k in keys}
    rkm = {k: jnp.where(incl, gk[k][C:2 * C], 0.0) for k in keys}
    x = {k: jnp.where(strict, -gb[k][0:C], 0.0) for k in keys}
    t = {k: eye + x[k] for k in keys}
    for _ in range(int(math.log2(C)) - 1):
        x = {k: _dot(x[k], bd(x[k])) for k in keys}
        t = {k: t[k] + _dot(t[k], bd(x[k])) for k in keys}
    bdv = {k: bd(vh[k]) for k in keys}
    lkv = {k: _dot(lk[k], bdv[k]) for k in keys}
    tb = {k: _dot(t[k], bd(bt[k])) for k in keys}
    tlv = {k: _dot(t[k], bd(lkv[k])) for k in keys}
    rq = {k: rt[k] - _dot(rb[k], bd(tb[k])) for k in keys}
    yc = {k: _dot(rkm[k], bdv[k]) - _dot(rb[k], bd(tlv[k])) for k in keys}
    mq = {k: jnp.where(same, _dot_tn(tb[k], bh[k]), 0.0).astype(BF16) for k in keys}
    nf = {k: _dot_tn(jnp.concatenate([vh[k], -tlv[k]], 0), jnp.concatenate([kh[k], bh[k]], 0)) for k in keys}
    vlo = lax.broadcasted_iota(jnp.int32, (64, LANE), 1) < 64
    nn = {k: jnp.where(vlo, nf[k][0:64], nf[k][64:128]) for k in keys}
    st = [jnp.concatenate([s_ref[0, 2 * p], s_ref[0, 2 * p + 1]], axis=1) for p in range(npair)]
    o_chunks = []
    for sc in range(nsc):
        ys = [_dot_nt(rq[sc, p], bd(st[p])) + yc[sc, p] for p in range(npair)]
        st = [st[p] * pc[sc, p] - _dot(st[p], mq[sc, p]) + nn[sc, p] for p in range(npair)]
        o_chunks.append(jnp.concatenate(ys, 1))
    for p in range(npair):
        s_ref[0, 2 * p] = st[p][:, 0:64]
        s_ref[0, 2 * p + 1] = st[p][:, 64:128]
    o = jnp.concatenate(o_chunks, 0)
    y_ref[...] = _rwkv_out(o, rm, k2, vm, g_ref[...], lnw_ref[...], lnb_ref[...], rk_ref[...], seg)


def _rwkv_params(mu, w0, w_up, a0, a_up, k_k, k_a, r_k, lnx_w, lnx_b):
    v = lambda t: t.reshape(1, -1)
    return dict(
        mur=v(mu[0:4096]), muk=v(mu[4096:8192]), muv=v(mu[8192:12288]),
        muwd=v(mu[12288:12416]), muad=v(mu[12416:12544]),
        w0=v(w0), wup=w_up.astype(BF16), a0=v(a0), aup=a_up.astype(BF16), kk=v(k_k), ka=v(k_a),
        lnw=v(lnx_w), lnb=v(lnx_b), rk=v(r_k),
        seg=jnp.asarray(np.kron(np.eye(LANE // 64), np.ones((64, 64))), BF16))


def _rwkv_prompt(proj, nb, l, rp, w_ride):
    nr = l // RW_ROWS
    rb = lambda b, h, c: b * nr + c
    w512 = HB * 64
    nhg = RWKV_HEADS // HB
    wi_spec, wo_spec, wo_shape = _ride_specs(w_ride, nb * nhg * nr, lambda b, h, c: (b * nhg + h) * nr + c)
    col = lambda c0: pl.BlockSpec((RW_ROWS, w512), lambda b, h, c: (rb(b, h, c), c0 // w512 + h))
    lora = lambda c0: pl.BlockSpec((RW_ROWS, LORA), lambda b, h, c: (rb(b, h, c), c0 // LORA))
    vec = pl.BlockSpec((1, w512), lambda b, h, c: (0, h))
    vec128 = pl.BlockSpec((1, LORA), lambda b, h, c: (0, 0))
    up = pl.BlockSpec((LORA, w512), lambda b, h, c: (0, h))
    tri = jnp.tril(jnp.ones((WKV_CHUNK, WKV_CHUNK), BF16))
    gate = pl.BlockSpec((pl.Element(RW_ROWS), pl.Element(w512)),
                        lambda b, h, c: (rb(b, h, c) * RW_ROWS, pl.multiple_of(_C_G + h * w512, LANE)))
    in_specs = [col(_C_R), col(_C_K), col(_C_V), gate, lora(_C_WD), lora(_C_AD),
                vec, vec, vec, vec128, vec128,
                vec, up, vec, up, vec, vec, vec, vec, vec,
                pl.BlockSpec((LANE, LANE), lambda b, h, c: (0, 0)),
                pl.BlockSpec((WKV_CHUNK, WKV_CHUNK), lambda b, h, c: (0, 0)),
                wi_spec]
    out_specs = [pl.BlockSpec((RW_ROWS, w512), lambda b, h, c: (rb(b, h, c), h)),
                 pl.BlockSpec((1, HB, 64, 64), lambda b, h, c: (b, h, 0, 0)),
                 wo_spec]
    return pl.pallas_call(
        _rwkv_prompt_kernel,
        grid=(nb, RWKV_HEADS // HB, nr),
        in_specs=in_specs,
        out_specs=out_specs,
        out_shape=[jax.ShapeDtypeStruct((proj.shape[0], RWKV_WIDTH), BF16),
                   jax.ShapeDtypeStruct((nb, RWKV_HEADS, 64, 64), F32),
                   wo_shape],
        scratch_shapes=[pltpu.VMEM((8, w512), F32)] * 3 + [pltpu.VMEM((8, LORA), F32)] * 2,
        compiler_params=_cp(("parallel", "parallel", "arbitrary")),
        name="rwkv_prompt",
    )(proj, proj, proj, proj, proj, proj,
      rp["mur"], rp["muk"], rp["muv"], rp["muwd"], rp["muad"],
      rp["w0"], rp["wup"], rp["a0"], rp["aup"], rp["kk"], rp["ka"], rp["lnw"], rp["lnb"], rp["rk"],
      rp["seg"], tri, w_ride)


def _trig_kernel(pos_ref, freq_ref, cos_ref, sin_ref):
    ang = pos_ref[...] * freq_ref[...]
    cos_ref[...] = jnp.cos(ang)
    sin_ref[...] = jnp.sin(ang)


def _trig(pos):
    n = pos.shape[0]
    half = RET_QK_DIM // 2
    freq = (ROPE_BASE ** (-jnp.arange(half, dtype=F32) / half)).reshape(1, half)
    posb = jnp.broadcast_to(pos.astype(F32)[:, None], (n, half))
    tn = _pick_tile(n, (256, 128, 8))
    blk = pl.BlockSpec((tn, half), lambda i: (i, 0))
    return pl.pallas_call(
        _trig_kernel, grid=(n // tn,),
        in_specs=[blk, pl.BlockSpec((1, half), lambda i: (0, 0))],
        out_specs=[blk, blk],
        out_shape=[jax.ShapeDtypeStruct((n, half), F32)] * 2,
        name="rope_tables",
    )(posb, freq)


def _rotate(x, cos, sin):
    x1, x2 = x[:, :128], x[:, 128:]
    return jnp.concatenate([x1 * cos - x2 * sin, x1 * sin + x2 * cos], 1)


def _ret_prompt_kernel(lg_ref, q_ref, k_ref, v_ref, g_ref, cos_ref, sin_ref, gnw_ref, wi_ref,
                       y_ref, s_ref, wo_ref):
    L = CHUNK
    wo_ref[...] = wi_ref[...].astype(BF16)
    hg = pl.program_id(1)
    c = pl.program_id(2)

    @pl.when(c == 0)
    def _init():
        s_ref[...] = jnp.zeros(s_ref.shape, F32)

    cos = cos_ref[...]
    sin = sin_ref[...]
    li = lax.broadcasted_iota(jnp.int32, (L, L), 0)
    si = lax.broadcasted_iota(jnp.int32, (L, L), 1)
    rel = (li - si).astype(F32)
    causal = li >= si
    icol = lax.broadcasted_iota(jnp.int32, (L, 1), 0).astype(F32)
    heads = range(RET_HB)
    lg = [lg_ref[hg * RET_HB + j] for j in heads]
    qr = [(_rotate(q_ref[:, j * 256:(j + 1) * 256], cos, sin) * (RET_QK_DIM ** -0.5)).astype(BF16) for j in heads]
    kr = [_rotate(k_ref[:, j * 256:(j + 1) * 256], cos, sin) for j in heads]
    v = [v_ref[:, j * 512:(j + 1) * 512].astype(BF16) for j in heads]
    s0 = [s_ref[0, j] for j in heads]
    qk = [_dot_nt(qr[j], kr[j]) for j in heads]
    y_st = [_dot(qr[j], s0[j]) for j in heads]
    s_in = [_dot_tn(kr[j] * jnp.exp((L - 1.0 - icol) * lg[j]), v[j]) for j in heads]
    sc = [qk[j] * jnp.exp(jnp.where(causal, rel * lg[j], -jnp.inf)) for j in heads]
    y_in = [_dot(sc[j], v[j]) for j in heads]
    outs = []
    for j in heads:
        s_ref[0, j] = s0[j] * jnp.exp(L * lg[j]) + s_in[j]
        y = y_in[j] + y_st[j] * jnp.exp((icol + 1.0) * lg[j])
        mu = jnp.mean(y, -1, keepdims=True)
        d = y - mu
        var = jnp.mean(d * d, -1, keepdims=True)
        outs.append(d * lax.rsqrt(var + RET_GN_EPS))
    o = jnp.concatenate(outs, 1) * gnw_ref[...]
    y_ref[...] = (o * _silu(g_ref[...])).astype(BF16)


def _ret_log_g():
    return jnp.log1p(-jnp.exp2(-5.0 - jnp.arange(RET_HEADS, dtype=F32)))


def _ret_prompt(proj, cos, sin, gn_w, nb, l, w_ride):
    nc = l // CHUNK
    rb = lambda b, h, c, lg: b * nc + c
    wq, wv = RET_HB * RET_QK_DIM, RET_HB * RET_V_DIM
    nhg = RET_HEADS // RET_HB
    wi_spec, wo_spec, wo_shape = _ride_specs(w_ride, nb * nhg * nc, lambda b, h, c, lg: (b * nhg + h) * nc + c)
    grid_spec = pltpu.PrefetchScalarGridSpec(
        num_scalar_prefetch=1,
        grid=(nb, RET_HEADS // RET_HB, nc),
        in_specs=[
            pl.BlockSpec((CHUNK, wq), lambda b, h, c, lg: (rb(b, h, c, lg), h)),
            pl.BlockSpec((CHUNK, wq), lambda b, h, c, lg: (rb(b, h, c, lg), RET_QK_WIDTH // wq + h)),
            pl.BlockSpec((CHUNK, wv), lambda b, h, c, lg: (rb(b, h, c, lg), 2 * RET_QK_WIDTH // wv + h)),
            pl.BlockSpec((CHUNK, wv), lambda b, h, c, lg: (rb(b, h, c, lg), (2 * RET_QK_WIDTH + RET_WIDTH) // wv + h)),
            pl.BlockSpec((CHUNK, 128), lambda b, h, c, lg: (c, 0)),
            pl.BlockSpec((CHUNK, 128), lambda b, h, c, lg: (c, 0)),
            pl.BlockSpec((1, wv), lambda b, h, c, lg: (0, h)),
            wi_spec,
        ],
        out_specs=[pl.BlockSpec((CHUNK, wv), lambda b, h, c, lg: (rb(b, h, c, lg), h)),
                   pl.BlockSpec((1, RET_HB, RET_QK_DIM, RET_V_DIM), lambda b, h, c, lg: (b, h, 0, 0)),
                   wo_spec],
    )
    return pl.pallas_call(
        _ret_prompt_kernel,
        grid_spec=grid_spec,
        out_shape=[jax.ShapeDtypeStruct((proj.shape[0], RET_WIDTH), BF16),
                   jax.ShapeDtypeStruct((nb, RET_HEADS, RET_QK_DIM, RET_V_DIM), F32),
                   wo_shape],
        compiler_params=_cp(("parallel", "parallel", "arbitrary")),
        name="ret_prompt",
    )(_ret_log_g(), proj, proj, proj, proj, cos, sin, gn_w.reshape(1, -1), w_ride)


def _ssd_pre_kernel(xs_ref, b_ref, c_ref, csx_ref, csb_ref, csc_ref, dtc_ref,
                    cwx_ref, cwb_ref, cwc_ref, cbx_ref, cbb_ref, cbc_ref, dtb_ref, al_ref,
                    xa_ref, ba_ref, ca_ref, dt_ref, dec_ref):
    def conv(u_ref, cs_ref, w_ref, bias_ref):
        acc = bias_ref[...] + u_ref[...] * w_ref[SSD_CONV - 1:SSD_CONV, :]
        for k in range(SSD_CONV - 1):
            acc = acc + cs_ref[k] * w_ref[k:k + 1, :]
        return _silu(acc)

    xs = conv(xs_ref, csx_ref, cwx_ref, cbx_ref)
    xa_ref[...] = xs
    ba_ref[0] = conv(b_ref, csb_ref, cwb_ref, cbb_ref)
    ca_ref[0] = conv(c_ref, csc_ref, cwc_ref, cbc_ref)
    dt = _softplus(dtc_ref[0] + dtb_ref[0])
    dt_ref[0] = dt
    dec_ref[0] = jnp.exp(dt * (-jnp.exp(al_ref[0])))


def _ssd_pre(proj, rb0, ns, cs_t, dtc, sp):
    g8 = SSD_GROUPS
    in_specs = [
        pl.BlockSpec((ns, 512), lambda g: (rb0, _C_XS // 512 + g)),
        pl.BlockSpec((ns, 128), lambda g: (rb0, _C_B // 128 + g)),
        pl.BlockSpec((ns, 128), lambda g: (rb0, _C_C // 128 + g)),
        pl.BlockSpec((3, ns, 512), lambda g: (0, 0, g)),
        pl.BlockSpec((3, ns, 128), lambda g: (0, 0, 4096 // 128 + g)),
        pl.BlockSpec((3, ns, 128), lambda g: (0, 0, 5120 // 128 + g)),
        pl.BlockSpec((1, ns, 8), lambda g: (g, rb0, 0)),
        pl.BlockSpec((SSD_CONV, 512), lambda g: (0, g)),
        pl.BlockSpec((SSD_CONV, 128), lambda g: (0, g)),
        pl.BlockSpec((SSD_CONV, 128), lambda g: (0, g)),
        pl.BlockSpec((1, 512), lambda g: (0, g)),
        pl.BlockSpec((1, 128), lambda g: (0, g)),
        pl.BlockSpec((1, 128), lambda g: (0, g)),
        pl.BlockSpec((1, 1, 8), lambda g: (g, 0, 0)),
        pl.BlockSpec((1, 1, 8), lambda g: (g, 0, 0)),
    ]
    out_specs = [
        pl.BlockSpec((ns, 512), lambda g: (0, g)),
        pl.BlockSpec((1, ns, 128), lambda g: (g, 0, 0)),
        pl.BlockSpec((1, ns, 128), lambda g: (g, 0, 0)),
        pl.BlockSpec((1, ns, 8), lambda g: (g, 0, 0)),
        pl.BlockSpec((1, ns, 8), lambda g: (g, 0, 0)),
    ]
    out_shape = [
        jax.ShapeDtypeStruct((ns, SSD_WIDTH), F32),
        jax.ShapeDtypeStruct((g8, ns, SSD_STATE), F32),
        jax.ShapeDtypeStruct((g8, ns, SSD_STATE), F32),
        jax.ShapeDtypeStruct((g8, ns, 8), F32),
        jax.ShapeDtypeStruct((g8, ns, 8), F32),
    ]
    return pl.pallas_call(
        _ssd_pre_kernel, grid=(g8,), in_specs=in_specs, out_specs=out_specs, out_shape=out_shape,
        compiler_params=_cp(("parallel",)), name="ssd_sample_pre",
    )(proj, proj, proj, cs_t, cs_t, cs_t, dtc, sp["cwx"], sp["cwb"], sp["cwc"], sp["cbx"], sp["cbb"], sp["cbc"],
      sp["dtbc"], sp["alc"])


def _outer_rows(x, y):
    hi = lambda t: t.astype(BF16).astype(F32)
    xh, yh = hi(x), hi(y)
    rx = lax.broadcasted_iota(jnp.int32, (8, x.shape[1]), 0)
    ry = lax.broadcasted_iota(jnp.int32, (8, y.shape[1]), 0)
    lhs = jnp.where(rx == 1, x - xh, jnp.where((rx == 0) | (rx == 2), xh, 0.0))
    rhs = jnp.where(ry == 2, y - yh, jnp.where(ry < 2, yh, 0.0))
    return lhs.astype(BF16), rhs.astype(BF16)


def _ssm_state_kernel(dt_ref, dec_ref, s_ref, x_ref, b_ref, c_ref, so_ref, y_ref):
    i = pl.program_id(0)
    hw = SSD_HPG * SSD_HEAD_DIM
    for j in range(SSM_BT):
        b = i * SSM_BT + j
        for g in range(SSD_GROUPS):
            lhs, rhs = _outer_rows(x_ref[j, :, g * hw:(g + 1) * hw], b_ref[j, g:g + 1, :])
            xb = _dot_tn(lhs, rhs)
            new = []
            for r in range(SSD_HPG):
                h = g * SSD_HPG + r
                sn = (s_ref[j, h] * dec_ref[b * SSD_HEADS + h]
                      + xb[r * 64:(r + 1) * 64] * dt_ref[b * SSD_HEADS + h])
                so_ref[j, h] = sn
                new.append(sn)
            crow = jnp.broadcast_to(c_ref[j, g:g + 1, :], (8, SSD_STATE))
            y_ref[j, g:g + 1, :] = _dot_nt(crow, jnp.concatenate(new, 0))[0:1, :]


def _ssm_state(dt, dec, s, xa, ba, ca):
    ns = s.shape[0]
    bt = SSM_BT
    smem = pl.BlockSpec(memory_space=pltpu.SMEM)
    sblk = pl.BlockSpec((bt, SSD_HEADS, SSD_HEAD_DIM, SSD_STATE), lambda i: (i, 0, 0, 0))
    bc = pl.BlockSpec((bt, SSD_GROUPS, SSD_STATE), lambda i: (i, 0, 0))
    return pl.pallas_call(
        _ssm_state_kernel, grid=(ns // bt,),
        in_specs=[smem, smem, sblk, pl.BlockSpec((bt, 1, SSD_WIDTH), lambda i: (i, 0, 0)), bc, bc],
        out_specs=[sblk, pl.BlockSpec((bt, SSD_GROUPS, 512), lambda i: (i, 0, 0))],
        out_shape=[jax.ShapeDtypeStruct(s.shape, F32), jax.ShapeDtypeStruct((ns, SSD_GROUPS, 512), F32)],
        compiler_params=_cp(("parallel",)), name="ssm_sample_state",
    )(dt, dec, s, xa, ba, ca)


def _ssd_post_kernel(y_ref, xa_ref, z_ref, dsk_ref, nw_ref, dst_ref, o_ref):
    del dst_ref
    y = (y_ref[...] + xa_ref[...] * dsk_ref[...]) * _silu(z_ref[...])
    y = y * lax.rsqrt(jnp.mean(y * y, -1, keepdims=True) + RMS_EPS) * nw_ref[...]
    o_ref[...] = y.astype(BF16)


def _ssd_post(y, xa, proj, rb0, sp, dst):
    ns = y.shape[0]
    blk = pl.BlockSpec((ns, 512), lambda g: (0, g))
    vec = pl.BlockSpec((1, 512), lambda g: (0, g))
    return pl.pallas_call(
        _ssd_post_kernel, grid=(SSD_GROUPS,),
        in_specs=[blk, blk, pl.BlockSpec((ns, 512), lambda g: (rb0, _C_Z // 512 + g)), vec, vec,
                  pl.BlockSpec(memory_space=pl.ANY)],
        out_specs=pl.BlockSpec((ns, 512), lambda g: (rb0, g)),
        out_shape=jax.ShapeDtypeStruct(dst.shape, BF16),
        input_output_aliases={5: 0},
        compiler_params=_cp(("parallel",)), name="ssd_sample_post",
    )(y, xa, proj, sp["dsk"], sp["nw"], dst)


def _wkv_pre_kernel(r_ref, k_ref, v_ref, wd_ref, ad_ref, sr_ref, sk_ref, sv_ref, swd_ref, sad_ref,
                    mur_ref, muk_ref, muv_ref, muwd_ref, muad_ref,
                    w0_ref, wup_ref, a0_ref, aup_ref, kk_ref, ka_ref, seg_ref,
                    ro_ref, ko_ref, vo_ref, rt_ref, wt_ref, kt_ref, bt_ref, kkt_ref, vt_ref):
    mix = lambda x_ref, s_ref, mu_ref: x_ref[...] + (s_ref[...] - x_ref[...]) * mu_ref[...]
    rm = mix(r_ref, sr_ref, mur_ref)
    km = mix(k_ref, sk_ref, muk_ref)
    vm = mix(v_ref, sv_ref, muv_ref)
    wdm = mix(wd_ref, swd_ref, muwd_ref)
    adm = mix(ad_ref, sad_ref, muad_ref)
    logw, kk, k2, bv = _rwkv_mix(rm, km, vm, wdm, adm, w0_ref[...], wup_ref[...], a0_ref[...], aup_ref[...],
                                 kk_ref[...], ka_ref[...], seg_ref[...])
    ro_ref[...] = rm
    ko_ref[...] = k2
    vo_ref[...] = vm
    rt_ref[...] = rm.T
    wt_ref[...] = jnp.exp(logw).T
    kt_ref[...] = k2.T
    bt_ref[...] = bv.T
    kkt_ref[...] = kk.T
    vt_ref[...] = vm.T


def _wkv_pre(proj, rb0, shift, rp):
    ns = shift.shape[0]
    w512 = HB * 64
    col = lambda c0: pl.BlockSpec((ns, w512), lambda h: (rb0, c0 // w512 + h))
    lora = lambda c0: pl.BlockSpec((ns, LORA), lambda h: (rb0, c0 // LORA))
    scol = lambda c0: pl.BlockSpec((ns, w512), lambda h: (0, c0 // w512 + h))
    slora = lambda c0: pl.BlockSpec((ns, LORA), lambda h: (0, c0 // LORA))
    vec = pl.BlockSpec((1, w512), lambda h: (0, h))
    vec128 = pl.BlockSpec((1, LORA), lambda h: (0, 0))
    up = pl.BlockSpec((LORA, w512), lambda h: (0, h))
    row = pl.BlockSpec((ns, w512), lambda h: (0, h))
    tr = pl.BlockSpec((w512, ns), lambda h: (h, 0))
    return pl.pallas_call(
        _wkv_pre_kernel, grid=(RWKV_HEADS // HB,),
        in_specs=[col(_C_R), col(_C_K), col(_C_V), lora(_C_WD), lora(_C_AD),
                  scol(0), scol(4096), scol(8192), slora(12288), slora(12416),
                  vec, vec, vec, vec128, vec128, vec, up, vec, up, vec, vec,
                  pl.BlockSpec((LANE, LANE), lambda h: (0, 0))],
        out_specs=[row] * 3 + [tr] * 6,
        out_shape=[jax.ShapeDtypeStruct((ns, RWKV_WIDTH), F32)] * 3
        + [jax.ShapeDtypeStruct((RWKV_WIDTH, ns), F32)] * 6,
        compiler_params=_cp(("parallel",)), name="wkv_sample_pre",
    )(proj, proj, proj, proj, proj, shift, shift, shift, shift, shift,
      rp["mur"], rp["muk"], rp["muv"], rp["muwd"], rp["muad"],
      rp["w0"], rp["wup"], rp["a0"], rp["aup"], rp["kk"], rp["ka"], rp["seg"])


def _wkv_state_kernel(s_ref, r_ref, w_ref, k_ref, b_ref, kk_ref, v_ref, so_ref, y_ref):
    for hh in range(WKV_HPS):
        ch = slice(hh * 64, (hh + 1) * 64)
        r, w, k, bv, kk = r_ref[ch, :], w_ref[ch, :], k_ref[ch, :], b_ref[ch, :], kk_ref[ch, :]

        def vrow(vi, carry):
            s = s_ref[hh, vi]
            sk = jnp.sum(s * kk, axis=0, keepdims=True)
            sn = s * w - sk * bv + v_ref[pl.ds(hh * 64 + vi, 1), :] * k
            so_ref[hh, vi] = sn
            y_ref[pl.ds(hh * 64 + vi, 1), :] = jnp.sum(sn * r, axis=0, keepdims=True)
            return carry

        lax.fori_loop(0, 64, vrow, 0, unroll=WKV_UNROLL)


def _wkv_state(s, r, w, k, bvec, kk, v):
    ns = s.shape[-1]
    hps = WKV_HPS
    sblk = pl.BlockSpec((hps, 64, 64, ns), lambda i: (i, 0, 0, 0))
    ch = pl.BlockSpec((hps * 64, ns), lambda i: (i, 0))
    return pl.pallas_call(
        _wkv_state_kernel, grid=(RWKV_HEADS // hps,),
        in_specs=[sblk, ch, ch, ch, ch, ch, ch],
        out_specs=[sblk, ch],
        out_shape=[jax.ShapeDtypeStruct(s.shape, F32), jax.ShapeDtypeStruct((RWKV_WIDTH, ns), F32)],
        compiler_params=_cp(("parallel",)), name="wkv_sample_state",
    )(s, r, w, k, bvec, kk, v)


def _wkv_post_kernel(o_ref, r_ref, k_ref, v_ref, g_ref, lnw_ref, lnb_ref, rk_ref, seg_ref, dst_ref, y_ref):
    del dst_ref
    y_ref[...] = _rwkv_out(o_ref[...].T, r_ref[...], k_ref[...], v_ref[...], g_ref[...],
                           lnw_ref[...], lnb_ref[...], rk_ref[...], seg_ref[...])


def _wkv_post(o_t, r, k2, v, proj, rb0, rp, dst):
    ns = o_t.shape[1]
    w512 = HB * 64
    row = pl.BlockSpec((ns, w512), lambda h: (0, h))
    vec = pl.BlockSpec((1, w512), lambda h: (0, h))
    return pl.pallas_call(
        _wkv_post_kernel, grid=(RWKV_HEADS // HB,),
        in_specs=[pl.BlockSpec((w512, ns), lambda h: (h, 0)), row, row, row,
                  pl.BlockSpec((pl.Element(ns), pl.Element(w512)), lambda h: (rb0 * ns, pl.multiple_of(_C_G + h * w512, LANE))),
                  vec, vec, vec, pl.BlockSpec((LANE, LANE), lambda h: (0, 0)),
                  pl.BlockSpec(memory_space=pl.ANY)],
        out_specs=pl.BlockSpec((ns, w512), lambda h: (rb0, h)),
        out_shape=jax.ShapeDtypeStruct(dst.shape, BF16),
        input_output_aliases={9: 0},
        compiler_params=_cp(("parallel",)), name="wkv_sample_post",
    )(o_t, r, k2, v, proj, rp["lnw"], rp["lnb"], rp["rk"], rp["seg"], dst)


def _ret_pre_kernel(q_ref, k_ref, cos_ref, sin_ref, qo_ref, kh_ref, kl_ref):
    cos = cos_ref[0:1, :]
    sin = sin_ref[0:1, :]
    qo_ref[...] = _rotate(q_ref[...], cos, sin) * (RET_QK_DIM ** -0.5)
    hi, lo = _split(_rotate(k_ref[...], cos, sin).T, 2)
    kh_ref[...] = hi
    kl_ref[...] = lo


def _ret_pre(proj, rb0, ns, cos, sin):
    return pl.pallas_call(
        _ret_pre_kernel, grid=(RET_HEADS,),
        in_specs=[pl.BlockSpec((ns, RET_QK_DIM), lambda h: (rb0, h)),
                  pl.BlockSpec((ns, RET_QK_DIM), lambda h: (rb0, RET_HEADS + h)),
                  pl.BlockSpec((8, 128), lambda h: (0, 0)), pl.BlockSpec((8, 128), lambda h: (0, 0))],
        out_specs=[pl.BlockSpec((ns, RET_QK_DIM), lambda h: (0, h)),
                   pl.BlockSpec((RET_QK_DIM, ns), lambda h: (h, 0)),
                   pl.BlockSpec((RET_QK_DIM, ns), lambda h: (h, 0))],
        out_shape=[jax.ShapeDtypeStruct((ns, RET_QK_WIDTH), F32),
                   jax.ShapeDtypeStruct((RET_QK_WIDTH, ns), BF16),
                   jax.ShapeDtypeStruct((RET_QK_WIDTH, ns), BF16)],
        compiler_params=_cp(("parallel",)), name="ret_sample_pre",
    )(proj, proj, cos, sin)


def _ret_state_kernel(gd_ref, s_ref, kh_ref, kl_ref, q_ref, v_ref, so_ref, y_ref, kb, ysc):
    b = pl.program_id(0)
    e = _onehot_cols(b, LANE)
    kb[...] = (jnp.dot(kh_ref[...], e, preferred_element_type=F32)
               + jnp.dot(kl_ref[...], e, preferred_element_type=F32))

    def head(h, carry):
        gd = gd_ref[h]
        vrow = v_ref[0, pl.ds(h, 1), :]
        for rc in range(RET_QK_DIM // 64):
            kcol = kb[pl.ds(pl.multiple_of(h * RET_QK_DIM + rc * 64, 64), 64), :]
            for lc in range(RET_V_DIM // LANE):
                rs, ls = slice(rc * 64, (rc + 1) * 64), slice(lc * LANE, (lc + 1) * LANE)
                so_ref[0, h, rs, ls] = s_ref[0, h, rs, ls] * gd + kcol * vrow[:, ls]
        qrow = q_ref[0, pl.ds(h, 1), :]
        y8 = _dot(jnp.broadcast_to(qrow, (8, RET_QK_DIM)), so_ref[0, h])
        ysc[pl.ds(h, 1), :] = y8[0:1, :]
        return carry

    lax.fori_loop(0, RET_HEADS, head, 0)
    y_ref[0] = ysc[...]


def _ret_state(s, kh, kl, q, v):
    ns = s.shape[0]
    gd = jnp.exp(_ret_log_g())
    sblk = pl.BlockSpec((1, RET_HEADS, RET_QK_DIM, RET_V_DIM), lambda i, g: (i, 0, 0, 0))
    full = pl.BlockSpec((RET_QK_WIDTH, ns), lambda i, g: (0, 0))
    grid_spec = pltpu.PrefetchScalarGridSpec(
        num_scalar_prefetch=1, grid=(ns,),
        in_specs=[sblk, full, full,
                  pl.BlockSpec((1, RET_HEADS, RET_QK_DIM), lambda i, g: (i, 0, 0)),
                  pl.BlockSpec((1, RET_HEADS, RET_V_DIM), lambda i, g: (i, 0, 0))],
        out_specs=[sblk, pl.BlockSpec((1, RET_HEADS, RET_V_DIM), lambda i, g: (i, 0, 0))],
        scratch_shapes=[pltpu.VMEM((RET_QK_WIDTH, LANE), F32), pltpu.VMEM((RET_HEADS, RET_V_DIM), F32)],
    )
    return pl.pallas_call(
        _ret_state_kernel, grid_spec=grid_spec,
        out_shape=[jax.ShapeDtypeStruct(s.shape, F32), jax.ShapeDtypeStruct((ns, RET_HEADS, RET_V_DIM), F32)],
        compiler_params=_cp(("parallel",)), name="ret_sample_state",
    )(gd, s, kh, kl, q, v)


def _ret_post_kernel(y_ref, g_ref, gnw_ref, dst_ref, o_ref):
    del dst_ref
    y = y_ref[...]
    mu = jnp.mean(y, -1, keepdims=True)
    d = y - mu
    var = jnp.mean(d * d, -1, keepdims=True)
    o = d * lax.rsqrt(var + RET_GN_EPS) * gnw_ref[...]
    o_ref[...] = (o * _silu(g_ref[...])).astype(BF16)


def _ret_post(y, proj, rb0, gn_w, dst):
    ns = y.shape[0]
    blk = pl.BlockSpec((ns, RET_V_DIM), lambda h: (0, h))
    return pl.pallas_call(
        _ret_post_kernel, grid=(RET_HEADS,),
        in_specs=[blk, pl.BlockSpec((ns, RET_V_DIM), lambda h: (rb0, (2 * RET_QK_WIDTH + RET_WIDTH) // RET_V_DIM + h)),
                  pl.BlockSpec((1, RET_V_DIM), lambda h: (0, h)), pl.BlockSpec(memory_space=pl.ANY)],
        out_specs=pl.BlockSpec((ns, RET_V_DIM), lambda h: (rb0, h)),
        out_shape=jax.ShapeDtypeStruct(dst.shape, BF16),
        input_output_aliases={3: 0},
        compiler_params=_cp(("parallel",)), name="ret_sample_post",
    )(y, proj, gn_w.reshape(1, -1), dst)


def _ab_layer(xp, xs, xb, nb, l, ns, conv_s, ssm_s, shift_s, wkv_s, w_in, sp, rp, w_out, ln_w, ln_b, w_next):
    mp = nb * l
    rb0 = mp // 128
    wt = w_in.T.astype(BF16)
    n_lo = AB_DT0 // AB_TN
    starts = [j * AB_TN for j in range(n_lo)] + [AB_DT0 + SSD_HEADS + j * AB_TN
                                                 for j in range((AB_MAIN - AB_DT0) // AB_TN)]
    proj = _matmul_wt(xb, wt, starts, AB_TN, "ab_in_proj")
    pdt = _matmul_wt(xb, wt, [AB_DT0], LANE, "ab_dt_proj")[:, :SSD_HEADS]
    m = proj.shape[0]
    dt3 = pdt.reshape(m, SSD_GROUPS, SSD_HPG)
    dtc = dt3.transpose(1, 0, 2)
    dtr = dt3.transpose(1, 2, 0)

    ya, ssm_p, w_out_b = _ssd_prompt(proj, dtc, dtr, nb, l, sp, w_out)
    yb, wkv_p, w_next_b = _rwkv_prompt(proj, nb, l, rp, w_next)
    tail = lambda n, c0, c1: jnp.stack([proj[(b + 1) * l - n:(b + 1) * l, c0:c1] for b in range(nb)])
    conv_p = tail(SSD_CONV - 1, _C_XS, _C_R)
    shift_p = tail(1, _C_R, _C_G)

    xa, ba, ca, dt_s, dec_s = _ssd_pre(proj, rb0, ns, conv_s.transpose(1, 0, 2), dtc, sp)
    flat = lambda t: t.transpose(1, 0, 2).reshape(ns * SSD_HEADS)
    ssm_n, y_s = _ssm_state(flat(dt_s), flat(dec_s), ssm_s, xa.reshape(ns, 1, SSD_WIDTH),
                            ba.transpose(1, 0, 2), ca.transpose(1, 0, 2))
    ya = _ssd_post(y_s.reshape(ns, SSD_WIDTH), xa, proj, rb0, sp, ya)
    conv_n = jnp.concatenate([conv_s[:, 1:], proj[mp:, None, _C_XS:_C_R]], axis=1)

    r_s, k_s, v_s, r_t, w_t, k_t, b_t, kk_t, v_t = _wkv_pre(proj, rb0, shift_s.reshape(ns, SHIFT_DIM), rp)
    wkv_t, o_t = _wkv_state(wkv_s.transpose(1, 2, 3, 0), r_t, w_t, k_t, b_t, kk_t, v_t)
    wkv_n = wkv_t.transpose(3, 0, 1, 2)
    yb = _wkv_post(o_t, r_s, k_s, v_s, proj, rb0, rp, yb)
    shift_n = proj[mp:, None, _C_R:_C_G]

    out = _matmul2(ya, yb, w_out_b, "ab_out_proj")
    x_new, xb_new = _deepnorm_first(xp, xs, out, ln_w, ln_b, "ab_deepnorm")
    return x_new, xb_new, (conv_p, ssm_p, shift_p, wkv_p), (conv_n, ssm_n, shift_n, wkv_n), w_next_b


def _ret_layer(x, xb, nb, l, ns, ret_s, w_in, gn_w, w_out, ln_w, ln_b):
    mp = nb * l
    rb0 = mp // 128
    proj = _matmul(xb, w_in, "ret_in_proj")
    cos, sin = _trig(jnp.arange(l))
    cos_s, sin_s = _trig(jnp.full((8,), PAST_LEN))
    y, ret_p, w_out_b = _ret_prompt(proj, cos, sin, gn_w, nb, l, w_out)

    q_s, kh, kl = _ret_pre(proj, rb0, ns, cos_s, sin_s)
    v_s = proj[mp:, 2 * RET_QK_WIDTH:2 * RET_QK_WIDTH + RET_WIDTH].reshape(ns, RET_HEADS, RET_V_DIM)
    ret_n, o_s = _ret_state(ret_s, kh, kl, q_s.reshape(ns, RET_HEADS, RET_QK_DIM), v_s)
    y = _ret_post(o_s.reshape(ns, RET_WIDTH), proj, rb0, gn_w, y)

    out = _matmul(y, w_out_b, "ret_out_proj")
    y_p, y_s = _deepnorm_last(x, out, ln_w, ln_b, ns, "ret_deepnorm")
    return y_p, y_s, ret_p, ret_n


def kernel(x_prompt, x_sample, state_conv, state_ssm, state_shift, state_wkv, state_ret, ab_w_in, ssd_conv_w, ssd_conv_b, ssd_dt_bias, ssd_a_log, ssd_d, ssd_norm_w, rwkv_mu, rwkv_w0, rwkv_w_up, rwkv_a0, rwkv_a_up, rwkv_k_k, rwkv_k_a, rwkv_r_k, rwkv_lnx_w, rwkv_lnx_b, ab_w_out, ab_ln_w, ab_ln_b, ret_w_in, ret_gn_w, ret_w_out, ret_ln_w, ret_ln_b):
    nb, l, d = x_prompt.shape
    ns = x_sample.shape[0]
    assert x_sample.shape[1] == 1 and l % CHUNK == 0 and ns % LANE == 0 and ns == LANE
    mp = nb * l
    xp, xs = x_prompt.reshape(mp, d), x_sample.reshape(ns, d)
    xb = jnp.concatenate([xp.astype(BF16), xs.astype(BF16)], axis=0)

    sp = _ssd_params(ssd_conv_w[0], ssd_conv_b[0], ssd_dt_bias[0], ssd_a_log[0], ssd_d[0], ssd_norm_w[0])
    rp = _rwkv_params(rwkv_mu[0], rwkv_w0[0], rwkv_w_up[0], rwkv_a0[0], rwkv_a_up[0], rwkv_k_k[0], rwkv_k_a[0],
                      rwkv_r_k[0], rwkv_lnx_w[0], rwkv_lnx_b[0])
    x, xb, pst, sst, ret_w_in_b = _ab_layer(xp, xs, xb, nb, l, ns, state_conv[0], state_ssm[0], state_shift[0],
                                            state_wkv[0], ab_w_in[0], sp, rp, ab_w_out[0], ab_ln_w[0], ab_ln_b[0],
                                            ret_w_in[0])
    y_p, y_s, ret_p, ret_n = _ret_layer(x, xb, nb, l, ns, state_ret[0], ret_w_in_b, ret_gn_w[0], ret_w_out[0],
                                        ret_ln_w[0], ret_ln_b[0])
    y_prompt = y_p.reshape(nb, l, d)
    y_sample = y_s.reshape(ns, 1, d)
    st = lambda t: t[None]
    return (y_prompt, y_sample,
            st(pst[0]), st(pst[1]), st(pst[2]), st(pst[3]), st(ret_p),
            st(sst[0]), st(sst[1]), st(sst[2]), st(sst[3]), st(ret_n))
```

```python
import functools
import math

import jax
import jax.numpy as jnp
import numpy as np
from jax import lax
from jax.experimental import pallas as pl
from jax.experimental.pallas import tpu as pltpu

F32 = jnp.float32
BF16 = jnp.bfloat16

D_MODEL = 4096
DEPTH = 2
PAST_LEN = 16384
SSD_WIDTH = 4096
SSD_HEAD_DIM = 64
SSD_HEADS = 64
SSD_GROUPS = 8
SSD_HPG = 8
SSD_STATE = 128
SSD_CONV = 4
SSD_CONV_DIM = SSD_WIDTH + 2 * SSD_GROUPS * SSD_STATE
RWKV_WIDTH = 4096
RWKV_HEAD_DIM = 64
RWKV_HEADS = 64
LORA = 128
SHIFT_DIM = 3 * RWKV_WIDTH + 2 * LORA
RET_HEADS = 16
RET_QK_DIM = 256
RET_V_DIM = 512
RET_QK_WIDTH = 4096
RET_WIDTH = 8192
ROPE_BASE = 10000.0
CHUNK = 128
ALPHA = (2 * DEPTH) ** 0.25
LN_EPS = 1e-5
RMS_EPS = 1e-5
RWKV_GN_EPS = 64e-5
RET_GN_EPS = 1e-6

LANE = 128
VMEM_LIMIT = 56 * 1024 * 1024
WKV_CHUNK = 64
RW_ROWS = 256
HB = 8
RET_HB = 4
SSM_BT = 2
SSD_CPS = 2
WKV_HPS = 2
WKV_UNROLL = 8

_C_Z, _C_XS, _C_B, _C_C = 0, 4096, 8192, 9216
_C_R, _C_K, _C_V, _C_WD, _C_AD, _C_G = 10240, 14336, 18432, 22528, 22656, 22784
AB_MAIN = 26880
AB_DT0 = 10240
AB_TN = 1280


def _cp(sem):
    return pltpu.CompilerParams(dimension_semantics=sem, vmem_limit_bytes=VMEM_LIMIT)


def _silu(x):
    return x * jax.nn.sigmoid(x)


def _softplus(x):
    return jnp.maximum(x, 0.0) + jnp.log1p(jnp.exp(-jnp.abs(x)))


def _dot(a, b):
    return jnp.dot(a.astype(BF16), b.astype(BF16), preferred_element_type=F32)


def _dot_nt(a, b):
    return lax.dot_general(a.astype(BF16), b.astype(BF16), (((1,), (1,)), ((), ())),
                           preferred_element_type=F32)


def _dot_tn(a, b):
    return lax.dot_general(a.astype(BF16), b.astype(BF16), (((0,), (0,)), ((), ())),
                           preferred_element_type=F32)


def _split(x, n):
    parts, r = [], x
    for _ in range(n):
        h = r.astype(BF16)
        parts.append(h)
        r = r - h.astype(F32)
    return parts


def _dot01(m01, x, n=3):
    return sum(jnp.dot(m01, p, preferred_element_type=F32) for p in _split(x, n))


def _dot01_r(x, m01, n=2):
    return sum(jnp.dot(p, m01, preferred_element_type=F32) for p in _split(x, n))


def _segsum(x, seg):
    r, w = x.shape
    nt = w // LANE
    tall = jnp.concatenate([x[:, i * LANE:(i + 1) * LANE] for i in range(nt)], axis=0)
    s = _dot01_r(tall, seg)
    return jnp.concatenate([s[i * r:(i + 1) * r] for i in range(nt)], axis=1)


def _onehot_cols(b, n):
    rows = lax.broadcasted_iota(jnp.int32, (LANE, n), 0)
    return jnp.where(rows == b, 1.0, 0.0).astype(BF16)


def _mm_kernel(x_ref, w_ref, o_ref):
    o_ref[...] = jnp.dot(x_ref[...], w_ref[...], preferred_element_type=F32)


def _pick_tile(n, prefs):
    for t in prefs:
        if n % t == 0:
            return t
    return n


def _matmul(x, w, name):
    m, k = x.shape
    n = w.shape[1]
    tm = _pick_tile(m, (640, 512, 256, 128))
    tn = _pick_tile(n, (1280, 1024, 512, 256, 128) if k <= 4096 else (512, 256, 128))
    return pl.pallas_call(
        _mm_kernel,
        grid=(n // tn, m // tm),
        in_specs=[pl.BlockSpec((tm, k), lambda j, i: (i, 0)),
                  pl.BlockSpec((k, tn), lambda j, i: (0, j))],
        out_specs=pl.BlockSpec((tm, tn), lambda j, i: (i, j)),
        out_shape=jax.ShapeDtypeStruct((m, n), F32),
        compiler_params=_cp(("parallel", "parallel")),
        name=name,
    )(x, w)


def _mm_wt_kernel(st_ref, x_ref, wt_ref, o_ref):
    del st_ref
    o_ref[...] = lax.dot_general(x_ref[...], wt_ref[...], (((1,), (1,)), ((), ())), preferred_element_type=F32)


def _matmul_wt(x, wt, row_starts, tn, name):
    m, k = x.shape
    nt = len(row_starts)
    tm = _pick_tile(m, (640, 512, 256, 128))
    starts = jnp.asarray(row_starts, jnp.int32)
    grid_spec = pltpu.PrefetchScalarGridSpec(
        num_scalar_prefetch=1,
        grid=(nt, m // tm),
        in_specs=[pl.BlockSpec((tm, k), lambda j, i, st: (i, 0)),
                  pl.BlockSpec((pl.Element(tn), pl.Element(k)), lambda j, i, st: (pl.multiple_of(st[j], 64), 0))],
        out_specs=pl.BlockSpec((tm, tn), lambda j, i, st: (i, j)),
    )
    return pl.pallas_call(
        _mm_wt_kernel, grid_spec=grid_spec,
        out_shape=jax.ShapeDtypeStruct((m, nt * tn), F32),
        compiler_params=_cp(("parallel", "parallel")),
        name=name,
    )(starts, x, wt)


def _mm2_kernel(a_ref, b_ref, w_ref, o_ref):
    ka = a_ref.shape[1]
    o_ref[...] = (jnp.dot(a_ref[...], w_ref[0:ka, :], preferred_element_type=F32)
                  + jnp.dot(b_ref[...], w_ref[ka:, :], preferred_element_type=F32))


def _matmul2(a, b, w, name):
    m, ka = a.shape
    kb = b.shape[1]
    n = w.shape[1]
    tm = _pick_tile(m, (640, 512, 256, 128))
    tn = _pick_tile(n, (512, 256, 128))
    return pl.pallas_call(
        _mm2_kernel,
        grid=(n // tn, m // tm),
        in_specs=[pl.BlockSpec((tm, ka), lambda j, i: (i, 0)),
                  pl.BlockSpec((tm, kb), lambda j, i: (i, 0)),
                  pl.BlockSpec((ka + kb, tn), lambda j, i: (0, j))],
        out_specs=pl.BlockSpec((tm, tn), lambda j, i: (i, j)),
        out_shape=jax.ShapeDtypeStruct((m, n), F32),
        compiler_params=_cp(("parallel", "parallel")),
        name=name,
    )(a, b, w)


def _post_norm(x, o, w, b):
    h = ALPHA * x + o
    mu = jnp.mean(h, -1, keepdims=True)
    d = h - mu
    var = jnp.mean(d * d, -1, keepdims=True)
    return d * lax.rsqrt(var + LN_EPS) * w + b


def _ln_first_kernel(xp_ref, xs_ref, o_ref, w_ref, b_ref, y_ref, yb_ref):
    is_sample = pl.program_id(0) == pl.num_programs(0) - 1
    x = jnp.where(is_sample, xs_ref[...], xp_ref[...])
    y = _post_norm(x, o_ref[...], w_ref[...], b_ref[...])
    y_ref[...] = y
    yb_ref[...] = y.astype(BF16)


def _deepnorm_first(xp, xs, o, w, b, name):
    mp, d = xp.shape
    ns = xs.shape[0]
    npt = mp // ns
    row = pl.BlockSpec((ns, d), lambda i: (i, 0))
    vec = pl.BlockSpec((1, d), lambda i: (0, 0))
    return pl.pallas_call(
        _ln_first_kernel,
        grid=(npt + 1,),
        in_specs=[pl.BlockSpec((ns, d), lambda i: (jnp.minimum(i, npt - 1), 0)),
                  pl.BlockSpec((ns, d), lambda i: (0, 0)), row, vec, vec],
        out_specs=[row, row],
        out_shape=[jax.ShapeDtypeStruct((mp + ns, d), F32), jax.ShapeDtypeStruct((mp + ns, d), BF16)],
        compiler_params=_cp(("parallel",)),
        name=name,
    )(xp, xs, o, w.reshape(1, d), b.reshape(1, d))


def _ln_last_kernel(x_ref, o_ref, w_ref, b_ref, yp_ref, ys_ref):
    is_sample = pl.program_id(0) == pl.num_programs(0) - 1
    y = _post_norm(x_ref[...], o_ref[...], w_ref[...], b_ref[...])

    @pl.when(jnp.logical_not(is_sample))
    def _prompt():
        yp_ref[...] = y

    @pl.when(is_sample)
    def _sample():
        ys_ref[...] = y


def _deepnorm_last(x, o, w, b, ns, name):
    m, d = x.shape
    npt = m // ns - 1
    row = pl.BlockSpec((ns, d), lambda i: (i, 0))
    vec = pl.BlockSpec((1, d), lambda i: (0, 0))
    return pl.pallas_call(
        _ln_last_kernel,
        grid=(npt + 1,),
        in_specs=[row, row, vec, vec],
        out_specs=[pl.BlockSpec((ns, d), lambda i: (jnp.minimum(i, npt - 1), 0)),
                   pl.BlockSpec((ns, d), lambda i: (0, 0))],
        out_shape=[jax.ShapeDtypeStruct((npt * ns, d), F32), jax.ShapeDtypeStruct((ns, d), F32)],
        compiler_params=_cp(("arbitrary",)),
        name=name,
    )(x, o, w.reshape(1, d), b.reshape(1, d))


def _ssd_prompt_kernel(z_ref, xs_ref, b_ref, c_ref, dtc_ref, dtr_ref,
                       cwx_ref, cwb_ref, cwc_ref, cbx_ref, cbb_ref, cbc_ref,
                       dtbc_ref, dtbr_ref, alc_ref, alr_ref, dsk_ref, nw_ref, tri_ref, rep64_ref, rep128_ref,
                       wi_ref, y_ref, s_ref, wo_ref, bufx, bufb, bufc):
    L = CHUNK
    c = pl.program_id(2)
    wo_ref[...] = wi_ref[...].astype(BF16)

    @pl.when(c == 0)
    def _init():
        for buf in (bufx, bufb, bufc):
            buf[0:8, :] = jnp.zeros((8, buf.shape[1]), F32)
        s_ref[...] = jnp.zeros(s_ref.shape, F32)

    tri = tri_ref[...]
    li = lax.broadcasted_iota(jnp.int32, (L, L), 0)
    si = lax.broadcasted_iota(jnp.int32, (L, L), 1)
    causal = li >= si
    lo = lax.broadcasted_iota(jnp.int32, (1, LANE), 1) < 64
    pairs = range(SSD_HPG // 2)
    tile = lambda a, p: a[:, p * LANE:(p + 1) * LANE]
    for ci in range(SSD_CPS):
        rows = slice(ci * L, (ci + 1) * L)

        def conv(u_ref, buf, w_ref, bias_ref):
            buf[8:8 + L, :] = u_ref[rows, :]
            acc = bias_ref[...] + buf[5:5 + L, :] * w_ref[0:1, :]
            for k in range(1, SSD_CONV):
                acc = acc + buf[5 + k:5 + k + L, :] * w_ref[k:k + 1, :]
            buf[0:8, :] = buf[L:L + 8, :]
            return _silu(acc)

        xs = conv(xs_ref, bufx, cwx_ref, cbx_ref)
        bm = conv(b_ref, bufb, cwb_ref, cbb_ref)
        cm = conv(c_ref, bufc, cwc_ref, cbc_ref)
        dtc = _softplus(dtc_ref[0, rows, :] + dtbc_ref[0])
        dtr = _softplus(dtr_ref[0, :, rows] + dtbr_ref[0])
        adt_c = dtc * (-jnp.exp(alc_ref[0]))
        adt_r = dtr * (-jnp.exp(alr_ref[0]))
        cum_c = _dot01(tri, adt_c)
        cum_r = sum(lax.dot_general(p, tri, (((1,), (1,)), ((), ())), preferred_element_type=F32)
                    for p in _split(adt_r, 3))
        cb = _dot_nt(cm, bm)
        dt_x = _dot01_r(dtc, rep64_ref[...], 3)
        cum_x = _dot01_r(cum_c, rep64_ref[...], 3)
        cum_b = _dot01_r(cum_c, rep128_ref[...], 3)
        xdt = xs * dt_x
        xdt_tail = xdt * jnp.exp(cum_x[L - 1:L, :] - cum_x)
        s_old = [s_ref[0, r] for r in range(SSD_HPG)]
        decay = [jnp.exp(jnp.where(causal, tile(cum_b, r) - cum_r[r:r + 1, :], -jnp.inf))
                 for r in range(SSD_HPG)]
        x_lo = [jnp.where(lo, tile(xdt, p), 0.0) for p in pairs]
        x_hi = [jnp.where(lo, 0.0, tile(xdt, p)) for p in pairs]
        y_in = [_dot(cb * decay[2 * p], x_lo[p]) + _dot(cb * decay[2 * p + 1], x_hi[p]) for p in pairs]
        y_st = [_dot_nt(cm, jnp.concatenate([s_old[2 * p], s_old[2 * p + 1]], 0)) for p in pairs]
        s_in = [_dot_tn(tile(xdt_tail, p), bm) for p in pairs]
        for r in range(SSD_HPG):
            half = s_in[r // 2][(r % 2) * 64:(r % 2 + 1) * 64]
            s_ref[0, r] = s_old[r] * jnp.exp(cum_c[L - 1:L, r:r + 1]) + half
        y = jnp.concatenate(y_in, axis=1) + jnp.concatenate(y_st, axis=1) * jnp.exp(cum_x) + xs * dsk_ref[...]
        y = y * _silu(z_ref[rows, :])
        y = y * lax.rsqrt(jnp.mean(y * y, -1, keepdims=True) + RMS_EPS) * nw_ref[...]
        y_ref[rows, :] = y.astype(BF16)


def _ssd_params(conv_w, conv_b, dt_bias, a_log, d_skip, norm_w):
    g = SSD_GROUPS
    return dict(
        cwx=conv_w[:, :4096], cwb=conv_w[:, 4096:5120], cwc=conv_w[:, 5120:],
        cbx=conv_b[:4096].reshape(1, -1), cbb=conv_b[4096:5120].reshape(1, -1), cbc=conv_b[5120:].reshape(1, -1),
        dtbc=dt_bias.reshape(g, 1, 8), dtbr=dt_bias.reshape(g, 8, 1),
        alc=a_log.reshape(g, 1, 8), alr=a_log.reshape(g, 8, 1),
        dsk=jnp.repeat(d_skip, SSD_HEAD_DIM).reshape(1, -1), nw=norm_w.reshape(1, -1))


def _ride_specs(w, nsteps, lin):
    r, n = w.shape
    assert r % nsteps == 0 and (r // nsteps) % 16 == 0
    blk = pl.BlockSpec((r // nsteps, n), lambda *idx: (lin(*idx), 0))
    return blk, blk, jax.ShapeDtypeStruct((r, n), BF16)


def _ssd_prompt(proj, dtc, dtr, nb, l, sp, w_ride):
    rows = CHUNK * SSD_CPS
    nc = l // rows
    rb = lambda b, g, c: b * nc + c
    tri = jnp.tril(jnp.ones((CHUNK, CHUNK), BF16))
    wi_spec, wo_spec, wo_shape = _ride_specs(w_ride, nb * SSD_GROUPS * nc,
                                             lambda b, g, c: (b * SSD_GROUPS + g) * nc + c)
    in_specs = [
        pl.BlockSpec((rows, 512), lambda b, g, c: (rb(b, g, c), _C_Z // 512 + g)),
        pl.BlockSpec((rows, 512), lambda b, g, c: (rb(b, g, c), _C_XS // 512 + g)),
        pl.BlockSpec((rows, 128), lambda b, g, c: (rb(b, g, c), _C_B // 128 + g)),
        pl.BlockSpec((rows, 128), lambda b, g, c: (rb(b, g, c), _C_C // 128 + g)),
        pl.BlockSpec((1, rows, 8), lambda b, g, c: (g, rb(b, g, c), 0)),
        pl.BlockSpec((1, 8, rows), lambda b, g, c: (g, 0, rb(b, g, c))),
        pl.BlockSpec((SSD_CONV, 512), lambda b, g, c: (0, g)),
        pl.BlockSpec((SSD_CONV, 128), lambda b, g, c: (0, g)),
        pl.BlockSpec((SSD_CONV, 128), lambda b, g, c: (0, g)),
        pl.BlockSpec((1, 512), lambda b, g, c: (0, g)),
        pl.BlockSpec((1, 128), lambda b, g, c: (0, g)),
        pl.BlockSpec((1, 128), lambda b, g, c: (0, g)),
        pl.BlockSpec((1, 1, 8), lambda b, g, c: (g, 0, 0)),
        pl.BlockSpec((1, 8, 1), lambda b, g, c: (g, 0, 0)),
        pl.BlockSpec((1, 1, 8), lambda b, g, c: (g, 0, 0)),
        pl.BlockSpec((1, 8, 1), lambda b, g, c: (g, 0, 0)),
        pl.BlockSpec((1, 512), lambda b, g, c: (0, g)),
        pl.BlockSpec((1, 512), lambda b, g, c: (0, g)),
        pl.BlockSpec((CHUNK, CHUNK), lambda b, g, c: (0, 0)),
        pl.BlockSpec((SSD_HPG, SSD_HPG * 64), lambda b, g, c: (0, 0)),
        pl.BlockSpec((SSD_HPG, SSD_HPG * LANE), lambda b, g, c: (0, 0)),
        wi_spec,
    ]
    rep64 = jnp.asarray(np.kron(np.eye(SSD_HPG), np.ones((1, 64))), BF16)
    rep128 = jnp.asarray(np.kron(np.eye(SSD_HPG), np.ones((1, LANE))), BF16)
    out_specs = [pl.BlockSpec((rows, 512), lambda b, g, c: (rb(b, g, c), g)),
                 pl.BlockSpec((1, SSD_HPG, SSD_HEAD_DIM, SSD_STATE), lambda b, g, c: (b, g, 0, 0)),
                 wo_spec]
    return pl.pallas_call(
        _ssd_prompt_kernel,
        grid=(nb, SSD_GROUPS, nc),
        in_specs=in_specs,
        out_specs=out_specs,
        out_shape=[jax.ShapeDtypeStruct((proj.shape[0], SSD_WIDTH), BF16),
                   jax.ShapeDtypeStruct((nb, SSD_HEADS, SSD_HEAD_DIM, SSD_STATE), F32),
                   wo_shape],
        scratch_shapes=[pltpu.VMEM((CHUNK + 8, 512), F32), pltpu.VMEM((CHUNK + 8, 128), F32),
                        pltpu.VMEM((CHUNK + 8, 128), F32)],
        compiler_params=_cp(("parallel", "parallel", "arbitrary")),
        name="ssd_prompt",
    )(proj, proj, proj, proj, dtc, dtr, sp["cwx"], sp["cwb"], sp["cwc"], sp["cbx"], sp["cbb"], sp["cbc"],
      sp["dtbc"], sp["dtbr"], sp["alc"], sp["alr"], sp["dsk"], sp["nw"], tri, rep64, rep128, w_ride)


def _rwkv_mix(rm, km, vm, wdm, adm, w0, wup, a0, aup, k_k, k_a, seg):
    wlog = -_softplus(-(w0 + _dot(jnp.tanh(wdm), wup))) - 0.5
    logw = -jnp.exp(wlog)
    aa = jax.nn.sigmoid(a0 + _dot(adm, aup))
    kkr = km * k_k
    kk = kkr * lax.rsqrt(jnp.maximum(_segsum(kkr * kkr, seg), 1e-24))
    k2 = km * (1.0 + (aa - 1.0) * k_a)
    return logw, kk, k2, kk * aa


def _rwkv_out(o, rm, k2, vm, g, lnw, lnb, rk, seg):
    inv = 1.0 / RWKV_HEAD_DIM
    mean = _segsum(o, seg) * inv
    d = o - mean
    var = _segsum(d * d, seg) * inv
    on = d * lax.rsqrt(var + RWKV_GN_EPS) * lnw + lnb
    bonus = _segsum(rm * k2 * rk, seg) * vm
    return ((on + bonus) * _silu(g)).astype(BF16)


def _rwkv_prompt_kernel(r_ref, k_ref, v_ref, g_ref, wd_ref, ad_ref,
                        mur_ref, muk_ref, muv_ref, muwd_ref, muad_ref,
                        w0_ref, wup_ref, a0_ref, aup_ref, kk_ref, ka_ref, lnw_ref, lnb_ref, rk_ref,
                        seg_ref, tri_ref, wi_ref,
                        y_ref, s_ref, wo_ref, cr, ck, cv, cwd, cad):
    R, C = RW_ROWS, WKV_CHUNK
    wo_ref[...] = wi_ref[...].astype(BF16)
    c = pl.program_id(2)

    @pl.when(c == 0)
    def _init():
        for buf in (cr, ck, cv, cwd, cad):
            buf[...] = jnp.zeros(buf.shape, F32)
        s_ref[...] = jnp.zeros(s_ref.shape, F32)

    row0 = lax.broadcasted_iota(jnp.int32, (R, 1), 0) == 0

    def shift(x_ref, carry, mu_ref):
        x = x_ref[...]
        prev = jnp.where(row0, carry[0:1, :], pltpu.roll(x, 1, 0))
        carry[0:1, :] = x[R - 1:R, :]
        return x + (prev - x) * mu_ref[...]

    rm = shift(r_ref, cr, mur_ref)
    km = shift(k_ref, ck, muk_ref)
    vm = shift(v_ref, cv, muv_ref)
    wdm = shift(wd_ref, cwd, muwd_ref)
    adm = shift(ad_ref, cad, muad_ref)
    seg = seg_ref[...]
    logw, kk, k2, bv = _rwkv_mix(rm, km, vm, wdm, adm, w0_ref[...], wup_ref[...], a0_ref[...], aup_ref[...],
                                 kk_ref[...], ka_ref[...], seg)

    tri = tri_ref[...]
    li = lax.broadcasted_iota(jnp.int32, (C, LANE), 0)
    lane = lax.broadcasted_iota(jnp.int32, (C, LANE), 1)
    si = lane % 64
    strict = li > si
    incl = li >= si
    eye = jnp.where(li == si, 1.0, 0.0)
    lo = lane < 64
    rlo = lax.broadcasted_iota(jnp.int32, (LANE, LANE), 0) < 64
    llo = lax.broadcasted_iota(jnp.int32, (LANE, LANE), 1) < 64
    same = rlo == llo

    def bd(a):
        ab = a.astype(BF16)
        zero = jnp.zeros_like(ab)
        return jnp.concatenate([jnp.where(lo, ab, zero), jnp.where(lo, zero, ab)], axis=0)

    nsc = R // C
    prep = []
    for sc in range(nsc):
        rows = slice(sc * C, (sc + 1) * C)
        lw = logw[rows]
        cs = _dot01(tri, lw)
        cl = cs[C - 1:C, :]
        e_tail = jnp.exp(cl - cs)
        e_neg = jnp.exp(-cs)
        prep.append(dict(
            bt=kk[rows] * jnp.exp(cs - lw),
            bb=bv[rows] * e_neg,
            kt=k2[rows] * e_neg,
            rt=rm[rows] * jnp.exp(cs),
            bh=bv[rows] * e_tail,
            kh=k2[rows] * e_tail,
            pc=jnp.exp(cl), v=vm[rows]))
    npair = HB // 2
    keys = [(sc, p) for sc in range(nsc) for p in range(npair)]
    part = lambda name: {k: prep[k[0]][name][:, k[1] * LANE:(k[1] + 1) * LANE] for k in keys}
    bt, bb, kt, rt, bh, kh, vh, pc = (part(n) for n in ("bt", "bb", "kt", "rt", "bh", "kh", "v", "pc"))
    lhs = {k: jnp.concatenate([bt[k], rt[k]], 0) for k in keys}
    gb = {k: _dot_nt(lhs[k], bd(bb[k])) for k in keys}
    gk = {k: _dot_nt(lhs[k], bd(kt[k])) for k in keys}
    lk = {k: jnp.where(strict, gk[k][0:C], 0.0) for k in keys}
    rb = {k: jnp.where(incl, gb[k][C:2 * C], 0.0) for k in keys}
    rkm = {k: jnp.where(incl, gk[k][C:2 * C], 0.0) for k in keys}
    x = {k: jnp.where(strict, -gb[k][0:C], 0.0) for k in keys}
    t = {k: eye + x[k] for k in keys}
    for _ in range(int(math.log2(C)) - 1):
        x = {k: _dot(x[k], bd(x[k])) for k in keys}
        t = {k: t[k] + _dot(t[k], bd(x[k])) for k in keys}
    bdv = {k: bd(vh[k]) for k in keys}
    lkv = {k: _dot(lk[k], bdv[k]) for k in keys}
    tb = {k: _dot(t[k], bd(bt[k])) for k in keys}
    tlv = {k: _dot(t[k], bd(lkv[k])) for k in keys}
    rq = {k: rt[k] - _dot(rb[k], bd(tb[k])) for k in keys}
    yc = {k: _dot(rkm[k], bdv[k]) - _dot(rb[k], bd(tlv[k])) for k in keys}
    mq = {k: jnp.where(same, _dot_tn(tb[k], bh[k]), 0.0).astype(BF16) for k in keys}
    nf = {k: _dot_tn(jnp.concatenate([vh[k], -tlv[k]], 0), jnp.concatenate([kh[k], bh[k]], 0)) for k in keys}
    vlo = lax.broadcasted_iota(jnp.int32, (64, LANE), 1) < 64
    nn = {k: jnp.where(vlo, nf[k][0:64], nf[k][64:128]) for k in keys}
    st = [jnp.concatenate([s_ref[0, 2 * p], s_ref[0, 2 * p + 1]], axis=1) for p in range(npair)]
    o_chunks = []
    for sc in range(nsc):
        ys = [_dot_nt(rq[sc, p], bd(st[p])) + yc[sc, p] for p in range(npair)]
        st = [st[p] * pc[sc, p] - _dot(st[p], mq[sc, p]) + nn[sc, p] for p in range(npair)]
        o_chunks.append(jnp.concatenate(ys, 1))
    for p in range(npair):
        s_ref[0, 2 * p] = st[p][:, 0:64]
        s_ref[0, 2 * p + 1] = st[p][:, 64:128]
    o = jnp.concatenate(o_chunks, 0)
    y_ref[...] = _rwkv_out(o, rm, k2, vm, g_ref[...], lnw_ref[...], lnb_ref[...], rk_ref[...], seg)


def _rwkv_params(mu, w0, w_up, a0, a_up, k_k, k_a, r_k, lnx_w, lnx_b):
    v = lambda t: t.reshape(1, -1)
    return dict(
        mur=v(mu[0:4096]), muk=v(mu[4096:8192]), muv=v(mu[8192:12288]),
        muwd=v(mu[12288:12416]), muad=v(mu[12416:12544]),
        w0=v(w0), wup=w_up.astype(BF16), a0=v(a0), aup=a_up.astype(BF16), kk=v(k_k), ka=v(k_a),
        lnw=v(lnx_w), lnb=v(lnx_b), rk=v(r_k),
        seg=jnp.asarray(np.kron(np.eye(LANE // 64), np.ones((64, 64))), BF16))


def _rwkv_prompt(proj, nb, l, rp, w_ride):
    nr = l // RW_ROWS
    rb = lambda b, h, c: b * nr + c
    w512 = HB * 64
    nhg = RWKV_HEADS // HB
    wi_spec, wo_spec, wo_shape = _ride_specs(w_ride, nb * nhg * nr, lambda b, h, c: (b * nhg + h) * nr + c)
    col = lambda c0: pl.BlockSpec((RW_ROWS, w512), lambda b, h, c: (rb(b, h, c), c0 // w512 + h))
    lora = lambda c0: pl.BlockSpec((RW_ROWS, LORA), lambda b, h, c: (rb(b, h, c), c0 // LORA))
    vec = pl.BlockSpec((1, w512), lambda b, h, c: (0, h))
    vec128 = pl.BlockSpec((1, LORA), lambda b, h, c: (0, 0))
    up = pl.BlockSpec((LORA, w512), lambda b, h, c: (0, h))
    tri = jnp.tril(jnp.ones((WKV_CHUNK, WKV_CHUNK), BF16))
    gate = pl.BlockSpec((pl.Element(RW_ROWS), pl.Element(w512)),
                        lambda b, h, c: (rb(b, h, c) * RW_ROWS, pl.multiple_of(_C_G + h * w512, LANE)))
    in_specs = [col(_C_R), col(_C_K), col(_C_V), gate, lora(_C_WD), lora(_C_AD),
                vec, vec, vec, vec128, vec128,
                vec, up, vec, up, vec, vec, vec, vec, vec,
                pl.BlockSpec((LANE, LANE), lambda b, h, c: (0, 0)),
                pl.BlockSpec((WKV_CHUNK, WKV_CHUNK), lambda b, h, c: (0, 0)),
                wi_spec]
    out_specs = [pl.BlockSpec((RW_ROWS, w512), lambda b, h, c: (rb(b, h, c), h)),
                 pl.BlockSpec((1, HB, 64, 64), lambda b, h, c: (b, h, 0, 0)),
                 wo_spec]
    return pl.pallas_call(
        _rwkv_prompt_kernel,
        grid=(nb, RWKV_HEADS // HB, nr),
        in_specs=in_specs,
        out_specs=out_specs,
        out_shape=[jax.ShapeDtypeStruct((proj.shape[0], RWKV_WIDTH), BF16),
                   jax.ShapeDtypeStruct((nb, RWKV_HEADS, 64, 64), F32),
                   wo_shape],
        scratch_shapes=[pltpu.VMEM((8, w512), F32)] * 3 + [pltpu.VMEM((8, LORA), F32)] * 2,
        compiler_params=_cp(("parallel", "parallel", "arbitrary")),
        name="rwkv_prompt",
    )(proj, proj, proj, proj, proj, proj,
      rp["mur"], rp["muk"], rp["muv"], rp["muwd"], rp["muad"],
      rp["w0"], rp["wup"], rp["a0"], rp["aup"], rp["kk"], rp["ka"], rp["lnw"], rp["lnb"], rp["rk"],
      rp["seg"], tri, w_ride)


def _trig_kernel(pos_ref, freq_ref, cos_ref, sin_ref):
    ang = pos_ref[...] * freq_ref[...]
    cos_ref[...] = jnp.cos(ang)
    sin_ref[...] = jnp.sin(ang)


def _trig(pos):
    n = pos.shape[0]
    half = RET_QK_DIM // 2
    freq = (ROPE_BASE ** (-jnp.arange(half, dtype=F32) / half)).reshape(1, half)
    posb = jnp.broadcast_to(pos.astype(F32)[:, None], (n, half))
    tn = _pick_tile(n, (256, 128, 8))
    blk = pl.BlockSpec((tn, half), lambda i: (i, 0))
    return pl.pallas_call(
        _trig_kernel, grid=(n // tn,),
        in_specs=[blk, pl.BlockSpec((1, half), lambda i: (0, 0))],
        out_specs=[blk, blk],
        out_shape=[jax.ShapeDtypeStruct((n, half), F32)] * 2,
        name="rope_tables",
    )(posb, freq)


def _rotate(x, cos, sin):
    x1, x2 = x[:, :128], x[:, 128:]
    return jnp.concatenate([x1 * cos - x2 * sin, x1 * sin + x2 * cos], 1)


def _ret_decode_tile(s, gd, qrow, krow, vrow):
    lhs, rhs = _outer_rows(krow, vrow)
    sn = s * gd + _dot_tn(lhs, rhs)
    y8 = _dot(jnp.broadcast_to(qrow, (8, RET_QK_DIM)), sn)
    return sn, y8[0:1, :]


def _ret_prompt_kernel(lg_ref, gd_ref, q_ref, k_ref, v_ref, g_ref, cos_ref, sin_ref, gnw_ref, wi_ref,
                       ds_ref, dq_ref, dk_ref, dv_ref,
                       y_ref, s_ref, wo_ref, dso_ref, dy_ref):
    L = CHUNK
    wo_ref[...] = wi_ref[...].astype(BF16)
    hg = pl.program_id(1)
    c = pl.program_id(2)

    @pl.when(c == 0)
    def _init():
        s_ref[...] = jnp.zeros(s_ref.shape, F32)

    cos = cos_ref[...]
    sin = sin_ref[...]
    li = lax.broadcasted_iota(jnp.int32, (L, L), 0)
    si = lax.broadcasted_iota(jnp.int32, (L, L), 1)
    rel = (li - si).astype(F32)
    causal = li >= si
    icol = lax.broadcasted_iota(jnp.int32, (L, 1), 0).astype(F32)
    heads = range(RET_HB)
    lg = [lg_ref[hg * RET_HB + j] for j in heads]
    qr = [(_rotate(q_ref[:, j * 256:(j + 1) * 256], cos, sin) * (RET_QK_DIM ** -0.5)).astype(BF16) for j in heads]
    kr = [_rotate(k_ref[:, j * 256:(j + 1) * 256], cos, sin) for j in heads]
    v = [v_ref[:, j * 512:(j + 1) * 512].astype(BF16) for j in heads]
    s0 = [s_ref[0, j] for j in heads]
    qk = [_dot_nt(qr[j], kr[j]) for j in heads]
    y_st = [_dot(qr[j], s0[j]) for j in heads]
    s_in = [_dot_tn(kr[j] * jnp.exp((L - 1.0 - icol) * lg[j]), v[j]) for j in heads]
    sc = [qk[j] * jnp.exp(jnp.where(causal, rel * lg[j], -jnp.inf)) for j in heads]
    y_in = [_dot(sc[j], v[j]) for j in heads]
    outs = []
    for j in heads:
        s_ref[0, j] = s0[j] * jnp.exp(L * lg[j]) + s_in[j]
        y = y_in[j] + y_st[j] * jnp.exp((icol + 1.0) * lg[j])
        mu = jnp.mean(y, -1, keepdims=True)
        d = y - mu
        var = jnp.mean(d * d, -1, keepdims=True)
        outs.append(d * lax.rsqrt(var + RET_GN_EPS))
    o = jnp.concatenate(outs, 1) * gnw_ref[...]
    y_ref[...] = (o * _silu(g_ref[...])).astype(BF16)

    spp, hpb = ds_ref.shape[0], ds_ref.shape[1]
    lin = (pl.program_id(0) * pl.num_programs(1) + hg) * pl.num_programs(2) + c
    h0 = (lin % (RET_HEADS // hpb)) * hpb
    for j in range(spp):
        for hh in range(hpb):
            row = lambda ref: ref[j, hh:hh + 1, :]
            sn, yrow = _ret_decode_tile(ds_ref[j, hh], gd_ref[h0 + hh], row(dq_ref), row(dk_ref), row(dv_ref))
            dso_ref[j, hh] = sn
            dy_ref[j, hh:hh + 1, :] = yrow


def _ret_log_g():
    return jnp.log1p(-jnp.exp2(-5.0 - jnp.arange(RET_HEADS, dtype=F32)))


def _ret_prompt(proj, cos, sin, gn_w, nb, l, w_ride, s_dec, q_dec, k_dec, v_dec):
    nc = l // CHUNK
    wq, wv = RET_HB * RET_QK_DIM, RET_HB * RET_V_DIM
    nhg = RET_HEADS // RET_HB
    nsteps = nb * nhg * nc
    ns = s_dec.shape[0]
    lin = lambda b, h, c, *_: (b * nhg + h) * nc + c
    rb = lambda b, h, c, *_: b * nc + c
    spp, hpb = (ns // nsteps, RET_HEADS) if nsteps <= ns else (1, RET_HEADS * ns // nsteps)
    parts = RET_HEADS // hpb
    assert spp * hpb * nsteps == ns * RET_HEADS and hpb % 8 == 0
    dec = lambda *last: pl.BlockSpec((spp, hpb) + last,
                                     lambda b, h, c, *_: (lin(b, h, c) // parts, lin(b, h, c) % parts) + (0,) * len(last))
    wi_spec, wo_spec, wo_shape = _ride_specs(w_ride, nsteps, lin)
    grid_spec = pltpu.PrefetchScalarGridSpec(
        num_scalar_prefetch=2,
        grid=(nb, nhg, nc),
        in_specs=[
            pl.BlockSpec((CHUNK, wq), lambda b, h, c, *_: (rb(b, h, c), h)),
            pl.BlockSpec((CHUNK, wq), lambda b, h, c, *_: (rb(b, h, c), RET_QK_WIDTH // wq + h)),
            pl.BlockSpec((CHUNK, wv), lambda b, h, c, *_: (rb(b, h, c), 2 * RET_QK_WIDTH // wv + h)),
            pl.BlockSpec((CHUNK, wv), lambda b, h, c, *_: (rb(b, h, c), (2 * RET_QK_WIDTH + RET_WIDTH) // wv + h)),
            pl.BlockSpec((CHUNK, 128), lambda b, h, c, *_: (c, 0)),
            pl.BlockSpec((CHUNK, 128), lambda b, h, c, *_: (c, 0)),
            pl.BlockSpec((1, wv), lambda b, h, c, *_: (0, h)),
            wi_spec,
            dec(RET_QK_DIM, RET_V_DIM), dec(RET_QK_DIM), dec(RET_QK_DIM), dec(RET_V_DIM),
        ],
        out_specs=[pl.BlockSpec((CHUNK, wv), lambda b, h, c, *_: (rb(b, h, c), h)),
                   pl.BlockSpec((1, RET_HB, RET_QK_DIM, RET_V_DIM), lambda b, h, c, *_: (b, h, 0, 0)),
                   wo_spec,
                   dec(RET_QK_DIM, RET_V_DIM), dec(RET_V_DIM)],
    )
    log_g = _ret_log_g()
    return pl.pallas_call(
        _ret_prompt_kernel,
        grid_spec=grid_spec,
        out_shape=[jax.ShapeDtypeStruct((proj.shape[0], RET_WIDTH), BF16),
                   jax.ShapeDtypeStruct((nb, RET_HEADS, RET_QK_DIM, RET_V_DIM), F32),
                   wo_shape,
                   jax.ShapeDtypeStruct(s_dec.shape, F32),
                   jax.ShapeDtypeStruct((ns, RET_HEADS, RET_V_DIM), F32)],
        compiler_params=_cp(("parallel", "parallel", "arbitrary")),
        name="ret_prompt",
    )(log_g, jnp.exp(log_g), proj, proj, proj, proj, cos, sin, gn_w.reshape(1, -1), w_ride,
      s_dec, q_dec, k_dec, v_dec)


def _ssd_pre_kernel(xs_ref, b_ref, c_ref, csx_ref, csb_ref, csc_ref, dtc_ref,
                    cwx_ref, cwb_ref, cwc_ref, cbx_ref, cbb_ref, cbc_ref, dtb_ref, al_ref,
                    xa_ref, ba_ref, ca_ref, dt_ref, dec_ref):
    def conv(u_ref, cs_ref, w_ref, bias_ref):
        acc = bias_ref[...] + u_ref[...] * w_ref[SSD_CONV - 1:SSD_CONV, :]
        for k in range(SSD_CONV - 1):
            acc = acc + cs_ref[k] * w_ref[k:k + 1, :]
        return _silu(acc)

    xs = conv(xs_ref, csx_ref, cwx_ref, cbx_ref)
    xa_ref[...] = xs
    ba_ref[0] = conv(b_ref, csb_ref, cwb_ref, cbb_ref)
    ca_ref[0] = conv(c_ref, csc_ref, cwc_ref, cbc_ref)
    dt = _softplus(dtc_ref[0] + dtb_ref[0])
    dt_ref[0] = dt
    dec_ref[0] = jnp.exp(dt * (-jnp.exp(al_ref[0])))


def _ssd_pre(proj, rb0, ns, cs_t, dtc, sp):
    g8 = SSD_GROUPS
    in_specs = [
        pl.BlockSpec((ns, 512), lambda g: (rb0, _C_XS // 512 + g)),
        pl.BlockSpec((ns, 128), lambda g: (rb0, _C_B // 128 + g)),
        pl.BlockSpec((ns, 128), lambda g: (rb0, _C_C // 128 + g)),
        pl.BlockSpec((3, ns, 512), lambda g: (0, 0, g)),
        pl.BlockSpec((3, ns, 128), lambda g: (0, 0, 4096 // 128 + g)),
        pl.BlockSpec((3, ns, 128), lambda g: (0, 0, 5120 // 128 + g)),
        pl.BlockSpec((1, ns, 8), lambda g: (g, rb0, 0)),
        pl.BlockSpec((SSD_CONV, 512), lambda g: (0, g)),
        pl.BlockSpec((SSD_CONV, 128), lambda g: (0, g)),
        pl.BlockSpec((SSD_CONV, 128), lambda g: (0, g)),
        pl.BlockSpec((1, 512), lambda g: (0, g)),
        pl.BlockSpec((1, 128), lambda g: (0, g)),
        pl.BlockSpec((1, 128), lambda g: (0, g)),
        pl.BlockSpec((1, 1, 8), lambda g: (g, 0, 0)),
        pl.BlockSpec((1, 1, 8), lambda g: (g, 0, 0)),
    ]
    out_specs = [
        pl.BlockSpec((ns, 512), lambda g: (0, g)),
        pl.BlockSpec((1, ns, 128), lambda g: (g, 0, 0)),
        pl.BlockSpec((1, ns, 128), lambda g: (g, 0, 0)),
        pl.BlockSpec((1, ns, 8), lambda g: (g, 0, 0)),
        pl.BlockSpec((1, ns, 8), lambda g: (g, 0, 0)),
    ]
    out_shape = [
        jax.ShapeDtypeStruct((ns, SSD_WIDTH), F32),
        jax.ShapeDtypeStruct((g8, ns, SSD_STATE), F32),
        jax.ShapeDtypeStruct((g8, ns, SSD_STATE), F32),
        jax.ShapeDtypeStruct((g8, ns, 8), F32),
        jax.ShapeDtypeStruct((g8, ns, 8), F32),
    ]
    return pl.pallas_call(
        _ssd_pre_kernel, grid=(g8,), in_specs=in_specs, out_specs=out_specs, out_shape=out_shape,
        compiler_params=_cp(("parallel",)), name="ssd_sample_pre",
    )(proj, proj, proj, cs_t, cs_t, cs_t, dtc, sp["cwx"], sp["cwb"], sp["cwc"], sp["cbx"], sp["cbb"], sp["cbc"],
      sp["dtbc"], sp["alc"])


def _outer_rows(x, y):
    hi = lambda t: t.astype(BF16).astype(F32)
    xh, yh = hi(x), hi(y)
    rx = lax.broadcasted_iota(jnp.int32, (8, x.shape[1]), 0)
    ry = lax.broadcasted_iota(jnp.int32, (8, y.shape[1]), 0)
    lhs = jnp.where(rx == 1, x - xh, jnp.where((rx == 0) | (rx == 2), xh, 0.0))
    rhs = jnp.where(ry == 2, y - yh, jnp.where(ry < 2, yh, 0.0))
    return lhs.astype(BF16), rhs.astype(BF16)


def _ssm_state_kernel(dt_ref, dec_ref, s_ref, x_ref, b_ref, c_ref, so_ref, y_ref):
    i = pl.program_id(0)
    hw = SSD_HPG * SSD_HEAD_DIM
    for j in range(SSM_BT):
        b = i * SSM_BT + j
        for g in range(SSD_GROUPS):
            lhs, rhs = _outer_rows(x_ref[j, :, g * hw:(g + 1) * hw], b_ref[j, g:g + 1, :])
            xb = _dot_tn(lhs, rhs)
            new = []
            for r in range(SSD_HPG):
                h = g * SSD_HPG + r
                sn = (s_ref[j, h] * dec_ref[b * SSD_HEADS + h]
                      + xb[r * 64:(r + 1) * 64] * dt_ref[b * SSD_HEADS + h])
                so_ref[j, h] = sn
                new.append(sn)
            crow = jnp.broadcast_to(c_ref[j, g:g + 1, :], (8, SSD_STATE))
            y_ref[j, g:g + 1, :] = _dot_nt(crow, jnp.concatenate(new, 0))[0:1, :]


def _ssm_state(dt, dec, s, xa, ba, ca):
    ns = s.shape[0]
    bt = SSM_BT
    smem = pl.BlockSpec(memory_space=pltpu.SMEM)
    sblk = pl.BlockSpec((bt, SSD_HEADS, SSD_HEAD_DIM, SSD_STATE), lambda i: (i, 0, 0, 0))
    bc = pl.BlockSpec((bt, SSD_GROUPS, SSD_STATE), lambda i: (i, 0, 0))
    return pl.pallas_call(
        _ssm_state_kernel, grid=(ns // bt,),
        in_specs=[smem, smem, sblk, pl.BlockSpec((bt, 1, SSD_WIDTH), lambda i: (i, 0, 0)), bc, bc],
        out_specs=[sblk, pl.BlockSpec((bt, SSD_GROUPS, 512), lambda i: (i, 0, 0))],
        out_shape=[jax.ShapeDtypeStruct(s.shape, F32), jax.ShapeDtypeStruct((ns, SSD_GROUPS, 512), F32)],
        compiler_params=_cp(("parallel",)), name="ssm_sample_state",
    )(dt, dec, s, xa, ba, ca)


def _ssd_post_kernel(y_ref, xa_ref, z_ref, dsk_ref, nw_ref, dst_ref, o_ref):
    del dst_ref
    y = (y_ref[...] + xa_ref[...] * dsk_ref[...]) * _silu(z_ref[...])
    y = y * lax.rsqrt(jnp.mean(y * y, -1, keepdims=True) + RMS_EPS) * nw_ref[...]
    o_ref[...] = y.astype(BF16)


def _ssd_post(y, xa, proj, rb0, sp, dst):
    ns = y.shape[0]
    blk = pl.BlockSpec((ns, 512), lambda g: (0, g))
    vec = pl.BlockSpec((1, 512), lambda g: (0, g))
    return pl.pallas_call(
        _ssd_post_kernel, grid=(SSD_GROUPS,),
        in_specs=[blk, blk, pl.BlockSpec((ns, 512), lambda g: (rb0, _C_Z // 512 + g)), vec, vec,
                  pl.BlockSpec(memory_space=pl.ANY)],
        out_specs=pl.BlockSpec((ns, 512), lambda g: (rb0, g)),
        out_shape=jax.ShapeDtypeStruct(dst.shape, BF16),
        input_output_aliases={5: 0},
        compiler_params=_cp(("parallel",)), name="ssd_sample_post",
    )(y, xa, proj, sp["dsk"], sp["nw"], dst)


def _wkv_pre_kernel(r_ref, k_ref, v_ref, wd_ref, ad_ref, sr_ref, sk_ref, sv_ref, swd_ref, sad_ref,
                    mur_ref, muk_ref, muv_ref, muwd_ref, muad_ref,
                    w0_ref, wup_ref, a0_ref, aup_ref, kk_ref, ka_ref, seg_ref,
                    ro_ref, ko_ref, vo_ref, rt_ref, wt_ref, kt_ref, bt_ref, kkt_ref, vt_ref):
    mix = lambda x_ref, s_ref, mu_ref: x_ref[...] + (s_ref[...] - x_ref[...]) * mu_ref[...]
    rm = mix(r_ref, sr_ref, mur_ref)
    km = mix(k_ref, sk_ref, muk_ref)
    vm = mix(v_ref, sv_ref, muv_ref)
    wdm = mix(wd_ref, swd_ref, muwd_ref)
    adm = mix(ad_ref, sad_ref, muad_ref)
    logw, kk, k2, bv = _rwkv_mix(rm, km, vm, wdm, adm, w0_ref[...], wup_ref[...], a0_ref[...], aup_ref[...],
                                 kk_ref[...], ka_ref[...], seg_ref[...])
    ro_ref[...] = rm
    ko_ref[...] = k2
    vo_ref[...] = vm
    rt_ref[...] = rm.T
    wt_ref[...] = jnp.exp(logw).T
    kt_ref[...] = k2.T
    bt_ref[...] = bv.T
    kkt_ref[...] = kk.T
    vt_ref[...] = vm.T


def _wkv_pre(proj, rb0, shift, rp):
    ns = shift.shape[0]
    w512 = HB * 64
    col = lambda c0: pl.BlockSpec((ns, w512), lambda h: (rb0, c0 // w512 + h))
    lora = lambda c0: pl.BlockSpec((ns, LORA), lambda h: (rb0, c0 // LORA))
    scol = lambda c0: pl.BlockSpec((ns, w512), lambda h: (0, c0 // w512 + h))
    slora = lambda c0: pl.BlockSpec((ns, LORA), lambda h: (0, c0 // LORA))
    vec = pl.BlockSpec((1, w512), lambda h: (0, h))
    vec128 = pl.BlockSpec((1, LORA), lambda h: (0, 0))
    up = pl.BlockSpec((LORA, w512), lambda h: (0, h))
    row = pl.BlockSpec((ns, w512), lambda h: (0, h))
    tr = pl.BlockSpec((w512, ns), lambda h: (h, 0))
    return pl.pallas_call(
        _wkv_pre_kernel, grid=(RWKV_HEADS // HB,),
        in_specs=[col(_C_R), col(_C_K), col(_C_V), lora(_C_WD), lora(_C_AD),
                  scol(0), scol(4096), scol(8192), slora(12288), slora(12416),
                  vec, vec, vec, vec128, vec128, vec, up, vec, up, vec, vec,
                  pl.BlockSpec((LANE, LANE), lambda h: (0, 0))],
        out_specs=[row] * 3 + [tr] * 6,
        out_shape=[jax.ShapeDtypeStruct((ns, RWKV_WIDTH), F32)] * 3
        + [jax.ShapeDtypeStruct((RWKV_WIDTH, ns), F32)] * 6,
        compiler_params=_cp(("parallel",)), name="wkv_sample_pre",
    )(proj, proj, proj, proj, proj, shift, shift, shift, shift, shift,
      rp["mur"], rp["muk"], rp["muv"], rp["muwd"], rp["muad"],
      rp["w0"], rp["wup"], rp["a0"], rp["aup"], rp["kk"], rp["ka"], rp["seg"])


def _wkv_state_kernel(s_ref, r_ref, w_ref, k_ref, b_ref, kk_ref, v_ref, so_ref, y_ref):
    for hh in range(WKV_HPS):
        ch = slice(hh * 64, (hh + 1) * 64)
        r, w, k, bv, kk = r_ref[ch, :], w_ref[ch, :], k_ref[ch, :], b_ref[ch, :], kk_ref[ch, :]

        def vrow(vi, carry):
            s = s_ref[hh, vi]
            sk = jnp.sum(s * kk, axis=0, keepdims=True)
            sn = s * w - sk * bv + v_ref[pl.ds(hh * 64 + vi, 1), :] * k
            so_ref[hh, vi] = sn
            y_ref[pl.ds(hh * 64 + vi, 1), :] = jnp.sum(sn * r, axis=0, keepdims=True)
            return carry

        lax.fori_loop(0, 64, vrow, 0, unroll=WKV_UNROLL)


def _wkv_state(s, r, w, k, bvec, kk, v):
    ns = s.shape[-1]
    hps = WKV_HPS
    sblk = pl.BlockSpec((hps, 64, 64, ns), lambda i: (i, 0, 0, 0))
    ch = pl.BlockSpec((hps * 64, ns), lambda i: (i, 0))
    return pl.pallas_call(
        _wkv_state_kernel, grid=(RWKV_HEADS // hps,),
        in_specs=[sblk, ch, ch, ch, ch, ch, ch],
        out_specs=[sblk, ch],
        out_shape=[jax.ShapeDtypeStruct(s.shape, F32), jax.ShapeDtypeStruct((RWKV_WIDTH, ns), F32)],
        compiler_params=_cp(("parallel",)), name="wkv_sample_state",
    )(s, r, w, k, bvec, kk, v)


def _wkv_post_kernel(o_ref, r_ref, k_ref, v_ref, g_ref, lnw_ref, lnb_ref, rk_ref, seg_ref, dst_ref, y_ref):
    del dst_ref
    y_ref[...] = _rwkv_out(o_ref[...].T, r_ref[...], k_ref[...], v_ref[...], g_ref[...],
                           lnw_ref[...], lnb_ref[...], rk_ref[...], seg_ref[...])


def _wkv_post(o_t, r, k2, v, proj, rb0, rp, dst):
    ns = o_t.shape[1]
    w512 = HB * 64
    row = pl.BlockSpec((ns, w512), lambda h: (0, h))
    vec = pl.BlockSpec((1, w512), lambda h: (0, h))
    return pl.pallas_call(
        _wkv_post_kernel, grid=(RWKV_HEADS // HB,),
        in_specs=[pl.BlockSpec((w512, ns), lambda h: (h, 0)), row, row, row,
                  pl.BlockSpec((pl.Element(ns), pl.Element(w512)), lambda h: (rb0 * ns, pl.multiple_of(_C_G + h * w512, LANE))),
                  vec, vec, vec, pl.BlockSpec((LANE, LANE), lambda h: (0, 0)),
                  pl.BlockSpec(memory_space=pl.ANY)],
        out_specs=pl.BlockSpec((ns, w512), lambda h: (rb0, h)),
        out_shape=jax.ShapeDtypeStruct(dst.shape, BF16),
        input_output_aliases={9: 0},
        compiler_params=_cp(("parallel",)), name="wkv_sample_post",
    )(o_t, r, k2, v, proj, rp["lnw"], rp["lnb"], rp["rk"], rp["seg"], dst)


def _ret_pre_kernel(q_ref, k_ref, cos_ref, sin_ref, qo_ref, ko_ref):
    cos = cos_ref[0:1, :]
    sin = sin_ref[0:1, :]
    qo_ref[...] = _rotate(q_ref[...], cos, sin) * (RET_QK_DIM ** -0.5)
    ko_ref[...] = _rotate(k_ref[...], cos, sin)


def _ret_pre(proj, rb0, ns, cos, sin):
    blk = pl.BlockSpec((ns, RET_QK_DIM), lambda h: (0, h))
    return pl.pallas_call(
        _ret_pre_kernel, grid=(RET_HEADS,),
        in_specs=[pl.BlockSpec((ns, RET_QK_DIM), lambda h: (rb0, h)),
                  pl.BlockSpec((ns, RET_QK_DIM), lambda h: (rb0, RET_HEADS + h)),
                  pl.BlockSpec((8, 128), lambda h: (0, 0)), pl.BlockSpec((8, 128), lambda h: (0, 0))],
        out_specs=[blk, blk],
        out_shape=[jax.ShapeDtypeStruct((ns, RET_QK_WIDTH), F32)] * 2,
        compiler_params=_cp(("parallel",)), name="ret_sample_pre",
    )(proj, proj, cos, sin)


def _ret_post_kernel(y_ref, g_ref, gnw_ref, dst_ref, o_ref):
    del dst_ref
    y = y_ref[...]
    mu = jnp.mean(y, -1, keepdims=True)
    d = y - mu
    var = jnp.mean(d * d, -1, keepdims=True)
    o = d * lax.rsqrt(var + RET_GN_EPS) * gnw_ref[...]
    o_ref[...] = (o * _silu(g_ref[...])).astype(BF16)


def _ret_post(y, proj, rb0, gn_w, dst):
    ns = y.shape[0]
    blk = pl.BlockSpec((ns, RET_V_DIM), lambda h: (0, h))
    return pl.pallas_call(
        _ret_post_kernel, grid=(RET_HEADS,),
        in_specs=[blk, pl.BlockSpec((ns, RET_V_DIM), lambda h: (rb0, (2 * RET_QK_WIDTH + RET_WIDTH) // RET_V_DIM + h)),
                  pl.BlockSpec((1, RET_V_DIM), lambda h: (0, h)), pl.BlockSpec(memory_space=pl.ANY)],
        out_specs=pl.BlockSpec((ns, RET_V_DIM), lambda h: (rb0, h)),
        out_shape=jax.ShapeDtypeStruct(dst.shape, BF16),
        input_output_aliases={3: 0},
        compiler_params=_cp(("parallel",)), name="ret_sample_post",
    )(y, proj, gn_w.reshape(1, -1), dst)


def _ab_layer(xp, xs, xb, nb, l, ns, conv_s, ssm_s, shift_s, wkv_s, w_in, sp, rp, w_out, ln_w, ln_b, w_next):
    mp = nb * l
    rb0 = mp // 128
    wt = w_in.T.astype(BF16)
    n_lo = AB_DT0 // AB_TN
    starts = [j * AB_TN for j in range(n_lo)] + [AB_DT0 + SSD_HEADS + j * AB_TN
                                                 for j in range((AB_MAIN - AB_DT0) // AB_TN)]
    proj = _matmul_wt(xb, wt, starts, AB_TN, "ab_in_proj")
    pdt = _matmul_wt(xb, wt, [AB_DT0], LANE, "ab_dt_proj")[:, :SSD_HEADS]
    m = proj.shape[0]
    dt3 = pdt.reshape(m, SSD_GROUPS, SSD_HPG)
    dtc = dt3.transpose(1, 0, 2)
    dtr = dt3.transpose(1, 2, 0)

    ya, ssm_p, w_out_b = _ssd_prompt(proj, dtc, dtr, nb, l, sp, w_out)
    yb, wkv_p, w_next_b = _rwkv_prompt(proj, nb, l, rp, w_next)
    tail = lambda n, c0, c1: jnp.stack([proj[(b + 1) * l - n:(b + 1) * l, c0:c1] for b in range(nb)])
    conv_p = tail(SSD_CONV - 1, _C_XS, _C_R)
    shift_p = tail(1, _C_R, _C_G)

    xa, ba, ca, dt_s, dec_s = _ssd_pre(proj, rb0, ns, conv_s.transpose(1, 0, 2), dtc, sp)
    flat = lambda t: t.transpose(1, 0, 2).reshape(ns * SSD_HEADS)
    ssm_n, y_s = _ssm_state(flat(dt_s), flat(dec_s), ssm_s, xa.reshape(ns, 1, SSD_WIDTH),
                            ba.transpose(1, 0, 2), ca.transpose(1, 0, 2))
    ya = _ssd_post(y_s.reshape(ns, SSD_WIDTH), xa, proj, rb0, sp, ya)
    conv_n = jnp.concatenate([conv_s[:, 1:], proj[mp:, None, _C_XS:_C_R]], axis=1)

    r_s, k_s, v_s, r_t, w_t, k_t, b_t, kk_t, v_t = _wkv_pre(proj, rb0, shift_s.reshape(ns, SHIFT_DIM), rp)
    wkv_t, o_t = _wkv_state(wkv_s.transpose(1, 2, 3, 0), r_t, w_t, k_t, b_t, kk_t, v_t)
    wkv_n = wkv_t.transpose(3, 0, 1, 2)
    yb = _wkv_post(o_t, r_s, k_s, v_s, proj, rb0, rp, yb)
    shift_n = proj[mp:, None, _C_R:_C_G]

    out = _matmul2(ya, yb, w_out_b, "ab_out_proj")
    x_new, xb_new = _deepnorm_first(xp, xs, out, ln_w, ln_b, "ab_deepnorm")
    return x_new, xb_new, (conv_p, ssm_p, shift_p, wkv_p), (conv_n, ssm_n, shift_n, wkv_n), w_next_b


def _ret_layer(x, xb, nb, l, ns, ret_s, w_in, gn_w, w_out, ln_w, ln_b):
    mp = nb * l
    rb0 = mp // 128
    proj = _matmul(xb, w_in, "ret_in_proj")
    cos, sin = _trig(jnp.arange(l))
    cos_s, sin_s = _trig(jnp.full((8,), PAST_LEN))
    q_s, k_s = _ret_pre(proj, rb0, ns, cos_s, sin_s)
    v_s = proj[mp:, 2 * RET_QK_WIDTH:2 * RET_QK_WIDTH + RET_WIDTH].reshape(ns, RET_HEADS, RET_V_DIM)
    h3 = lambda t: t.reshape(ns, RET_HEADS, RET_QK_DIM)
    y, ret_p, w_out_b, ret_n, o_s = _ret_prompt(proj, cos, sin, gn_w, nb, l, w_out, ret_s, h3(q_s), h3(k_s), v_s)
    y = _ret_post(o_s.reshape(ns, RET_WIDTH), proj, rb0, gn_w, y)

    out = _matmul(y, w_out_b, "ret_out_proj")
    y_p, y_s = _deepnorm_last(x, out, ln_w, ln_b, ns, "ret_deepnorm")
    return y_p, y_s, ret_p, ret_n


def kernel(x_prompt, x_sample, state_conv, state_ssm, state_shift, state_wkv, state_ret, ab_w_in, ssd_conv_w, ssd_conv_b, ssd_dt_bias, ssd_a_log, ssd_d, ssd_norm_w, rwkv_mu, rwkv_w0, rwkv_w_up, rwkv_a0, rwkv_a_up, rwkv_k_k, rwkv_k_a, rwkv_r_k, rwkv_lnx_w, rwkv_lnx_b, ab_w_out, ab_ln_w, ab_ln_b, ret_w_in, ret_gn_w, ret_w_out, ret_ln_w, ret_ln_b):
    nb, l, d = x_prompt.shape
    ns = x_sample.shape[0]
    assert x_sample.shape[1] == 1 and l % CHUNK == 0 and ns % LANE == 0 and ns == LANE
    mp = nb * l
    xp, xs = x_prompt.reshape(mp, d), x_sample.reshape(ns, d)
    xb = jnp.concatenate([xp.astype(BF16), xs.astype(BF16)], axis=0)

    sp = _ssd_params(ssd_conv_w[0], ssd_conv_b[0], ssd_dt_bias[0], ssd_a_log[0], ssd_d[0], ssd_norm_w[0])
    rp = _rwkv_params(rwkv_mu[0], rwkv_w0[0], rwkv_w_up[0], rwkv_a0[0], rwkv_a_up[0], rwkv_k_k[0], rwkv_k_a[0],
                      rwkv_r_k[0], rwkv_lnx_w[0], rwkv_lnx_b[0])
    x, xb, pst, sst, ret_w_in_b = _ab_layer(xp, xs, xb, nb, l, ns, state_conv[0], state_ssm[0], state_shift[0],
                                            state_wkv[0], ab_w_in[0], sp, rp, ab_w_out[0], ab_ln_w[0], ab_ln_b[0],
                                            ret_w_in[0])
    y_p, y_s, ret_p, ret_n = _ret_layer(x, xb, nb, l, ns, state_ret[0], ret_w_in_b, ret_gn_w[0], ret_w_out[0],
                                        ret_ln_w[0], ret_ln_b[0])
    y_prompt = y_p.reshape(nb, l, d)
    y_sample = y_s.reshape(ns, 1, d)
    st = lambda t: t[None]
    return (y_prompt, y_sample,
            st(pst[0]), st(pst[1]), st(pst[2]), st(pst[3]), st(ret_p),
            st(sst[0]), st(sst[1]), st(sst[2]), st(sst[3]), st(ret_n))
```

```python
import functools
import math

import jax
import jax.numpy as jnp
import numpy as np
from jax import lax
from jax.experimental import pallas as pl
from jax.experimental.pallas import tpu as pltpu

F32 = jnp.float32
BF16 = jnp.bfloat16

D_MODEL = 4096
DEPTH = 2
PAST_LEN = 16384
SSD_WIDTH = 4096
SSD_HEAD_DIM = 64
SSD_HEADS = 64
SSD_GROUPS = 8
SSD_HPG = 8
SSD_STATE = 128
SSD_CONV = 4
SSD_CONV_DIM = SSD_WIDTH + 2 * SSD_GROUPS * SSD_STATE
RWKV_WIDTH = 4096
RWKV_HEAD_DIM = 64
RWKV_HEADS = 64
LORA = 128
SHIFT_DIM = 3 * RWKV_WIDTH + 2 * LORA
RET_HEADS = 16
RET_QK_DIM = 256
RET_V_DIM = 512
RET_QK_WIDTH = 4096
RET_WIDTH = 8192
ROPE_BASE = 10000.0
CHUNK = 128
ALPHA = (2 * DEPTH) ** 0.25
LN_EPS = 1e-5
RMS_EPS = 1e-5
RWKV_GN_EPS = 64e-5
RET_GN_EPS = 1e-6

LANE = 128
VMEM_LIMIT = 56 * 1024 * 1024
WKV_CHUNK = 64
RW_ROWS = 256
HB = 8
RET_HB = 4
SSM_BT = 2
SSD_CPS = 2
WKV_HPS = 2
WKV_UNROLL = 8

_C_Z, _C_XS, _C_B, _C_C = 0, 4096, 8192, 9216
_C_R, _C_K, _C_V, _C_WD, _C_AD, _C_G = 10240, 14336, 18432, 22528, 22656, 22784
AB_MAIN = 26880
AB_DT0 = 10240
AB_TN = 1280


def _cp(sem):
    return pltpu.CompilerParams(dimension_semantics=sem, vmem_limit_bytes=VMEM_LIMIT)


def _silu(x):
    return x * jax.nn.sigmoid(x)


def _softplus(x):
    return jnp.maximum(x, 0.0) + jnp.log1p(jnp.exp(-jnp.abs(x)))


def _dot(a, b):
    return jnp.dot(a.astype(BF16), b.astype(BF16), preferred_element_type=F32)


def _dot_nt(a, b):
    return lax.dot_general(a.astype(BF16), b.astype(BF16), (((1,), (1,)), ((), ())),
                           preferred_element_type=F32)


def _dot_tn(a, b):
    return lax.dot_general(a.astype(BF16), b.astype(BF16), (((0,), (0,)), ((), ())),
                           preferred_element_type=F32)


def _split(x, n):
    parts, r = [], x
    for _ in range(n):
        h = r.astype(BF16)
        parts.append(h)
        r = r - h.astype(F32)
    return parts


def _dot01(m01, x, n=3):
    return sum(jnp.dot(m01, p, preferred_element_type=F32) for p in _split(x, n))


def _dot01_r(x, m01, n=2):
    return sum(jnp.dot(p, m01, preferred_element_type=F32) for p in _split(x, n))


def _segsum(x, seg):
    r, w = x.shape
    nt = w // LANE
    tall = jnp.concatenate([x[:, i * LANE:(i + 1) * LANE] for i in range(nt)], axis=0)
    s = _dot01_r(tall, seg)
    return jnp.concatenate([s[i * r:(i + 1) * r] for i in range(nt)], axis=1)


def _onehot_cols(b, n):
    rows = lax.broadcasted_iota(jnp.int32, (LANE, n), 0)
    return jnp.where(rows == b, 1.0, 0.0).astype(BF16)


def _mm_kernel(x_ref, w_ref, o_ref):
    o_ref[...] = jnp.dot(x_ref[...], w_ref[...], preferred_element_type=F32).astype(o_ref.dtype)


def _pick_tile(n, prefs):
    for t in prefs:
        if n % t == 0:
            return t
    return n


def _matmul(x, w, name, out_dtype=F32):
    m, k = x.shape
    n = w.shape[1]
    tm = _pick_tile(m, (640, 512, 256, 128))
    tn = _pick_tile(n, (1280, 1024, 512, 256, 128) if k <= 4096 else (512, 256, 128))
    return pl.pallas_call(
        _mm_kernel,
        grid=(n // tn, m // tm),
        in_specs=[pl.BlockSpec((tm, k), lambda j, i: (i, 0)),
                  pl.BlockSpec((k, tn), lambda j, i: (0, j))],
        out_specs=pl.BlockSpec((tm, tn), lambda j, i: (i, j)),
        out_shape=jax.ShapeDtypeStruct((m, n), out_dtype),
        compiler_params=_cp(("parallel", "parallel")),
        name=name,
    )(x, w)


def _mm_wt_kernel(st_ref, x_ref, wt_ref, o_ref):
    del st_ref
    o_ref[...] = lax.dot_general(x_ref[...], wt_ref[...], (((1,), (1,)), ((), ())), preferred_element_type=F32)


def _matmul_wt(x, wt, row_starts, tn, name):
    m, k = x.shape
    nt = len(row_starts)
    tm = _pick_tile(m, (640, 512, 256, 128))
    starts = jnp.asarray(row_starts, jnp.int32)
    grid_spec = pltpu.PrefetchScalarGridSpec(
        num_scalar_prefetch=1,
        grid=(nt, m // tm),
        in_specs=[pl.BlockSpec((tm, k), lambda j, i, st: (i, 0)),
                  pl.BlockSpec((pl.Element(tn), pl.Element(k)), lambda j, i, st: (pl.multiple_of(st[j], 64), 0))],
        out_specs=pl.BlockSpec((tm, tn), lambda j, i, st: (i, j)),
    )
    return pl.pallas_call(
        _mm_wt_kernel, grid_spec=grid_spec,
        out_shape=jax.ShapeDtypeStruct((m, nt * tn), F32),
        compiler_params=_cp(("parallel", "parallel")),
        name=name,
    )(starts, x, wt)


def _mm2_kernel(a_ref, b_ref, w_ref, o_ref):
    ka = a_ref.shape[1]
    o_ref[...] = (jnp.dot(a_ref[...], w_ref[0:ka, :], preferred_element_type=F32)
                  + jnp.dot(b_ref[...], w_ref[ka:, :], preferred_element_type=F32))


def _matmul2(a, b, w, name):
    m, ka = a.shape
    kb = b.shape[1]
    n = w.shape[1]
    tm = _pick_tile(m, (640, 512, 256, 128))
    tn = _pick_tile(n, (512, 256, 128))
    return pl.pallas_call(
        _mm2_kernel,
        grid=(n // tn, m // tm),
        in_specs=[pl.BlockSpec((tm, ka), lambda j, i: (i, 0)),
                  pl.BlockSpec((tm, kb), lambda j, i: (i, 0)),
                  pl.BlockSpec((ka + kb, tn), lambda j, i: (0, j))],
        out_specs=pl.BlockSpec((tm, tn), lambda j, i: (i, j)),
        out_shape=jax.ShapeDtypeStruct((m, n), F32),
        compiler_params=_cp(("parallel", "parallel")),
        name=name,
    )(a, b, w)


def _post_norm(x, o, w, b):
    h = ALPHA * x + o
    mu = jnp.mean(h, -1, keepdims=True)
    d = h - mu
    var = jnp.mean(d * d, -1, keepdims=True)
    return d * lax.rsqrt(var + LN_EPS) * w + b


def _ln_first_kernel(xp_ref, xs_ref, o_ref, w_ref, b_ref, y_ref, yb_ref):
    is_sample = pl.program_id(0) == pl.num_programs(0) - 1
    x = jnp.where(is_sample, xs_ref[...], xp_ref[...])
    y = _post_norm(x, o_ref[...], w_ref[...], b_ref[...])
    y_ref[...] = y
    yb_ref[...] = y.astype(BF16)


def _deepnorm_first(xp, xs, o, w, b, name):
    mp, d = xp.shape
    ns = xs.shape[0]
    npt = mp // ns
    row = pl.BlockSpec((ns, d), lambda i: (i, 0))
    vec = pl.BlockSpec((1, d), lambda i: (0, 0))
    return pl.pallas_call(
        _ln_first_kernel,
        grid=(npt + 1,),
        in_specs=[pl.BlockSpec((ns, d), lambda i: (jnp.minimum(i, npt - 1), 0)),
                  pl.BlockSpec((ns, d), lambda i: (0, 0)), row, vec, vec],
        out_specs=[row, row],
        out_shape=[jax.ShapeDtypeStruct((mp + ns, d), F32), jax.ShapeDtypeStruct((mp + ns, d), BF16)],
        compiler_params=_cp(("parallel",)),
        name=name,
    )(xp, xs, o, w.reshape(1, d), b.reshape(1, d))


def _ln_last_kernel(x_ref, o_ref, w_ref, b_ref, yp_ref, ys_ref):
    is_sample = pl.program_id(0) == pl.num_programs(0) - 1
    y = _post_norm(x_ref[...], o_ref[...], w_ref[...], b_ref[...])

    @pl.when(jnp.logical_not(is_sample))
    def _prompt():
        yp_ref[...] = y

    @pl.when(is_sample)
    def _sample():
        ys_ref[...] = y


def _deepnorm_last(x, o, w, b, ns, name):
    m, d = x.shape
    npt = m // ns - 1
    row = pl.BlockSpec((ns, d), lambda i: (i, 0))
    vec = pl.BlockSpec((1, d), lambda i: (0, 0))
    return pl.pallas_call(
        _ln_last_kernel,
        grid=(npt + 1,),
        in_specs=[row, row, vec, vec],
        out_specs=[pl.BlockSpec((ns, d), lambda i: (jnp.minimum(i, npt - 1), 0)),
                   pl.BlockSpec((ns, d), lambda i: (0, 0))],
        out_shape=[jax.ShapeDtypeStruct((npt * ns, d), F32), jax.ShapeDtypeStruct((ns, d), F32)],
        compiler_params=_cp(("arbitrary",)),
        name=name,
    )(x, o, w.reshape(1, d), b.reshape(1, d))


def _ssd_prompt_kernel(z_ref, xs_ref, b_ref, c_ref, dtc_ref, dtr_ref,
                       cwx_ref, cwb_ref, cwc_ref, cbx_ref, cbb_ref, cbc_ref,
                       dtbc_ref, dtbr_ref, alc_ref, alr_ref, dsk_ref, nw_ref, tri_ref, rep64_ref, rep128_ref,
                       wi_ref, wi2_ref, y_ref, s_ref, wo_ref, wo2_ref, bufx, bufb, bufc):
    L = CHUNK
    c = pl.program_id(2)
    wo_ref[...] = wi_ref[...].astype(BF16)
    wo2_ref[...] = wi2_ref[...].astype(BF16)

    @pl.when(c == 0)
    def _init():
        for buf in (bufx, bufb, bufc):
            buf[0:8, :] = jnp.zeros((8, buf.shape[1]), F32)
        s_ref[...] = jnp.zeros(s_ref.shape, F32)

    tri = tri_ref[...]
    li = lax.broadcasted_iota(jnp.int32, (L, L), 0)
    si = lax.broadcasted_iota(jnp.int32, (L, L), 1)
    causal = li >= si
    lo = lax.broadcasted_iota(jnp.int32, (1, LANE), 1) < 64
    pairs = range(SSD_HPG // 2)
    tile = lambda a, p: a[:, p * LANE:(p + 1) * LANE]
    for ci in range(SSD_CPS):
        rows = slice(ci * L, (ci + 1) * L)

        def conv(u_ref, buf, w_ref, bias_ref):
            buf[8:8 + L, :] = u_ref[rows, :]
            acc = bias_ref[...] + buf[5:5 + L, :] * w_ref[0:1, :]
            for k in range(1, SSD_CONV):
                acc = acc + buf[5 + k:5 + k + L, :] * w_ref[k:k + 1, :]
            buf[0:8, :] = buf[L:L + 8, :]
            return _silu(acc)

        xs = conv(xs_ref, bufx, cwx_ref, cbx_ref)
        bm = conv(b_ref, bufb, cwb_ref, cbb_ref)
        cm = conv(c_ref, bufc, cwc_ref, cbc_ref)
        dtc = _softplus(dtc_ref[0, rows, :] + dtbc_ref[0])
        dtr = _softplus(dtr_ref[0, :, rows] + dtbr_ref[0])
        adt_c = dtc * (-jnp.exp(alc_ref[0]))
        adt_r = dtr * (-jnp.exp(alr_ref[0]))
        cum_c = _dot01(tri, adt_c)
        cum_r = sum(lax.dot_general(p, tri, (((1,), (1,)), ((), ())), preferred_element_type=F32)
                    for p in _split(adt_r, 3))
        cb = _dot_nt(cm, bm)
        dt_x = _dot01_r(dtc, rep64_ref[...], 3)
        cum_x = _dot01_r(cum_c, rep64_ref[...], 3)
        cum_b = _dot01_r(cum_c, rep128_ref[...], 3)
        xdt = xs * dt_x
        xdt_tail = xdt * jnp.exp(cum_x[L - 1:L, :] - cum_x)
        s_old = [s_ref[0, r] for r in range(SSD_HPG)]
        decay = [jnp.exp(jnp.where(causal, tile(cum_b, r) - cum_r[r:r + 1, :], -jnp.inf))
                 for r in range(SSD_HPG)]
        x_lo = [jnp.where(lo, tile(xdt, p), 0.0) for p in pairs]
        x_hi = [jnp.where(lo, 0.0, tile(xdt, p)) for p in pairs]
        y_in = [_dot(cb * decay[2 * p], x_lo[p]) + _dot(cb * decay[2 * p + 1], x_hi[p]) for p in pairs]
        y_st = [_dot_nt(cm, jnp.concatenate([s_old[2 * p], s_old[2 * p + 1]], 0)) for p in pairs]
        s_in = [_dot_tn(tile(xdt_tail, p), bm) for p in pairs]
        for r in range(SSD_HPG):
            half = s_in[r // 2][(r % 2) * 64:(r % 2 + 1) * 64]
            s_ref[0, r] = s_old[r] * jnp.exp(cum_c[L - 1:L, r:r + 1]) + half
        y = jnp.concatenate(y_in, axis=1) + jnp.concatenate(y_st, axis=1) * jnp.exp(cum_x) + xs * dsk_ref[...]
        y = y * _silu(z_ref[rows, :])
        y = y * lax.rsqrt(jnp.mean(y * y, -1, keepdims=True) + RMS_EPS) * nw_ref[...]
        y_ref[rows, :] = y.astype(BF16)


def _ssd_params(conv_w, conv_b, dt_bias, a_log, d_skip, norm_w):
    g = SSD_GROUPS
    return dict(
        cwx=conv_w[:, :4096], cwb=conv_w[:, 4096:5120], cwc=conv_w[:, 5120:],
        cbx=conv_b[:4096].reshape(1, -1), cbb=conv_b[4096:5120].reshape(1, -1), cbc=conv_b[5120:].reshape(1, -1),
        dtbc=dt_bias.reshape(g, 1, 8), dtbr=dt_bias.reshape(g, 8, 1),
        alc=a_log.reshape(g, 1, 8), alr=a_log.reshape(g, 8, 1),
        dsk=jnp.repeat(d_skip, SSD_HEAD_DIM).reshape(1, -1), nw=norm_w.reshape(1, -1))


def _ride_specs(w, nsteps, lin):
    r, n = w.shape
    assert r % nsteps == 0 and (r // nsteps) % 16 == 0
    blk = pl.BlockSpec((r // nsteps, n), lambda *idx: (lin(*idx), 0))
    return blk, blk, jax.ShapeDtypeStruct((r, n), BF16)


def _ssd_prompt(proj, dtc, dtr, nb, l, sp, w_ride, w_ride2):
    rows = CHUNK * SSD_CPS
    nc = l // rows
    rb = lambda b, g, c: b * nc + c
    tri = jnp.tril(jnp.ones((CHUNK, CHUNK), BF16))
    lin = lambda b, g, c: (b * SSD_GROUPS + g) * nc + c
    wi_spec, wo_spec, wo_shape = _ride_specs(w_ride, nb * SSD_GROUPS * nc, lin)
    wi2_spec, wo2_spec, wo2_shape = _ride_specs(w_ride2, nb * SSD_GROUPS * nc, lin)
    in_specs = [
        pl.BlockSpec((rows, 512), lambda b, g, c: (rb(b, g, c), _C_Z // 512 + g)),
        pl.BlockSpec((rows, 512), lambda b, g, c: (rb(b, g, c), _C_XS // 512 + g)),
        pl.BlockSpec((rows, 128), lambda b, g, c: (rb(b, g, c), _C_B // 128 + g)),
        pl.BlockSpec((rows, 128), lambda b, g, c: (rb(b, g, c), _C_C // 128 + g)),
        pl.BlockSpec((1, rows, 8), lambda b, g, c: (g, rb(b, g, c), 0)),
        pl.BlockSpec((1, 8, rows), lambda b, g, c: (g, 0, rb(b, g, c))),
        pl.BlockSpec((SSD_CONV, 512), lambda b, g, c: (0, g)),
        pl.BlockSpec((SSD_CONV, 128), lambda b, g, c: (0, g)),
        pl.BlockSpec((SSD_CONV, 128), lambda b, g, c: (0, g)),
        pl.BlockSpec((1, 512), lambda b, g, c: (0, g)),
        pl.BlockSpec((1, 128), lambda b, g, c: (0, g)),
        pl.BlockSpec((1, 128), lambda b, g, c: (0, g)),
        pl.BlockSpec((1, 1, 8), lambda b, g, c: (g, 0, 0)),
        pl.BlockSpec((1, 8, 1), lambda b, g, c: (g, 0, 0)),
        pl.BlockSpec((1, 1, 8), lambda b, g, c: (g, 0, 0)),
        pl.BlockSpec((1, 8, 1), lambda b, g, c: (g, 0, 0)),
        pl.BlockSpec((1, 512), lambda b, g, c: (0, g)),
        pl.BlockSpec((1, 512), lambda b, g, c: (0, g)),
        pl.BlockSpec((CHUNK, CHUNK), lambda b, g, c: (0, 0)),
        pl.BlockSpec((SSD_HPG, SSD_HPG * 64), lambda b, g, c: (0, 0)),
        pl.BlockSpec((SSD_HPG, SSD_HPG * LANE), lambda b, g, c: (0, 0)),
        wi_spec, wi2_spec,
    ]
    rep64 = jnp.asarray(np.kron(np.eye(SSD_HPG), np.ones((1, 64))), BF16)
    rep128 = jnp.asarray(np.kron(np.eye(SSD_HPG), np.ones((1, LANE))), BF16)
    out_specs = [pl.BlockSpec((rows, 512), lambda b, g, c: (rb(b, g, c), g)),
                 pl.BlockSpec((1, SSD_HPG, SSD_HEAD_DIM, SSD_STATE), lambda b, g, c: (b, g, 0, 0)),
                 wo_spec, wo2_spec]
    return pl.pallas_call(
        _ssd_prompt_kernel,
        grid=(nb, SSD_GROUPS, nc),
        in_specs=in_specs,
        out_specs=out_specs,
        out_shape=[jax.ShapeDtypeStruct((proj.shape[0], SSD_WIDTH), BF16),
                   jax.ShapeDtypeStruct((nb, SSD_HEADS, SSD_HEAD_DIM, SSD_STATE), F32),
                   wo_shape, wo2_shape],
        scratch_shapes=[pltpu.VMEM((CHUNK + 8, 512), F32), pltpu.VMEM((CHUNK + 8, 128), F32),
                        pltpu.VMEM((CHUNK + 8, 128), F32)],
        compiler_params=_cp(("parallel", "parallel", "arbitrary")),
        name="ssd_prompt",
    )(proj, proj, proj, proj, dtc, dtr, sp["cwx"], sp["cwb"], sp["cwc"], sp["cbx"], sp["cbb"], sp["cbc"],
      sp["dtbc"], sp["dtbr"], sp["alc"], sp["alr"], sp["dsk"], sp["nw"], tri, rep64, rep128, w_ride, w_ride2)


def _rwkv_mix(rm, km, vm, wdm, adm, w0, wup, a0, aup, k_k, k_a, seg):
    wlog = -_softplus(-(w0 + _dot(jnp.tanh(wdm), wup))) - 0.5
    logw = -jnp.exp(wlog)
    aa = jax.nn.sigmoid(a0 + _dot(adm, aup))
    kkr = km * k_k
    kk = kkr * lax.rsqrt(jnp.maximum(_segsum(kkr * kkr, seg), 1e-24))
    k2 = km * (1.0 + (aa - 1.0) * k_a)
    return logw, kk, k2, kk * aa


def _rwkv_out(o, rm, k2, vm, g, lnw, lnb, rk, seg):
    inv = 1.0 / RWKV_HEAD_DIM
    mean = _segsum(o, seg) * inv
    d = o - mean
    var = _segsum(d * d, seg) * inv
    on = d * lax.rsqrt(var + RWKV_GN_EPS) * lnw + lnb
    bonus = _segsum(rm * k2 * rk, seg) * vm
    return ((on + bonus) * _silu(g)).astype(BF16)


def _rwkv_prompt_kernel(r_ref, k_ref, v_ref, g_ref, wd_ref, ad_ref,
                        mur_ref, muk_ref, muv_ref, muwd_ref, muad_ref,
                        w0_ref, wup_ref, a0_ref, aup_ref, kk_ref, ka_ref, lnw_ref, lnb_ref, rk_ref,
                        seg_ref, tri_ref, wi_ref,
                        y_ref, s_ref, wo_ref, cr, ck, cv, cwd, cad):
    R, C = RW_ROWS, WKV_CHUNK
    wo_ref[...] = wi_ref[...].astype(BF16)
    c = pl.program_id(2)

    @pl.when(c == 0)
    def _init():
        for buf in (cr, ck, cv, cwd, cad):
            buf[...] = jnp.zeros(buf.shape, F32)
        s_ref[...] = jnp.zeros(s_ref.shape, F32)

    row0 = lax.broadcasted_iota(jnp.int32, (R, 1), 0) == 0

    def shift(x_ref, carry, mu_ref):
        x = x_ref[...]
        prev = jnp.where(row0, carry[0:1, :], pltpu.roll(x, 1, 0))
        carry[0:1, :] = x[R - 1:R, :]
        return x + (prev - x) * mu_ref[...]

    rm = shift(r_ref, cr, mur_ref)
    km = shift(k_ref, ck, muk_ref)
    vm = shift(v_ref, cv, muv_ref)
    wdm = shift(wd_ref, cwd, muwd_ref)
    adm = shift(ad_ref, cad, muad_ref)
    seg = seg_ref[...]
    logw, kk, k2, bv = _rwkv_mix(rm, km, vm, wdm, adm, w0_ref[...], wup_ref[...], a0_ref[...], aup_ref[...],
                                 kk_ref[...], ka_ref[...], seg)

    tri = tri_ref[...]
    li = lax.broadcasted_iota(jnp.int32, (C, LANE), 0)
    lane = lax.broadcasted_iota(jnp.int32, (C, LANE), 1)
    si = lane % 64
    strict = li > si
    incl = li >= si
    eye = jnp.where(li == si, 1.0, 0.0)
    lo = lane < 64
    rlo = lax.broadcasted_iota(jnp.int32, (LANE, LANE), 0) < 64
    llo = lax.broadcasted_iota(jnp.int32, (LANE, LANE), 1) < 64
    same = rlo == llo

    def bd(a):
        ab = a.astype(BF16)
        zero = jnp.zeros_like(ab)
        return jnp.concatenate([jnp.where(lo, ab, zero), jnp.where(lo, zero, ab)], axis=0)

    nsc = R // C
    prep = []
    for sc in range(nsc):
        rows = slice(sc * C, (sc + 1) * C)
        lw = logw[rows]
        cs = _dot01(tri, lw)
        cl = cs[C - 1:C, :]
        e_tail = jnp.exp(cl - cs)
        e_neg = jnp.exp(-cs)
        prep.append(dict(
            bt=kk[rows] * jnp.exp(cs - lw),
            bb=bv[rows] * e_neg,
            kt=k2[rows] * e_neg,
            rt=rm[rows] * jnp.exp(cs),
            bh=bv[rows] * e_tail,
            kh=k2[rows] * e_tail,
            pc=jnp.exp(cl), v=vm[rows]))
    npair = HB // 2
    keys = [(sc, p) for sc in range(nsc) for p in range(npair)]
    part = lambda name: {k: prep[k[0]][name][:, k[1] * LANE:(k[1] + 1) * LANE] for k in keys}
    bt, bb, kt, rt, bh, kh, vh, pc = (part(n) for n in ("bt", "bb", "kt", "rt", "bh", "kh", "v", "pc"))
    lhs = {k: jnp.concatenate([bt[k], rt[k]], 0) for k in keys}
    gb = {k: _dot_nt(lhs[k], bd(bb[k])) for k in keys}
    gk = {k: _dot_nt(lhs[k], bd(kt[k])) for k in keys}
    lk = {k: jnp.where(strict, gk[k][0:C], 0.0) for k in keys}
    rb = {k: jnp.where(incl, gb[k][C:2 * C], 0.0) for k in keys}
    rkm = {k: jnp.where(incl, gk[k][C:2 * C], 0.0) for k in keys}
    x = {k: jnp.where(strict, -gb[k][0:C], 0.0) for k in keys}
    t = {k: eye + x[k] for k in keys}
    for _ in range(int(math.log2(C)) - 1):
        x = {k: _dot(x[k], bd(x[k])) for k in keys}
        t = {k: t[k] + _dot(t[k], bd(x[k])) for k in keys}
    bdv = {k: bd(vh[k]) for k in keys}
    lkv = {k: _dot(lk[k], bdv[k]) for k in keys}
    tb = {k: _dot(t[k], bd(bt[k])) for k in keys}
    tlv = {k: _dot(t[k], bd(lkv[k])) for k in keys}
    rq = {k: rt[k] - _dot(rb[k], bd(tb[k])) for k in keys}
    yc = {k: _dot(rkm[k], bdv[k]) - _dot(rb[k], bd(tlv[k])) for k in keys}
    mq = {k: jnp.where(same, _dot_tn(tb[k], bh[k]), 0.0).astype(BF16) for k in keys}
    nf = {k: _dot_tn(jnp.concatenate([vh[k], -tlv[k]], 0), jnp.concatenate([kh[k], bh[k]], 0)) for k in keys}
    vlo = lax.broadcasted_iota(jnp.int32, (64, LANE), 1) < 64
    nn = {k: jnp.where(vlo, nf[k][0:64], nf[k][64:128]) for k in keys}
    st = [jnp.concatenate([s_ref[0, 2 * p], s_ref[0, 2 * p + 1]], axis=1) for p in range(npair)]
    o_chunks = []
    for sc in range(nsc):
        ys = [_dot_nt(rq[sc, p], bd(st[p])) + yc[sc, p] for p in range(npair)]
        st = [st[p] * pc[sc, p] - _dot(st[p], mq[sc, p]) + nn[sc, p] for p in range(npair)]
        o_chunks.append(jnp.concatenate(ys, 1))
    for p in range(npair):
        s_ref[0, 2 * p] = st[p][:, 0:64]
        s_ref[0, 2 * p + 1] = st[p][:, 64:128]
    o = jnp.concatenate(o_chunks, 0)
    y_ref[...] = _rwkv_out(o, rm, k2, vm, g_ref[...], lnw_ref[...], lnb_ref[...], rk_ref[...], seg)


def _rwkv_params(mu, w0, w_up, a0, a_up, k_k, k_a, r_k, lnx_w, lnx_b):
    v = lambda t: t.reshape(1, -1)
    return dict(
        mur=v(mu[0:4096]), muk=v(mu[4096:8192]), muv=v(mu[8192:12288]),
        muwd=v(mu[12288:12416]), muad=v(mu[12416:12544]),
        w0=v(w0), wup=w_up.astype(BF16), a0=v(a0), aup=a_up.astype(BF16), kk=v(k_k), ka=v(k_a),
        lnw=v(lnx_w), lnb=v(lnx_b), rk=v(r_k),
        seg=jnp.asarray(np.kron(np.eye(LANE // 64), np.ones((64, 64))), BF16))


def _rwkv_prompt(proj, nb, l, rp, w_ride):
    nr = l // RW_ROWS
    rb = lambda b, h, c: b * nr + c
    w512 = HB * 64
    nhg = RWKV_HEADS // HB
    wi_spec, wo_spec, wo_shape = _ride_specs(w_ride, nb * nhg * nr, lambda b, h, c: (b * nhg + h) * nr + c)
    col = lambda c0: pl.BlockSpec((RW_ROWS, w512), lambda b, h, c: (rb(b, h, c), c0 // w512 + h))
    lora = lambda c0: pl.BlockSpec((RW_ROWS, LORA), lambda b, h, c: (rb(b, h, c), c0 // LORA))
    vec = pl.BlockSpec((1, w512), lambda b, h, c: (0, h))
    vec128 = pl.BlockSpec((1, LORA), lambda b, h, c: (0, 0))
    up = pl.BlockSpec((LORA, w512), lambda b, h, c: (0, h))
    tri = jnp.tril(jnp.ones((WKV_CHUNK, WKV_CHUNK), BF16))
    gate = pl.BlockSpec((pl.Element(RW_ROWS), pl.Element(w512)),
                        lambda b, h, c: (rb(b, h, c) * RW_ROWS, pl.multiple_of(_C_G + h * w512, LANE)))
    in_specs = [col(_C_R), col(_C_K), col(_C_V), gate, lora(_C_WD), lora(_C_AD),
                vec, vec, vec, vec128, vec128,
                vec, up, vec, up, vec, vec, vec, vec, vec,
                pl.BlockSpec((LANE, LANE), lambda b, h, c: (0, 0)),
                pl.BlockSpec((WKV_CHUNK, WKV_CHUNK), lambda b, h, c: (0, 0)),
                wi_spec]
    out_specs = [pl.BlockSpec((RW_ROWS, w512), lambda b, h, c: (rb(b, h, c), h)),
                 pl.BlockSpec((1, HB, 64, 64), lambda b, h, c: (b, h, 0, 0)),
                 wo_spec]
    return pl.pallas_call(
        _rwkv_prompt_kernel,
        grid=(nb, RWKV_HEADS // HB, nr),
        in_specs=in_specs,
        out_specs=out_specs,
        out_shape=[jax.ShapeDtypeStruct((proj.shape[0], RWKV_WIDTH), BF16),
                   jax.ShapeDtypeStruct((nb, RWKV_HEADS, 64, 64), F32),
                   wo_shape],
        scratch_shapes=[pltpu.VMEM((8, w512), F32)] * 3 + [pltpu.VMEM((8, LORA), F32)] * 2,
        compiler_params=_cp(("parallel", "parallel", "arbitrary")),
        name="rwkv_prompt",
    )(proj, proj, proj, proj, proj, proj,
      rp["mur"], rp["muk"], rp["muv"], rp["muwd"], rp["muad"],
      rp["w0"], rp["wup"], rp["a0"], rp["aup"], rp["kk"], rp["ka"], rp["lnw"], rp["lnb"], rp["rk"],
      rp["seg"], tri, w_ride)


def _trig_kernel(pos_ref, freq_ref, cos_ref, sin_ref):
    ang = pos_ref[...] * freq_ref[...]
    cos_ref[...] = jnp.cos(ang)
    sin_ref[...] = jnp.sin(ang)


def _trig(pos):
    n = pos.shape[0]
    half = RET_QK_DIM // 2
    freq = (ROPE_BASE ** (-jnp.arange(half, dtype=F32) / half)).reshape(1, half)
    posb = jnp.broadcast_to(pos.astype(F32)[:, None], (n, half))
    tn = _pick_tile(n, (256, 128, 8))
    blk = pl.BlockSpec((tn, half), lambda i: (i, 0))
    return pl.pallas_call(
        _trig_kernel, grid=(n // tn,),
        in_specs=[blk, pl.BlockSpec((1, half), lambda i: (0, 0))],
        out_specs=[blk, blk],
        out_shape=[jax.ShapeDtypeStruct((n, half), F32)] * 2,
        name="rope_tables",
    )(posb, freq)


def _rotate(x, cos, sin):
    x1, x2 = x[:, :128], x[:, 128:]
    return jnp.concatenate([x1 * cos - x2 * sin, x1 * sin + x2 * cos], 1)


def _ret_decode_tile(s, gd, qrow, krow, vrow):
    lhs, rhs = _outer_rows(krow, vrow)
    sn = s * gd + _dot_tn(lhs, rhs)
    y8 = _dot(jnp.broadcast_to(qrow, (8, RET_QK_DIM)), sn)
    return sn, y8[0:1, :]


def _ret_prompt_kernel(lg_ref, gd_ref, q_ref, k_ref, v_ref, g_ref, cos_ref, sin_ref, gnw_ref,
                       ds_ref, dq_ref, dk_ref, dv_ref,
                       y_ref, s_ref, dso_ref, dy_ref):
    L = CHUNK
    hg = pl.program_id(1)
    c = pl.program_id(2)

    @pl.when(c == 0)
    def _init():
        s_ref[...] = jnp.zeros(s_ref.shape, F32)

    cos = cos_ref[...]
    sin = sin_ref[...]
    li = lax.broadcasted_iota(jnp.int32, (L, L), 0)
    si = lax.broadcasted_iota(jnp.int32, (L, L), 1)
    rel = (li - si).astype(F32)
    causal = li >= si
    icol = lax.broadcasted_iota(jnp.int32, (L, 1), 0).astype(F32)
    heads = range(RET_HB)
    lg = [lg_ref[hg * RET_HB + j] for j in heads]
    part = lambda ref, j: ref[:, j * 256:(j + 1) * 256].astype(F32)
    qr = [(_rotate(part(q_ref, j), cos, sin) * (RET_QK_DIM ** -0.5)).astype(BF16) for j in heads]
    kr = [_rotate(part(k_ref, j), cos, sin) for j in heads]
    v = [v_ref[:, j * 512:(j + 1) * 512].astype(BF16) for j in heads]
    s0 = [s_ref[0, j] for j in heads]
    qk = [_dot_nt(qr[j], kr[j]) for j in heads]
    y_st = [_dot(qr[j], s0[j]) for j in heads]
    s_in = [_dot_tn(kr[j] * jnp.exp((L - 1.0 - icol) * lg[j]), v[j]) for j in heads]
    sc = [qk[j] * jnp.exp(jnp.where(causal, rel * lg[j], -jnp.inf)) for j in heads]
    y_in = [_dot(sc[j], v[j]) for j in heads]
    outs = []
    for j in heads:
        s_ref[0, j] = s0[j] * jnp.exp(L * lg[j]) + s_in[j]
        y = y_in[j] + y_st[j] * jnp.exp((icol + 1.0) * lg[j])
        mu = jnp.mean(y, -1, keepdims=True)
        d = y - mu
        var = jnp.mean(d * d, -1, keepdims=True)
        outs.append(d * lax.rsqrt(var + RET_GN_EPS))
    o = jnp.concatenate(outs, 1) * gnw_ref[...]
    y_ref[...] = (o * _silu(g_ref[...].astype(F32))).astype(BF16)

    spp, hpb = ds_ref.shape[0], ds_ref.shape[1]
    lin = (pl.program_id(0) * pl.num_programs(1) + hg) * pl.num_programs(2) + c
    h0 = (lin % (RET_HEADS // hpb)) * hpb
    for j in range(spp):
        for hh in range(hpb):
            row = lambda ref: ref[j, hh:hh + 1, :]
            sn, yrow = _ret_decode_tile(ds_ref[j, hh], gd_ref[h0 + hh], row(dq_ref), row(dk_ref), row(dv_ref))
            dso_ref[j, hh] = sn
            dy_ref[j, hh:hh + 1, :] = yrow


def _ret_log_g():
    return jnp.log1p(-jnp.exp2(-5.0 - jnp.arange(RET_HEADS, dtype=F32)))


def _ret_prompt(proj, cos, sin, gn_w, nb, l, s_dec, q_dec, k_dec, v_dec):
    nc = l // CHUNK
    wq, wv = RET_HB * RET_QK_DIM, RET_HB * RET_V_DIM
    nhg = RET_HEADS // RET_HB
    nsteps = nb * nhg * nc
    ns = s_dec.shape[0]
    lin = lambda b, h, c, *_: (b * nhg + h) * nc + c
    rb = lambda b, h, c, *_: b * nc + c
    spp, hpb = (ns // nsteps, RET_HEADS) if nsteps <= ns else (1, RET_HEADS * ns // nsteps)
    parts = RET_HEADS // hpb
    assert spp * hpb * nsteps == ns * RET_HEADS and hpb % 8 == 0
    dec = lambda *last: pl.BlockSpec((spp, hpb) + last,
                                     lambda b, h, c, *_: (lin(b, h, c) // parts, lin(b, h, c) % parts) + (0,) * len(last))
    grid_spec = pltpu.PrefetchScalarGridSpec(
        num_scalar_prefetch=2,
        grid=(nb, nhg, nc),
        in_specs=[
            pl.BlockSpec((CHUNK, wq), lambda b, h, c, *_: (rb(b, h, c), h)),
            pl.BlockSpec((CHUNK, wq), lambda b, h, c, *_: (rb(b, h, c), RET_QK_WIDTH // wq + h)),
            pl.BlockSpec((CHUNK, wv), lambda b, h, c, *_: (rb(b, h, c), 2 * RET_QK_WIDTH // wv + h)),
            pl.BlockSpec((CHUNK, wv), lambda b, h, c, *_: (rb(b, h, c), (2 * RET_QK_WIDTH + RET_WIDTH) // wv + h)),
            pl.BlockSpec((CHUNK, 128), lambda b, h, c, *_: (c, 0)),
            pl.BlockSpec((CHUNK, 128), lambda b, h, c, *_: (c, 0)),
            pl.BlockSpec((1, wv), lambda b, h, c, *_: (0, h)),
            dec(RET_QK_DIM, RET_V_DIM), dec(RET_QK_DIM), dec(RET_QK_DIM), dec(RET_V_DIM),
        ],
        out_specs=[pl.BlockSpec((CHUNK, wv), lambda b, h, c, *_: (rb(b, h, c), h)),
                   pl.BlockSpec((1, RET_HB, RET_QK_DIM, RET_V_DIM), lambda b, h, c, *_: (b, h, 0, 0)),
                   dec(RET_QK_DIM, RET_V_DIM), dec(RET_V_DIM)],
    )
    log_g = _ret_log_g()
    return pl.pallas_call(
        _ret_prompt_kernel,
        grid_spec=grid_spec,
        out_shape=[jax.ShapeDtypeStruct((proj.shape[0], RET_WIDTH), BF16),
                   jax.ShapeDtypeStruct((nb, RET_HEADS, RET_QK_DIM, RET_V_DIM), F32),
                   jax.ShapeDtypeStruct(s_dec.shape, F32),
                   jax.ShapeDtypeStruct((ns, RET_HEADS, RET_V_DIM), F32)],
        compiler_params=_cp(("parallel", "parallel", "arbitrary")),
        name="ret_prompt",
    )(log_g, jnp.exp(log_g), proj, proj, proj, proj, cos, sin, gn_w.reshape(1, -1),
      s_dec, q_dec, k_dec, v_dec)


def _ssd_pre_kernel(xs_ref, b_ref, c_ref, csx_ref, csb_ref, csc_ref, dtc_ref,
                    cwx_ref, cwb_ref, cwc_ref, cbx_ref, cbb_ref, cbc_ref, dtb_ref, al_ref,
                    xa_ref, ba_ref, ca_ref, dt_ref, dec_ref):
    def conv(u_ref, cs_ref, w_ref, bias_ref):
        acc = bias_ref[...] + u_ref[...] * w_ref[SSD_CONV - 1:SSD_CONV, :]
        for k in range(SSD_CONV - 1):
            acc = acc + cs_ref[k] * w_ref[k:k + 1, :]
        return _silu(acc)

    xs = conv(xs_ref, csx_ref, cwx_ref, cbx_ref)
    xa_ref[...] = xs
    ba_ref[0] = conv(b_ref, csb_ref, cwb_ref, cbb_ref)
    ca_ref[0] = conv(c_ref, csc_ref, cwc_ref, cbc_ref)
    dt = _softplus(dtc_ref[0] + dtb_ref[0])
    dt_ref[0] = dt
    dec_ref[0] = jnp.exp(dt * (-jnp.exp(al_ref[0])))


def _ssd_pre(proj, rb0, ns, cs_t, dtc, sp):
    g8 = SSD_GROUPS
    in_specs = [
        pl.BlockSpec((ns, 512), lambda g: (rb0, _C_XS // 512 + g)),
        pl.BlockSpec((ns, 128), lambda g: (rb0, _C_B // 128 + g)),
        pl.BlockSpec((ns, 128), lambda g: (rb0, _C_C // 128 + g)),
        pl.BlockSpec((3, ns, 512), lambda g: (0, 0, g)),
        pl.BlockSpec((3, ns, 128), lambda g: (0, 0, 4096 // 128 + g)),
        pl.BlockSpec((3, ns, 128), lambda g: (0, 0, 5120 // 128 + g)),
        pl.BlockSpec((1, ns, 8), lambda g: (g, rb0, 0)),
        pl.BlockSpec((SSD_CONV, 512), lambda g: (0, g)),
        pl.BlockSpec((SSD_CONV, 128), lambda g: (0, g)),
        pl.BlockSpec((SSD_CONV, 128), lambda g: (0, g)),
        pl.BlockSpec((1, 512), lambda g: (0, g)),
        pl.BlockSpec((1, 128), lambda g: (0, g)),
        pl.BlockSpec((1, 128), lambda g: (0, g)),
        pl.BlockSpec((1, 1, 8), lambda g: (g, 0, 0)),
        pl.BlockSpec((1, 1, 8), lambda g: (g, 0, 0)),
    ]
    out_specs = [
        pl.BlockSpec((ns, 512), lambda g: (0, g)),
        pl.BlockSpec((1, ns, 128), lambda g: (g, 0, 0)),
        pl.BlockSpec((1, ns, 128), lambda g: (g, 0, 0)),
        pl.BlockSpec((1, ns, 8), lambda g: (g, 0, 0)),
        pl.BlockSpec((1, ns, 8), lambda g: (g, 0, 0)),
    ]
    out_shape = [
        jax.ShapeDtypeStruct((ns, SSD_WIDTH), F32),
        jax.ShapeDtypeStruct((g8, ns, SSD_STATE), F32),
        jax.ShapeDtypeStruct((g8, ns, SSD_STATE), F32),
        jax.ShapeDtypeStruct((g8, ns, 8), F32),
        jax.ShapeDtypeStruct((g8, ns, 8), F32),
    ]
    return pl.pallas_call(
        _ssd_pre_kernel, grid=(g8,), in_specs=in_specs, out_specs=out_specs, out_shape=out_shape,
        compiler_params=_cp(("parallel",)), name="ssd_sample_pre",
    )(proj, proj, proj, cs_t, cs_t, cs_t, dtc, sp["cwx"], sp["cwb"], sp["cwc"], sp["cbx"], sp["cbb"], sp["cbc"],
      sp["dtbc"], sp["alc"])


def _outer_rows(x, y):
    hi = lambda t: t.astype(BF16).astype(F32)
    xh, yh = hi(x), hi(y)
    rx = lax.broadcasted_iota(jnp.int32, (8, x.shape[1]), 0)
    ry = lax.broadcasted_iota(jnp.int32, (8, y.shape[1]), 0)
    lhs = jnp.where(rx == 1, x - xh, jnp.where((rx == 0) | (rx == 2), xh, 0.0))
    rhs = jnp.where(ry == 2, y - yh, jnp.where(ry < 2, yh, 0.0))
    return lhs.astype(BF16), rhs.astype(BF16)


def _ssm_state_kernel(dt_ref, dec_ref, s_ref, x_ref, b_ref, c_ref, so_ref, y_ref):
    i = pl.program_id(0)
    hw = SSD_HPG * SSD_HEAD_DIM
    for j in range(SSM_BT):
        b = i * SSM_BT + j
        for g in range(SSD_GROUPS):
            lhs, rhs = _outer_rows(x_ref[j, :, g * hw:(g + 1) * hw], b_ref[j, g:g + 1, :])
            xb = _dot_tn(lhs, rhs)
            new = []
            for r in range(SSD_HPG):
                h = g * SSD_HPG + r
                sn = (s_ref[j, h] * dec_ref[b * SSD_HEADS + h]
                      + xb[r * 64:(r + 1) * 64] * dt_ref[b * SSD_HEADS + h])
                so_ref[j, h] = sn
                new.append(sn)
            crow = jnp.broadcast_to(c_ref[j, g:g + 1, :], (8, SSD_STATE))
            y_ref[j, g:g + 1, :] = _dot_nt(crow, jnp.concatenate(new, 0))[0:1, :]


def _ssm_state(dt, dec, s, xa, ba, ca):
    ns = s.shape[0]
    bt = SSM_BT
    smem = pl.BlockSpec(memory_space=pltpu.SMEM)
    sblk = pl.BlockSpec((bt, SSD_HEADS, SSD_HEAD_DIM, SSD_STATE), lambda i: (i, 0, 0, 0))
    bc = pl.BlockSpec((bt, SSD_GROUPS, SSD_STATE), lambda i: (i, 0, 0))
    return pl.pallas_call(
        _ssm_state_kernel, grid=(ns // bt,),
        in_specs=[smem, smem, sblk, pl.BlockSpec((bt, 1, SSD_WIDTH), lambda i: (i, 0, 0)), bc, bc],
        out_specs=[sblk, pl.BlockSpec((bt, SSD_GROUPS, 512), lambda i: (i, 0, 0))],
        out_shape=[jax.ShapeDtypeStruct(s.shape, F32), jax.ShapeDtypeStruct((ns, SSD_GROUPS, 512), F32)],
        compiler_params=_cp(("parallel",)), name="ssm_sample_state",
    )(dt, dec, s, xa, ba, ca)


def _ssd_post_kernel(y_ref, xa_ref, z_ref, dsk_ref, nw_ref, dst_ref, o_ref):
    del dst_ref
    y = (y_ref[...] + xa_ref[...] * dsk_ref[...]) * _silu(z_ref[...])
    y = y * lax.rsqrt(jnp.mean(y * y, -1, keepdims=True) + RMS_EPS) * nw_ref[...]
    o_ref[...] = y.astype(BF16)


def _ssd_post(y, xa, proj, rb0, sp, dst):
    ns = y.shape[0]
    blk = pl.BlockSpec((ns, 512), lambda g: (0, g))
    vec = pl.BlockSpec((1, 512), lambda g: (0, g))
    return pl.pallas_call(
        _ssd_post_kernel, grid=(SSD_GROUPS,),
        in_specs=[blk, blk, pl.BlockSpec((ns, 512), lambda g: (rb0, _C_Z // 512 + g)), vec, vec,
                  pl.BlockSpec(memory_space=pl.ANY)],
        out_specs=pl.BlockSpec((ns, 512), lambda g: (rb0, g)),
        out_shape=jax.ShapeDtypeStruct(dst.shape, BF16),
        input_output_aliases={5: 0},
        compiler_params=_cp(("parallel",)), name="ssd_sample_post",
    )(y, xa, proj, sp["dsk"], sp["nw"], dst)


def _wkv_pre_kernel(r_ref, k_ref, v_ref, wd_ref, ad_ref, sr_ref, sk_ref, sv_ref, swd_ref, sad_ref,
                    mur_ref, muk_ref, muv_ref, muwd_ref, muad_ref,
                    w0_ref, wup_ref, a0_ref, aup_ref, kk_ref, ka_ref, seg_ref,
                    ro_ref, ko_ref, vo_ref, rt_ref, wt_ref, kt_ref, bt_ref, kkt_ref, vt_ref):
    mix = lambda x_ref, s_ref, mu_ref: x_ref[...] + (s_ref[...] - x_ref[...]) * mu_ref[...]
    rm = mix(r_ref, sr_ref, mur_ref)
    km = mix(k_ref, sk_ref, muk_ref)
    vm = mix(v_ref, sv_ref, muv_ref)
    wdm = mix(wd_ref, swd_ref, muwd_ref)
    adm = mix(ad_ref, sad_ref, muad_ref)
    logw, kk, k2, bv = _rwkv_mix(rm, km, vm, wdm, adm, w0_ref[...], wup_ref[...], a0_ref[...], aup_ref[...],
                                 kk_ref[...], ka_ref[...], seg_ref[...])
    ro_ref[...] = rm
    ko_ref[...] = k2
    vo_ref[...] = vm
    rt_ref[...] = rm.T
    wt_ref[...] = jnp.exp(logw).T
    kt_ref[...] = k2.T
    bt_ref[...] = bv.T
    kkt_ref[...] = kk.T
    vt_ref[...] = vm.T


def _wkv_pre(proj, rb0, shift, rp):
    ns = shift.shape[0]
    w512 = HB * 64
    col = lambda c0: pl.BlockSpec((ns, w512), lambda h: (rb0, c0 // w512 + h))
    lora = lambda c0: pl.BlockSpec((ns, LORA), lambda h: (rb0, c0 // LORA))
    scol = lambda c0: pl.BlockSpec((ns, w512), lambda h: (0, c0 // w512 + h))
    slora = lambda c0: pl.BlockSpec((ns, LORA), lambda h: (0, c0 // LORA))
    vec = pl.BlockSpec((1, w512), lambda h: (0, h))
    vec128 = pl.BlockSpec((1, LORA), lambda h: (0, 0))
    up = pl.BlockSpec((LORA, w512), lambda h: (0, h))
    row = pl.BlockSpec((ns, w512), lambda h: (0, h))
    tr = pl.BlockSpec((w512, ns), lambda h: (h, 0))
    return pl.pallas_call(
        _wkv_pre_kernel, grid=(RWKV_HEADS // HB,),
        in_specs=[col(_C_R), col(_C_K), col(_C_V), lora(_C_WD), lora(_C_AD),
                  scol(0), scol(4096), scol(8192), slora(12288), slora(12416),
                  vec, vec, vec, vec128, vec128, vec, up, vec, up, vec, vec,
                  pl.BlockSpec((LANE, LANE), lambda h: (0, 0))],
        out_specs=[row] * 3 + [tr] * 6,
        out_shape=[jax.ShapeDtypeStruct((ns, RWKV_WIDTH), F32)] * 3
        + [jax.ShapeDtypeStruct((RWKV_WIDTH, ns), F32)] * 6,
        compiler_params=_cp(("parallel",)), name="wkv_sample_pre",
    )(proj, proj, proj, proj, proj, shift, shift, shift, shift, shift,
      rp["mur"], rp["muk"], rp["muv"], rp["muwd"], rp["muad"],
      rp["w0"], rp["wup"], rp["a0"], rp["aup"], rp["kk"], rp["ka"], rp["seg"])


def _wkv_state_kernel(s_ref, r_ref, w_ref, k_ref, b_ref, kk_ref, v_ref, so_ref, y_ref):
    for hh in range(WKV_HPS):
        ch = slice(hh * 64, (hh + 1) * 64)
        r, w, k, bv, kk = r_ref[ch, :], w_ref[ch, :], k_ref[ch, :], b_ref[ch, :], kk_ref[ch, :]

        def vrow(vi, carry):
            s = s_ref[hh, vi]
            sk = jnp.sum(s * kk, axis=0, keepdims=True)
            sn = s * w - sk * bv + v_ref[pl.ds(hh * 64 + vi, 1), :] * k
            so_ref[hh, vi] = sn
            y_ref[pl.ds(hh * 64 + vi, 1), :] = jnp.sum(sn * r, axis=0, keepdims=True)
            return carry

        lax.fori_loop(0, 64, vrow, 0, unroll=WKV_UNROLL)


def _wkv_state(s, r, w, k, bvec, kk, v):
    ns = s.shape[-1]
    hps = WKV_HPS
    sblk = pl.BlockSpec((hps, 64, 64, ns), lambda i: (i, 0, 0, 0))
    ch = pl.BlockSpec((hps * 64, ns), lambda i: (i, 0))
    return pl.pallas_call(
        _wkv_state_kernel, grid=(RWKV_HEADS // hps,),
        in_specs=[sblk, ch, ch, ch, ch, ch, ch],
        out_specs=[sblk, ch],
        out_shape=[jax.ShapeDtypeStruct(s.shape, F32), jax.ShapeDtypeStruct((RWKV_WIDTH, ns), F32)],
        compiler_params=_cp(("parallel",)), name="wkv_sample_state",
    )(s, r, w, k, bvec, kk, v)


def _wkv_post_kernel(o_ref, r_ref, k_ref, v_ref, g_ref, lnw_ref, lnb_ref, rk_ref, seg_ref, dst_ref, y_ref):
    del dst_ref
    y_ref[...] = _rwkv_out(o_ref[...].T, r_ref[...], k_ref[...], v_ref[...], g_ref[...],
                           lnw_ref[...], lnb_ref[...], rk_ref[...], seg_ref[...])


def _wkv_post(o_t, r, k2, v, proj, rb0, rp, dst):
    ns = o_t.shape[1]
    w512 = HB * 64
    row = pl.BlockSpec((ns, w512), lambda h: (0, h))
    vec = pl.BlockSpec((1, w512), lambda h: (0, h))
    return pl.pallas_call(
        _wkv_post_kernel, grid=(RWKV_HEADS // HB,),
        in_specs=[pl.BlockSpec((w512, ns), lambda h: (h, 0)), row, row, row,
                  pl.BlockSpec((pl.Element(ns), pl.Element(w512)), lambda h: (rb0 * ns, pl.multiple_of(_C_G + h * w512, LANE))),
                  vec, vec, vec, pl.BlockSpec((LANE, LANE), lambda h: (0, 0)),
                  pl.BlockSpec(memory_space=pl.ANY)],
        out_specs=pl.BlockSpec((ns, w512), lambda h: (rb0, h)),
        out_shape=jax.ShapeDtypeStruct(dst.shape, BF16),
        input_output_aliases={9: 0},
        compiler_params=_cp(("parallel",)), name="wkv_sample_post",
    )(o_t, r, k2, v, proj, rp["lnw"], rp["lnb"], rp["rk"], rp["seg"], dst)


def _ret_pre_kernel(q_ref, k_ref, cos_ref, sin_ref, qo_ref, ko_ref):
    cos = cos_ref[0:1, :]
    sin = sin_ref[0:1, :]
    qo_ref[...] = _rotate(q_ref[...].astype(F32), cos, sin) * (RET_QK_DIM ** -0.5)
    ko_ref[...] = _rotate(k_ref[...].astype(F32), cos, sin)


def _ret_pre(proj, rb0, ns, cos, sin):
    blk = pl.BlockSpec((ns, RET_QK_DIM), lambda h: (0, h))
    return pl.pallas_call(
        _ret_pre_kernel, grid=(RET_HEADS,),
        in_specs=[pl.BlockSpec((ns, RET_QK_DIM), lambda h: (rb0, h)),
                  pl.BlockSpec((ns, RET_QK_DIM), lambda h: (rb0, RET_HEADS + h)),
                  pl.BlockSpec((8, 128), lambda h: (0, 0)), pl.BlockSpec((8, 128), lambda h: (0, 0))],
        out_specs=[blk, blk],
        out_shape=[jax.ShapeDtypeStruct((ns, RET_QK_WIDTH), F32)] * 2,
        compiler_params=_cp(("parallel",)), name="ret_sample_pre",
    )(proj, proj, cos, sin)


def _ret_post_kernel(y_ref, g_ref, gnw_ref, dst_ref, o_ref):
    del dst_ref
    y = y_ref[...]
    mu = jnp.mean(y, -1, keepdims=True)
    d = y - mu
    var = jnp.mean(d * d, -1, keepdims=True)
    o = d * lax.rsqrt(var + RET_GN_EPS) * gnw_ref[...]
    o_ref[...] = (o * _silu(g_ref[...].astype(F32))).astype(BF16)


def _ret_post(y, proj, rb0, gn_w, dst):
    ns = y.shape[0]
    blk = pl.BlockSpec((ns, RET_V_DIM), lambda h: (0, h))
    return pl.pallas_call(
        _ret_post_kernel, grid=(RET_HEADS,),
        in_specs=[blk, pl.BlockSpec((ns, RET_V_DIM), lambda h: (rb0, (2 * RET_QK_WIDTH + RET_WIDTH) // RET_V_DIM + h)),
                  pl.BlockSpec((1, RET_V_DIM), lambda h: (0, h)), pl.BlockSpec(memory_space=pl.ANY)],
        out_specs=pl.BlockSpec((ns, RET_V_DIM), lambda h: (rb0, h)),
        out_shape=jax.ShapeDtypeStruct(dst.shape, BF16),
        input_output_aliases={3: 0},
        compiler_params=_cp(("parallel",)), name="ret_sample_post",
    )(y, proj, gn_w.reshape(1, -1), dst)


def _ab_layer(xp, xs, xb, nb, l, ns, conv_s, ssm_s, shift_s, wkv_s, w_in, sp, rp, w_out, ln_w, ln_b,
              w_next_in, w_next_out):
    mp = nb * l
    rb0 = mp // 128
    wt = w_in.T.astype(BF16)
    n_lo = AB_DT0 // AB_TN
    starts = [j * AB_TN for j in range(n_lo)] + [AB_DT0 + SSD_HEADS + j * AB_TN
                                                 for j in range((AB_MAIN - AB_DT0) // AB_TN)]
    proj = _matmul_wt(xb, wt, starts, AB_TN, "ab_in_proj")
    pdt = _matmul_wt(xb, wt, [AB_DT0], LANE, "ab_dt_proj")[:, :SSD_HEADS]
    m = proj.shape[0]
    dt3 = pdt.reshape(m, SSD_GROUPS, SSD_HPG)
    dtc = dt3.transpose(1, 0, 2)
    dtr = dt3.transpose(1, 2, 0)

    ya, ssm_p, w_out_b, w_next_out_b = _ssd_prompt(proj, dtc, dtr, nb, l, sp, w_out, w_next_out)
    yb, wkv_p, w_next_in_b = _rwkv_prompt(proj, nb, l, rp, w_next_in)
    tail = lambda n, c0, c1: jnp.stack([proj[(b + 1) * l - n:(b + 1) * l, c0:c1] for b in range(nb)])
    conv_p = tail(SSD_CONV - 1, _C_XS, _C_R)
    shift_p = tail(1, _C_R, _C_G)

    xa, ba, ca, dt_s, dec_s = _ssd_pre(proj, rb0, ns, conv_s.transpose(1, 0, 2), dtc, sp)
    flat = lambda t: t.transpose(1, 0, 2).reshape(ns * SSD_HEADS)
    ssm_n, y_s = _ssm_state(flat(dt_s), flat(dec_s), ssm_s, xa.reshape(ns, 1, SSD_WIDTH),
                            ba.transpose(1, 0, 2), ca.transpose(1, 0, 2))
    ya = _ssd_post(y_s.reshape(ns, SSD_WIDTH), xa, proj, rb0, sp, ya)
    conv_n = jnp.concatenate([conv_s[:, 1:], proj[mp:, None, _C_XS:_C_R]], axis=1)

    r_s, k_s, v_s, r_t, w_t, k_t, b_t, kk_t, v_t = _wkv_pre(proj, rb0, shift_s.reshape(ns, SHIFT_DIM), rp)
    wkv_t, o_t = _wkv_state(wkv_s.transpose(1, 2, 3, 0), r_t, w_t, k_t, b_t, kk_t, v_t)
    wkv_n = wkv_t.transpose(3, 0, 1, 2)
    yb = _wkv_post(o_t, r_s, k_s, v_s, proj, rb0, rp, yb)
    shift_n = proj[mp:, None, _C_R:_C_G]

    out = _matmul2(ya, yb, w_out_b, "ab_out_proj")
    x_new, xb_new = _deepnorm_first(xp, xs, out, ln_w, ln_b, "ab_deepnorm")
    return (x_new, xb_new, (conv_p, ssm_p, shift_p, wkv_p), (conv_n, ssm_n, shift_n, wkv_n),
            w_next_in_b, w_next_out_b)


def _ret_layer(x, xb, nb, l, ns, ret_s, w_in, gn_w, w_out, ln_w, ln_b):
    mp = nb * l
    rb0 = mp // 128
    proj = _matmul(xb, w_in, "ret_in_proj", BF16)
    cos, sin = _trig(jnp.arange(l))
    cos_s, sin_s = _trig(jnp.full((8,), PAST_LEN))
    q_s, k_s = _ret_pre(proj, rb0, ns, cos_s, sin_s)
    v_s = proj[mp:, 2 * RET_QK_WIDTH:2 * RET_QK_WIDTH + RET_WIDTH].astype(F32).reshape(ns, RET_HEADS, RET_V_DIM)
    h3 = lambda t: t.reshape(ns, RET_HEADS, RET_QK_DIM)
    y, ret_p, ret_n, o_s = _ret_prompt(proj, cos, sin, gn_w, nb, l, ret_s, h3(q_s), h3(k_s), v_s)
    y = _ret_post(o_s.reshape(ns, RET_WIDTH), proj, rb0, gn_w, y)

    out = _matmul(y, w_out, "ret_out_proj")
    y_p, y_s = _deepnorm_last(x, out, ln_w, ln_b, ns, "ret_deepnorm")
    return y_p, y_s, ret_p, ret_n


def kernel(x_prompt, x_sample, state_conv, state_ssm, state_shift, state_wkv, state_ret, ab_w_in, ssd_conv_w, ssd_conv_b, ssd_dt_bias, ssd_a_log, ssd_d, ssd_norm_w, rwkv_mu, rwkv_w0, rwkv_w_up, rwkv_a0, rwkv_a_up, rwkv_k_k, rwkv_k_a, rwkv_r_k, rwkv_lnx_w, rwkv_lnx_b, ab_w_out, ab_ln_w, ab_ln_b, ret_w_in, ret_gn_w, ret_w_out, ret_ln_w, ret_ln_b):
    nb, l, d = x_prompt.shape
    ns = x_sample.shape[0]
    assert x_sample.shape[1] == 1 and l % CHUNK == 0 and ns % LANE == 0 and ns == LANE
    mp = nb * l
    xp, xs = x_prompt.reshape(mp, d), x_sample.reshape(ns, d)
    xb = jnp.concatenate([xp.astype(BF16), xs.astype(BF16)], axis=0)

    sp = _ssd_params(ssd_conv_w[0], ssd_conv_b[0], ssd_dt_bias[0], ssd_a_log[0], ssd_d[0], ssd_norm_w[0])
    rp = _rwkv_params(rwkv_mu[0], rwkv_w0[0], rwkv_w_up[0], rwkv_a0[0], rwkv_a_up[0], rwkv_k_k[0], rwkv_k_a[0],
                      rwkv_r_k[0], rwkv_lnx_w[0], rwkv_lnx_b[0])
    x, xb, pst, sst, ret_w_in_b, ret_w_out_b = _ab_layer(
        xp, xs, xb, nb, l, ns, state_conv[0], state_ssm[0], state_shift[0], state_wkv[0],
        ab_w_in[0], sp, rp, ab_w_out[0], ab_ln_w[0], ab_ln_b[0], ret_w_in[0], ret_w_out[0])
    y_p, y_s, ret_p, ret_n = _ret_layer(x, xb, nb, l, ns, state_ret[0], ret_w_in_b, ret_gn_w[0], ret_w_out_b,
                                        ret_ln_w[0], ret_ln_b[0])
    y_prompt = y_p.reshape(nb, l, d)
    y_sample = y_s.reshape(ns, 1, d)
    st = lambda t: t[None]
    return (y_prompt, y_sample,
            st(pst[0]), st(pst[1]), st(pst[2]), st(pst[3]), st(ret_p),
            st(sst[0]), st(sst[1]), st(sst[2]), st(sst[3]), st(ret_n))
```

```python
import functools
import math

import jax
import jax.numpy as jnp
import numpy as np
from jax import lax
from jax.experimental import pallas as pl
from jax.experimental.pallas import tpu as pltpu

F32 = jnp.float32
BF16 = jnp.bfloat16

D_MODEL = 4096
DEPTH = 2
PAST_LEN = 16384
SSD_WIDTH = 4096
SSD_HEAD_DIM = 64
SSD_HEADS = 64
SSD_GROUPS = 8
SSD_HPG = 8
SSD_STATE = 128
SSD_CONV = 4
SSD_CONV_DIM = SSD_WIDTH + 2 * SSD_GROUPS * SSD_STATE
RWKV_WIDTH = 4096
RWKV_HEAD_DIM = 64
RWKV_HEADS = 64
LORA = 128
SHIFT_DIM = 3 * RWKV_WIDTH + 2 * LORA
RET_HEADS = 16
RET_QK_DIM = 256
RET_V_DIM = 512
RET_QK_WIDTH = 4096
RET_WIDTH = 8192
ROPE_BASE = 10000.0
CHUNK = 128
ALPHA = (2 * DEPTH) ** 0.25
LN_EPS = 1e-5
RMS_EPS = 1e-5
RWKV_GN_EPS = 64e-5
RET_GN_EPS = 1e-6

LANE = 128
VMEM_LIMIT = 56 * 1024 * 1024
WKV_CHUNK = 64
RW_ROWS = 256
HB = 8
RET_HB = 4
SSM_BT = 2
SSD_CPS = 2
WKV_UNROLL = 8

_C_Z, _C_XS, _C_B, _C_C = 0, 4096, 8192, 9216
_C_R, _C_K, _C_V, _C_WD, _C_AD, _C_G = 0, 4096, 8192, 12288, 12416, 12544
AB_SSD_W = 10240
AB_RWKV_W = 16640
AB_TN = 1280


def _cp(sem):
    return pltpu.CompilerParams(dimension_semantics=sem, vmem_limit_bytes=VMEM_LIMIT)


def _silu(x):
    return x * jax.nn.sigmoid(x)


def _softplus(x):
    return jnp.maximum(x, 0.0) + jnp.log1p(jnp.exp(-jnp.abs(x)))


def _dot(a, b):
    return jnp.dot(a.astype(BF16), b.astype(BF16), preferred_element_type=F32)


def _dot_nt(a, b):
    return lax.dot_general(a.astype(BF16), b.astype(BF16), (((1,), (1,)), ((), ())),
                           preferred_element_type=F32)


def _dot_tn(a, b):
    return lax.dot_general(a.astype(BF16), b.astype(BF16), (((0,), (0,)), ((), ())),
                           preferred_element_type=F32)


def _split(x, n):
    parts, r = [], x
    for _ in range(n):
        h = r.astype(BF16)
        parts.append(h)
        r = r - h.astype(F32)
    return parts


def _dot01(m01, x, n=3):
    return sum(jnp.dot(m01, p, preferred_element_type=F32) for p in _split(x, n))


def _dot01_r(x, m01, n=2):
    return sum(jnp.dot(p, m01, preferred_element_type=F32) for p in _split(x, n))


def _segsum(x, seg):
    r, w = x.shape
    nt = w // LANE
    tall = jnp.concatenate([x[:, i * LANE:(i + 1) * LANE] for i in range(nt)], axis=0)
    s = _dot01_r(tall, seg)
    return jnp.concatenate([s[i * r:(i + 1) * r] for i in range(nt)], axis=1)


def _onehot_cols(b, n):
    rows = lax.broadcasted_iota(jnp.int32, (LANE, n), 0)
    return jnp.where(rows == b, 1.0, 0.0).astype(BF16)


def _mm_kernel(x_ref, w_ref, o_ref):
    o_ref[...] = jnp.dot(x_ref[...], w_ref[...], preferred_element_type=F32).astype(o_ref.dtype)


def _pick_tile(n, prefs):
    for t in prefs:
        if n % t == 0:
            return t
    return n


def _matmul(x, w, name, out_dtype=F32):
    m, k = x.shape
    n = w.shape[1]
    tm = _pick_tile(m, (640, 512, 256, 128))
    tn = _pick_tile(n, (1280, 1024, 512, 256, 128) if k <= 4096 else (512, 256, 128))
    return pl.pallas_call(
        _mm_kernel,
        grid=(n // tn, m // tm),
        in_specs=[pl.BlockSpec((tm, k), lambda j, i: (i, 0)),
                  pl.BlockSpec((k, tn), lambda j, i: (0, j))],
        out_specs=pl.BlockSpec((tm, tn), lambda j, i: (i, j)),
        out_shape=jax.ShapeDtypeStruct((m, n), out_dtype),
        compiler_params=_cp(("parallel", "parallel")),
        name=name,
    )(x, w)


def _mm_wt_kernel(st_ref, x_ref, wt_ref, o_ref):
    del st_ref
    o_ref[...] = lax.dot_general(x_ref[...], wt_ref[...], (((1,), (1,)), ((), ())), preferred_element_type=F32)


def _mm_wt_ride_kernel(st_ref, x_ref, wt_ref, wi_ref, o_ref, wo_ref):
    _mm_wt_kernel(st_ref, x_ref, wt_ref, o_ref)
    wo_ref[...] = wi_ref[...].astype(BF16)


def _matmul_wt(x, wt, row_starts, tn, name, ride=None):
    m, k = x.shape
    nt = len(row_starts)
    tm = _pick_tile(m, (640, 512, 256, 128))
    nm = m // tm
    starts = jnp.asarray(row_starts, jnp.int32)
    in_specs = [pl.BlockSpec((tm, k), lambda j, i, st: (i, 0)),
                pl.BlockSpec((pl.Element(tn), pl.Element(k)), lambda j, i, st: (pl.multiple_of(st[j], 64), 0))]
    out_specs = pl.BlockSpec((tm, tn), lambda j, i, st: (i, j))
    out_shape = jax.ShapeDtypeStruct((m, nt * tn), F32)
    args = (starts, x, wt)
    body = _mm_wt_kernel
    if ride is not None:
        w_f32, row0, nrows = ride
        rps = nrows // (nt * nm)
        assert rps * nt * nm == nrows and rps % 16 == 0 and row0 % 8 == 0
        in_specs.append(pl.BlockSpec((pl.Element(rps), pl.Element(w_f32.shape[1])),
                                     lambda j, i, st: (pl.multiple_of(row0 + (j * nm + i) * rps, 8), 0)))
        out_specs = [out_specs, pl.BlockSpec((rps, w_f32.shape[1]), lambda j, i, st: (j * nm + i, 0))]
        out_shape = [out_shape, jax.ShapeDtypeStruct((nrows, w_f32.shape[1]), BF16)]
        args = args + (w_f32,)
        body = _mm_wt_ride_kernel
    grid_spec = pltpu.PrefetchScalarGridSpec(num_scalar_prefetch=1, grid=(nt, nm),
                                             in_specs=in_specs, out_specs=out_specs)
    return pl.pallas_call(
        body, grid_spec=grid_spec, out_shape=out_shape,
        compiler_params=_cp(("parallel", "parallel")),
        name=name,
    )(*args)


def _mm2_kernel(a_ref, b_ref, w_ref, o_ref):
    ka = a_ref.shape[1]
    o_ref[...] = (jnp.dot(a_ref[...], w_ref[0:ka, :], preferred_element_type=F32)
                  + jnp.dot(b_ref[...], w_ref[ka:, :], preferred_element_type=F32))


def _matmul2(a, b, w, name):
    m, ka = a.shape
    kb = b.shape[1]
    n = w.shape[1]
    tm = _pick_tile(m, (640, 512, 256, 128))
    tn = _pick_tile(n, (512, 256, 128))
    return pl.pallas_call(
        _mm2_kernel,
        grid=(n // tn, m // tm),
        in_specs=[pl.BlockSpec((tm, ka), lambda j, i: (i, 0)),
                  pl.BlockSpec((tm, kb), lambda j, i: (i, 0)),
                  pl.BlockSpec((ka + kb, tn), lambda j, i: (0, j))],
        out_specs=pl.BlockSpec((tm, tn), lambda j, i: (i, j)),
        out_shape=jax.ShapeDtypeStruct((m, n), F32),
        compiler_params=_cp(("parallel", "parallel")),
        name=name,
    )(a, b, w)


def _post_norm(x, o, w, b):
    h = ALPHA * x + o
    mu = jnp.mean(h, -1, keepdims=True)
    d = h - mu
    var = jnp.mean(d * d, -1, keepdims=True)
    return d * lax.rsqrt(var + LN_EPS) * w + b


def _ln_first_kernel(xp_ref, xs_ref, o_ref, w_ref, b_ref, y_ref, yb_ref):
    is_sample = pl.program_id(0) == pl.num_programs(0) - 1
    x = jnp.where(is_sample, xs_ref[...], xp_ref[...])
    y = _post_norm(x, o_ref[...], w_ref[...], b_ref[...])
    y_ref[...] = y
    yb_ref[...] = y.astype(BF16)


def _deepnorm_first(xp, xs, o, w, b, name):
    mp, d = xp.shape
    ns = xs.shape[0]
    npt = mp // ns
    row = pl.BlockSpec((ns, d), lambda i: (i, 0))
    vec = pl.BlockSpec((1, d), lambda i: (0, 0))
    return pl.pallas_call(
        _ln_first_kernel,
        grid=(npt + 1,),
        in_specs=[pl.BlockSpec((ns, d), lambda i: (jnp.minimum(i, npt - 1), 0)),
                  pl.BlockSpec((ns, d), lambda i: (0, 0)), row, vec, vec],
        out_specs=[row, row],
        out_shape=[jax.ShapeDtypeStruct((mp + ns, d), F32), jax.ShapeDtypeStruct((mp + ns, d), BF16)],
        compiler_params=_cp(("parallel",)),
        name=name,
    )(xp, xs, o, w.reshape(1, d), b.reshape(1, d))


def _ln_last_kernel(x_ref, o_ref, w_ref, b_ref, yp_ref, ys_ref):
    is_sample = pl.program_id(0) == pl.num_programs(0) - 1
    y = _post_norm(x_ref[...], o_ref[...], w_ref[...], b_ref[...])

    @pl.when(jnp.logical_not(is_sample))
    def _prompt():
        yp_ref[...] = y

    @pl.when(is_sample)
    def _sample():
        ys_ref[...] = y


def _deepnorm_last(x, o, w, b, ns, name):
    m, d = x.shape
    npt = m // ns - 1
    row = pl.BlockSpec((ns, d), lambda i: (i, 0))
    vec = pl.BlockSpec((1, d), lambda i: (0, 0))
    return pl.pallas_call(
        _ln_last_kernel,
        grid=(npt + 1,),
        in_specs=[row, row, vec, vec],
        out_specs=[pl.BlockSpec((ns, d), lambda i: (jnp.minimum(i, npt - 1), 0)),
                   pl.BlockSpec((ns, d), lambda i: (0, 0))],
        out_shape=[jax.ShapeDtypeStruct((npt * ns, d), F32), jax.ShapeDtypeStruct((ns, d), F32)],
        compiler_params=_cp(("arbitrary",)),
        name=name,
    )(x, o, w.reshape(1, d), b.reshape(1, d))


def _ssd_prompt_kernel(z_ref, xs_ref, b_ref, c_ref, dtc_ref, dtr_ref,
                       cwx_ref, cwb_ref, cwc_ref, cbx_ref, cbb_ref, cbc_ref,
                       dtbc_ref, dtbr_ref, alc_ref, alr_ref, dsk_ref, nw_ref, tri_ref, rep64_ref, rep128_ref,
                       wi_ref, wi2_ref, y_ref, s_ref, wo_ref, wo2_ref, bufx, bufb, bufc):
    L = CHUNK
    c = pl.program_id(2)
    wo_ref[...] = wi_ref[...].astype(BF16)
    wo2_ref[...] = wi2_ref[...].astype(BF16)

    @pl.when(c == 0)
    def _init():
        for buf in (bufx, bufb, bufc):
            buf[0:8, :] = jnp.zeros((8, buf.shape[1]), F32)
        s_ref[...] = jnp.zeros(s_ref.shape, F32)

    tri = tri_ref[...]
    li = lax.broadcasted_iota(jnp.int32, (L, L), 0)
    si = lax.broadcasted_iota(jnp.int32, (L, L), 1)
    causal = li >= si
    lo = lax.broadcasted_iota(jnp.int32, (1, LANE), 1) < 64
    pairs = range(SSD_HPG // 2)
    tile = lambda a, p: a[:, p * LANE:(p + 1) * LANE]
    for ci in range(SSD_CPS):
        rows = slice(ci * L, (ci + 1) * L)

        def conv(u_ref, buf, w_ref, bias_ref):
            buf[8:8 + L, :] = u_ref[rows, :]
            acc = bias_ref[...] + buf[5:5 + L, :] * w_ref[0:1, :]
            for k in range(1, SSD_CONV):
                acc = acc + buf[5 + k:5 + k + L, :] * w_ref[k:k + 1, :]
            buf[0:8, :] = buf[L:L + 8, :]
            return _silu(acc)

        xs = conv(xs_ref, bufx, cwx_ref, cbx_ref)
        bm = conv(b_ref, bufb, cwb_ref, cbb_ref)
        cm = conv(c_ref, bufc, cwc_ref, cbc_ref)
        dtc = _softplus(dtc_ref[0, rows, :] + dtbc_ref[0])
        dtr = _softplus(dtr_ref[0, :, rows] + dtbr_ref[0])
        adt_c = dtc * (-jnp.exp(alc_ref[0]))
        adt_r = dtr * (-jnp.exp(alr_ref[0]))
        cum_c = _dot01(tri, adt_c)
        cum_r = sum(lax.dot_general(p, tri, (((1,), (1,)), ((), ())), preferred_element_type=F32)
                    for p in _split(adt_r, 3))
        cb = _dot_nt(cm, bm)
        dt_x = _dot01_r(dtc, rep64_ref[...], 3)
        cum_x = _dot01_r(cum_c, rep64_ref[...], 3)
        cum_b = _dot01_r(cum_c, rep128_ref[...], 3)
        xdt = xs * dt_x
        xdt_tail = xdt * jnp.exp(cum_x[L - 1:L, :] - cum_x)
        s_old = [s_ref[0, r] for r in range(SSD_HPG)]
        decay = [jnp.exp(jnp.where(causal, tile(cum_b, r) - cum_r[r:r + 1, :], -jnp.inf))
                 for r in range(SSD_HPG)]
        x_lo = [jnp.where(lo, tile(xdt, p), 0.0) for p in pairs]
        x_hi = [jnp.where(lo, 0.0, tile(xdt, p)) for p in pairs]
        y_in = [_dot(cb * decay[2 * p], x_lo[p]) + _dot(cb * decay[2 * p + 1], x_hi[p]) for p in pairs]
        y_st = [_dot_nt(cm, jnp.concatenate([s_old[2 * p], s_old[2 * p + 1]], 0)) for p in pairs]
        s_in = [_dot_tn(tile(xdt_tail, p), bm) for p in pairs]
        for r in range(SSD_HPG):
            half = s_in[r // 2][(r % 2) * 64:(r % 2 + 1) * 64]
            s_ref[0, r] = s_old[r] * jnp.exp(cum_c[L - 1:L, r:r + 1]) + half
        y = jnp.concatenate(y_in, axis=1) + jnp.concatenate(y_st, axis=1) * jnp.exp(cum_x) + xs * dsk_ref[...]
        y = y * _silu(z_ref[rows, :])
        y = y * lax.rsqrt(jnp.mean(y * y, -1, keepdims=True) + RMS_EPS) * nw_ref[...]
        y_ref[rows, :] = y.astype(BF16)


def _ssd_params(conv_w, conv_b, dt_bias, a_log, d_skip, norm_w):
    g = SSD_GROUPS
    return dict(
        cwx=conv_w[:, :4096], cwb=conv_w[:, 4096:5120], cwc=conv_w[:, 5120:],
        cbx=conv_b[:4096].reshape(1, -1), cbb=conv_b[4096:5120].reshape(1, -1), cbc=conv_b[5120:].reshape(1, -1),
        dtbc=dt_bias.reshape(g, 1, 8), dtbr=dt_bias.reshape(g, 8, 1),
        alc=a_log.reshape(g, 1, 8), alr=a_log.reshape(g, 8, 1),
        dsk=jnp.repeat(d_skip, SSD_HEAD_DIM).reshape(1, -1), nw=norm_w.reshape(1, -1))


def _ride_specs(w, nsteps, lin):
    r, n = w.shape
    assert r % nsteps == 0 and (r // nsteps) % 16 == 0
    blk = pl.BlockSpec((r // nsteps, n), lambda *idx: (lin(*idx), 0))
    return blk, blk, jax.ShapeDtypeStruct((r, n), BF16)


def _ssd_prompt(proj, dtc, dtr, nb, l, sp, w_ride, w_ride2):
    rows = CHUNK * SSD_CPS
    nc = l // rows
    rb = lambda b, g, c, *_: b * nc + c
    tri = jnp.tril(jnp.ones((CHUNK, CHUNK), BF16))
    lin = lambda b, g, c, *_: (b * SSD_GROUPS + g) * nc + c
    nsteps = nb * SSD_GROUPS * nc
    wi_spec, wo_spec, wo_shape = _ride_specs(w_ride, nsteps, lin)
    wi2_spec, wo2_spec, wo2_shape = _ride_specs(w_ride2, nsteps, lin)
    in_specs = [
        pl.BlockSpec((rows, 512), lambda b, g, c, *_: (rb(b, g, c), _C_Z // 512 + g)),
        pl.BlockSpec((rows, 512), lambda b, g, c, *_: (rb(b, g, c), _C_XS // 512 + g)),
        pl.BlockSpec((rows, 128), lambda b, g, c, *_: (rb(b, g, c), _C_B // 128 + g)),
        pl.BlockSpec((rows, 128), lambda b, g, c, *_: (rb(b, g, c), _C_C // 128 + g)),
        pl.BlockSpec((1, rows, 8), lambda b, g, c, *_: (g, rb(b, g, c), 0)),
        pl.BlockSpec((1, 8, rows), lambda b, g, c, *_: (g, 0, rb(b, g, c))),
        pl.BlockSpec((SSD_CONV, 512), lambda b, g, c, *_: (0, g)),
        pl.BlockSpec((SSD_CONV, 128), lambda b, g, c, *_: (0, g)),
        pl.BlockSpec((SSD_CONV, 128), lambda b, g, c, *_: (0, g)),
        pl.BlockSpec((1, 512), lambda b, g, c, *_: (0, g)),
        pl.BlockSpec((1, 128), lambda b, g, c, *_: (0, g)),
        pl.BlockSpec((1, 128), lambda b, g, c, *_: (0, g)),
        pl.BlockSpec((1, 1, 8), lambda b, g, c, *_: (g, 0, 0)),
        pl.BlockSpec((1, 8, 1), lambda b, g, c, *_: (g, 0, 0)),
        pl.BlockSpec((1, 1, 8), lambda b, g, c, *_: (g, 0, 0)),
        pl.BlockSpec((1, 8, 1), lambda b, g, c, *_: (g, 0, 0)),
        pl.BlockSpec((1, 512), lambda b, g, c, *_: (0, g)),
        pl.BlockSpec((1, 512), lambda b, g, c, *_: (0, g)),
        pl.BlockSpec((CHUNK, CHUNK), lambda b, g, c, *_: (0, 0)),
        pl.BlockSpec((SSD_HPG, SSD_HPG * 64), lambda b, g, c, *_: (0, 0)),
        pl.BlockSpec((SSD_HPG, SSD_HPG * LANE), lambda b, g, c, *_: (0, 0)),
        wi_spec, wi2_spec,
    ]
    rep64 = jnp.asarray(np.kron(np.eye(SSD_HPG), np.ones((1, 64))), BF16)
    rep128 = jnp.asarray(np.kron(np.eye(SSD_HPG), np.ones((1, LANE))), BF16)
    out_specs = [pl.BlockSpec((rows, 512), lambda b, g, c, *_: (rb(b, g, c), g)),
                 pl.BlockSpec((1, SSD_HPG, SSD_HEAD_DIM, SSD_STATE), lambda b, g, c, *_: (b, g, 0, 0)),
                 wo_spec, wo2_spec]
    return pl.pallas_call(
        _ssd_prompt_kernel,
        grid=(nb, SSD_GROUPS, nc),
        in_specs=in_specs,
        out_specs=out_specs,
        out_shape=[jax.ShapeDtypeStruct((proj.shape[0], SSD_WIDTH), BF16),
                   jax.ShapeDtypeStruct((nb, SSD_HEADS, SSD_HEAD_DIM, SSD_STATE), F32),
                   wo_shape, wo2_shape],
        scratch_shapes=[pltpu.VMEM((CHUNK + 8, 512), F32), pltpu.VMEM((CHUNK + 8, 128), F32),
                        pltpu.VMEM((CHUNK + 8, 128), F32)],
        compiler_params=_cp(("parallel", "parallel", "arbitrary")),
        name="ssd_prompt",
    )(proj, proj, proj, proj, dtc, dtr, sp["cwx"], sp["cwb"], sp["cwc"], sp["cbx"], sp["cbb"], sp["cbc"],
      sp["dtbc"], sp["dtbr"], sp["alc"], sp["alr"], sp["dsk"], sp["nw"], tri, rep64, rep128, w_ride, w_ride2)


def _rwkv_mix(rm, km, vm, wdm, adm, w0, wup, a0, aup, k_k, k_a, seg):
    wlog = -_softplus(-(w0 + _dot(jnp.tanh(wdm), wup))) - 0.5
    logw = -jnp.exp(wlog)
    aa = jax.nn.sigmoid(a0 + _dot(adm, aup))
    kkr = km * k_k
    kk = kkr * lax.rsqrt(jnp.maximum(_segsum(kkr * kkr, seg), 1e-24))
    k2 = km * (1.0 + (aa - 1.0) * k_a)
    return logw, kk, k2, kk * aa


def _rwkv_out(o, rm, k2, vm, g, lnw, lnb, rk, seg):
    inv = 1.0 / RWKV_HEAD_DIM
    mean = _segsum(o, seg) * inv
    d = o - mean
    var = _segsum(d * d, seg) * inv
    on = d * lax.rsqrt(var + RWKV_GN_EPS) * lnw + lnb
    bonus = _segsum(rm * k2 * rk, seg) * vm
    return ((on + bonus) * _silu(g)).astype(BF16)


def _wkv_decode(s_ref, r_ref, w_ref, k_ref, b_ref, kk_ref, v_ref, so_ref, y_ref):
    nh, nv = s_ref.shape[0], s_ref.shape[1]
    for hh in range(nh):
        ch = slice(hh * 64, (hh + 1) * 64)
        r, w, k, bv, kk = r_ref[ch, :], w_ref[ch, :], k_ref[ch, :], b_ref[ch, :], kk_ref[ch, :]

        def vrow(vi, carry):
            s = s_ref[hh, vi]
            sk = jnp.sum(s * kk, axis=0, keepdims=True)
            sn = s * w - sk * bv + v_ref[pl.ds(hh * nv + vi, 1), :] * k
            so_ref[hh, vi] = sn
            y_ref[pl.ds(hh * nv + vi, 1), :] = jnp.sum(sn * r, axis=0, keepdims=True)
            return carry

        lax.fori_loop(0, nv, vrow, 0, unroll=min(nv, WKV_UNROLL))


def _rwkv_prompt_kernel(r_ref, k_ref, v_ref, g_ref, wd_ref, ad_ref,
                        mur_ref, muk_ref, muv_ref, muwd_ref, muad_ref,
                        w0_ref, wup_ref, a0_ref, aup_ref, kk_ref, ka_ref, lnw_ref, lnb_ref, rk_ref,
                        seg_ref, tri_ref, wi_ref,
                        ds_ref, dr_ref, dw_ref, dk_ref, db_ref, dkk_ref, dv_ref,
                        y_ref, s_ref, wo_ref, dso_ref, dy_ref, cr, ck, cv, cwd, cad):
    R, C = RW_ROWS, WKV_CHUNK
    wo_ref[...] = wi_ref[...].astype(BF16)
    c = pl.program_id(2)

    @pl.when(c == 0)
    def _init():
        for buf in (cr, ck, cv, cwd, cad):
            buf[...] = jnp.zeros(buf.shape, F32)
        s_ref[...] = jnp.zeros(s_ref.shape, F32)

    row0 = lax.broadcasted_iota(jnp.int32, (R, 1), 0) == 0

    def shift(x_ref, carry, mu_ref):
        x = x_ref[...]
        prev = jnp.where(row0, carry[0:1, :], pltpu.roll(x, 1, 0))
        carry[0:1, :] = x[R - 1:R, :]
        return x + (prev - x) * mu_ref[...]

    rm = shift(r_ref, cr, mur_ref)
    km = shift(k_ref, ck, muk_ref)
    vm = shift(v_ref, cv, muv_ref)
    wdm = shift(wd_ref, cwd, muwd_ref)
    adm = shift(ad_ref, cad, muad_ref)
    seg = seg_ref[...]
    logw, kk, k2, bv = _rwkv_mix(rm, km, vm, wdm, adm, w0_ref[...], wup_ref[...], a0_ref[...], aup_ref[...],
                                 kk_ref[...], ka_ref[...], seg)

    tri = tri_ref[...]
    li = lax.broadcasted_iota(jnp.int32, (C, LANE), 0)
    lane = lax.broadcasted_iota(jnp.int32, (C, LANE), 1)
    si = lane % 64
    strict = li > si
    incl = li >= si
    eye = jnp.where(li == si, 1.0, 0.0)
    lo = lane < 64
    rlo = lax.broadcasted_iota(jnp.int32, (LANE, LANE), 0) < 64
    llo = lax.broadcasted_iota(jnp.int32, (LANE, LANE), 1) < 64
    same = rlo == llo

    def bd(a):
        ab = a.astype(BF16)
        zero = jnp.zeros_like(ab)
        return jnp.concatenate([jnp.where(lo, ab, zero), jnp.where(lo, zero, ab)], axis=0)

    nsc = R // C
    prep = []
    for sc in range(nsc):
        rows = slice(sc * C, (sc + 1) * C)
        lw = logw[rows]
        cs = _dot01(tri, lw)
        cl = cs[C - 1:C, :]
        e_tail = jnp.exp(cl - cs)
        e_neg = jnp.exp(-cs)
        prep.append(dict(
            bt=kk[rows] * jnp.exp(cs - lw),
            bb=bv[rows] * e_neg,
            kt=k2[rows] * e_neg,
            rt=rm[rows] * jnp.exp(cs),
            bh=bv[rows] * e_tail,
            kh=k2[rows] * e_tail,
            pc=jnp.exp(cl), v=vm[rows]))
    npair = HB // 2
    keys = [(sc, p) for sc in range(nsc) for p in range(npair)]
    part = lambda name: {k: prep[k[0]][name][:, k[1] * LANE:(k[1] + 1) * LANE] for k in keys}
    bt, bb, kt, rt, bh, kh, vh, pc = (part(n) for n in ("bt", "bb", "kt", "rt", "bh", "kh", "v", "pc"))
    lhs = {k: jnp.concatenate([bt[k], rt[k]], 0) for k in keys}
    gb = {k: _dot_nt(lhs[k], bd(bb[k])) for k in keys}
    gk = {k: _dot_nt(lhs[k], bd(kt[k])) for k in keys}
    lk = {k: jnp.where(strict, gk[k][0:C], 0.0) for k in keys}
    rb = {k: jnp.where(incl, gb[k][C:2 * C], 0.0) for k in keys}
    rkm = {k: jnp.where(incl, gk[k][C:2 * C], 0.0) for k in keys}
    x = {k: jnp.where(strict, -gb[k][0:C], 0.0) for k in keys}
    t = {k: eye + x[k] for k in keys}
    for _ in range(int(math.log2(C)) - 1):
        x = {k: _dot(x[k], bd(x[k])) for k in keys}
        t = {k: t[k] + _dot(t[k], bd(x[k])) for k in keys}
    bdv = {k: bd(vh[k]) for k in keys}
    lkv = {k: _dot(lk[k], bdv[k]) for k in keys}
    tb = {k: _dot(t[k], bd(bt[k])) for k in keys}
    tlv = {k: _dot(t[k], bd(lkv[k])) for k in keys}
    rq = {k: rt[k] - _dot(rb[k], bd(tb[k])) for k in keys}
    yc = {k: _dot(rkm[k], bdv[k]) - _dot(rb[k], bd(tlv[k])) for k in keys}
    mq = {k: jnp.where(same, _dot_tn(tb[k], bh[k]), 0.0).astype(BF16) for k in keys}
    nf = {k: _dot_tn(jnp.concatenate([vh[k], -tlv[k]], 0), jnp.concatenate([kh[k], bh[k]], 0)) for k in keys}
    vlo = lax.broadcasted_iota(jnp.int32, (64, LANE), 1) < 64
    nn = {k: jnp.where(vlo, nf[k][0:64], nf[k][64:128]) for k in keys}
    st = [jnp.concatenate([s_ref[0, 2 * p], s_ref[0, 2 * p + 1]], axis=1) for p in range(npair)]
    o_chunks = []
    for sc in range(nsc):
        ys = [_dot_nt(rq[sc, p], bd(st[p])) + yc[sc, p] for p in range(npair)]
        st = [st[p] * pc[sc, p] - _dot(st[p], mq[sc, p]) + nn[sc, p] for p in range(npair)]
        o_chunks.append(jnp.concatenate(ys, 1))
    for p in range(npair):
        s_ref[0, 2 * p] = st[p][:, 0:64]
        s_ref[0, 2 * p + 1] = st[p][:, 64:128]
    o = jnp.concatenate(o_chunks, 0)
    y_ref[...] = _rwkv_out(o, rm, k2, vm, g_ref[...], lnw_ref[...], lnb_ref[...], rk_ref[...], seg)
    _wkv_decode(ds_ref, dr_ref, dw_ref, dk_ref, db_ref, dkk_ref, dv_ref, dso_ref, dy_ref)


def _rwkv_params(mu, w0, w_up, a0, a_up, k_k, k_a, r_k, lnx_w, lnx_b):
    v = lambda t: t.reshape(1, -1)
    return dict(
        mur=v(mu[0:4096]), muk=v(mu[4096:8192]), muv=v(mu[8192:12288]),
        muwd=v(mu[12288:12416]), muad=v(mu[12416:12544]),
        w0=v(w0), wup=w_up.astype(BF16), a0=v(a0), aup=a_up.astype(BF16), kk=v(k_k), ka=v(k_a),
        lnw=v(lnx_w), lnb=v(lnx_b), rk=v(r_k),
        seg=jnp.asarray(np.kron(np.eye(LANE // 64), np.ones((64, 64))), BF16))


def _rwkv_prompt(proj, nb, l, rp, w_ride, s_dec, dec_rows, v_dec):
    nr = l // RW_ROWS
    rb = lambda b, h, c: b * nr + c
    w512 = HB * 64
    nhg = RWKV_HEADS // HB
    nsteps = nb * nhg * nr
    lin = lambda b, h, c: (b * nhg + h) * nr + c
    wi_spec, wo_spec, wo_shape = _ride_specs(w_ride, nsteps, lin)
    ns = s_dec.shape[-1]
    vps = RWKV_HEADS * 64 // nsteps
    assert vps * nsteps == RWKV_HEADS * 64 and (vps % 64 == 0 or (64 % vps == 0 and vps % 8 == 0))
    if vps >= 64:
        d_state = pl.BlockSpec((vps // 64, 64, 64, ns), lambda b, h, c: (lin(b, h, c), 0, 0, 0))
        d_head = pl.BlockSpec((vps, ns), lambda b, h, c: (lin(b, h, c), 0))
    else:
        per = 64 // vps
        d_state = pl.BlockSpec((1, vps, 64, ns), lambda b, h, c: (lin(b, h, c) // per, lin(b, h, c) % per, 0, 0))
        d_head = pl.BlockSpec((64, ns), lambda b, h, c: (lin(b, h, c) // per, 0))
    d_vrow = pl.BlockSpec((vps, ns), lambda b, h, c: (lin(b, h, c), 0))
    col = lambda c0: pl.BlockSpec((RW_ROWS, w512), lambda b, h, c: (rb(b, h, c), c0 // w512 + h))
    lora = lambda c0: pl.BlockSpec((RW_ROWS, LORA), lambda b, h, c: (rb(b, h, c), c0 // LORA))
    vec = pl.BlockSpec((1, w512), lambda b, h, c: (0, h))
    vec128 = pl.BlockSpec((1, LORA), lambda b, h, c: (0, 0))
    up = pl.BlockSpec((LORA, w512), lambda b, h, c: (0, h))
    tri = jnp.tril(jnp.ones((WKV_CHUNK, WKV_CHUNK), BF16))
    gate = pl.BlockSpec((pl.Element(RW_ROWS), pl.Element(w512)),
                        lambda b, h, c: (rb(b, h, c) * RW_ROWS, pl.multiple_of(_C_G + h * w512, LANE)))
    in_specs = [col(_C_R), col(_C_K), col(_C_V), gate, lora(_C_WD), lora(_C_AD),
                vec, vec, vec, vec128, vec128,
                vec, up, vec, up, vec, vec, vec, vec, vec,
                pl.BlockSpec((LANE, LANE), lambda b, h, c: (0, 0)),
                pl.BlockSpec((WKV_CHUNK, WKV_CHUNK), lambda b, h, c: (0, 0)),
                wi_spec,
                d_state, d_head, d_head, d_head, d_head, d_head, d_vrow]
    out_specs = [pl.BlockSpec((RW_ROWS, w512), lambda b, h, c: (rb(b, h, c), h)),
                 pl.BlockSpec((1, HB, 64, 64), lambda b, h, c: (b, h, 0, 0)),
                 wo_spec, d_state, d_vrow]
    return pl.pallas_call(
        _rwkv_prompt_kernel,
        grid=(nb, RWKV_HEADS // HB, nr),
        in_specs=in_specs,
        out_specs=out_specs,
        out_shape=[jax.ShapeDtypeStruct((proj.shape[0], RWKV_WIDTH), BF16),
                   jax.ShapeDtypeStruct((nb, RWKV_HEADS, 64, 64), F32),
                   wo_shape,
                   jax.ShapeDtypeStruct(s_dec.shape, F32),
                   jax.ShapeDtypeStruct((RWKV_WIDTH, ns), F32)],
        scratch_shapes=[pltpu.VMEM((8, w512), F32)] * 3 + [pltpu.VMEM((8, LORA), F32)] * 2,
        compiler_params=_cp(("parallel", "parallel", "arbitrary")),
        name="rwkv_prompt",
    )(proj, proj, proj, proj, proj, proj,
      rp["mur"], rp["muk"], rp["muv"], rp["muwd"], rp["muad"],
      rp["w0"], rp["wup"], rp["a0"], rp["aup"], rp["kk"], rp["ka"], rp["lnw"], rp["lnb"], rp["rk"],
      rp["seg"], tri, w_ride, s_dec, *dec_rows, v_dec)


def _trig_kernel(pos_ref, freq_ref, cos_ref, sin_ref):
    ang = pos_ref[...] * freq_ref[...]
    cos_ref[...] = jnp.cos(ang)
    sin_ref[...] = jnp.sin(ang)


def _trig(pos):
    n = pos.shape[0]
    half = RET_QK_DIM // 2
    freq = (ROPE_BASE ** (-jnp.arange(half, dtype=F32) / half)).reshape(1, half)
    posb = jnp.broadcast_to(pos.astype(F32)[:, None], (n, half))
    tn = _pick_tile(n, (256, 128, 8))
    blk = pl.BlockSpec((tn, half), lambda i: (i, 0))
    return pl.pallas_call(
        _trig_kernel, grid=(n // tn,),
        in_specs=[blk, pl.BlockSpec((1, half), lambda i: (0, 0))],
        out_specs=[blk, blk],
        out_shape=[jax.ShapeDtypeStruct((n, half), F32)] * 2,
        name="rope_tables",
    )(posb, freq)


def _rotate(x, cos, sin):
    x1, x2 = x[:, :128], x[:, 128:]
    return jnp.concatenate([x1 * cos - x2 * sin, x1 * sin + x2 * cos], 1)


def _ret_decode_tile(s, gd, qrow, krow, vrow):
    lhs, rhs = _outer_rows(krow, vrow)
    sn = s * gd + _dot_tn(lhs, rhs)
    y8 = _dot(jnp.broadcast_to(qrow, (8, RET_QK_DIM)), sn)
    return sn, y8[0:1, :]


def _ret_prompt_kernel(lg_ref, gd_ref, q_ref, k_ref, v_ref, g_ref, cos_ref, sin_ref, gnw_ref,
                       ds_ref, dq_ref, dk_ref, dv_ref,
                       y_ref, s_ref, dso_ref, dy_ref):
    L = CHUNK
    hg = pl.program_id(1)
    c = pl.program_id(2)

    @pl.when(c == 0)
    def _init():
        s_ref[...] = jnp.zeros(s_ref.shape, F32)

    cos = cos_ref[...]
    sin = sin_ref[...]
    li = lax.broadcasted_iota(jnp.int32, (L, L), 0)
    si = lax.broadcasted_iota(jnp.int32, (L, L), 1)
    rel = (li - si).astype(F32)
    causal = li >= si
    icol = lax.broadcasted_iota(jnp.int32, (L, 1), 0).astype(F32)
    heads = range(RET_HB)
    lg = [lg_ref[hg * RET_HB + j] for j in heads]
    part = lambda ref, j: ref[:, j * 256:(j + 1) * 256].astype(F32)
    qr = [(_rotate(part(q_ref, j), cos, sin) * (RET_QK_DIM ** -0.5)).astype(BF16) for j in heads]
    kr = [_rotate(part(k_ref, j), cos, sin) for j in heads]
    v = [v_ref[:, j * 512:(j + 1) * 512].astype(BF16) for j in heads]
    s0 = [s_ref[0, j] for j in heads]
    qk = [_dot_nt(qr[j], kr[j]) for j in heads]
    y_st = [_dot(qr[j], s0[j]) for j in heads]
    s_in = [_dot_tn(kr[j] * jnp.exp((L - 1.0 - icol) * lg[j]), v[j]) for j in heads]
    sc = [qk[j] * jnp.exp(jnp.where(causal, rel * lg[j], -jnp.inf)) for j in heads]
    y_in = [_dot(sc[j], v[j]) for j in heads]
    outs = []
    for j in heads:
        s_ref[0, j] = s0[j] * jnp.exp(L * lg[j]) + s_in[j]
        y = y_in[j] + y_st[j] * jnp.exp((icol + 1.0) * lg[j])
        mu = jnp.mean(y, -1, keepdims=True)
        d = y - mu
        var = jnp.mean(d * d, -1, keepdims=True)
        outs.append(d * lax.rsqrt(var + RET_GN_EPS))
    o = jnp.concatenate(outs, 1) * gnw_ref[...]
    y_ref[...] = (o * _silu(g_ref[...].astype(F32))).astype(BF16)

    spp, hpb = ds_ref.shape[0], ds_ref.shape[1]
    lin = (pl.program_id(0) * pl.num_programs(1) + hg) * pl.num_programs(2) + c
    h0 = (lin % (RET_HEADS // hpb)) * hpb
    for j in range(spp):
        for hh in range(hpb):
            row = lambda ref: ref[j, hh:hh + 1, :]
            sn, yrow = _ret_decode_tile(ds_ref[j, hh], gd_ref[h0 + hh], row(dq_ref), row(dk_ref), row(dv_ref))
            dso_ref[j, hh] = sn
            dy_ref[j, hh:hh + 1, :] = yrow


def _ret_log_g():
    return jnp.log1p(-jnp.exp2(-5.0 - jnp.arange(RET_HEADS, dtype=F32)))


def _ret_prompt(proj, cos, sin, gn_w, nb, l, s_dec, q_dec, k_dec, v_dec):
    nc = l // CHUNK
    wq, wv = RET_HB * RET_QK_DIM, RET_HB * RET_V_DIM
    nhg = RET_HEADS // RET_HB
    nsteps = nb * nhg * nc
    ns = s_dec.shape[0]
    lin = lambda b, h, c, *_: (b * nhg + h) * nc + c
    rb = lambda b, h, c, *_: b * nc + c
    spp, hpb = (ns // nsteps, RET_HEADS) if nsteps <= ns else (1, RET_HEADS * ns // nsteps)
    parts = RET_HEADS // hpb
    assert spp * hpb * nsteps == ns * RET_HEADS and hpb % 8 == 0
    dec = lambda *last: pl.BlockSpec((spp, hpb) + last,
                                     lambda b, h, c, *_: (lin(b, h, c) // parts, lin(b, h, c) % parts) + (0,) * len(last))
    grid_spec = pltpu.PrefetchScalarGridSpec(
        num_scalar_prefetch=2,
        grid=(nb, nhg, nc),
        in_specs=[
            pl.BlockSpec((CHUNK, wq), lambda b, h, c, *_: (rb(b, h, c), h)),
            pl.BlockSpec((CHUNK, wq), lambda b, h, c, *_: (rb(b, h, c), RET_QK_WIDTH // wq + h)),
            pl.BlockSpec((CHUNK, wv), lambda b, h, c, *_: (rb(b, h, c), 2 * RET_QK_WIDTH // wv + h)),
            pl.BlockSpec((CHUNK, wv), lambda b, h, c, *_: (rb(b, h, c), (2 * RET_QK_WIDTH + RET_WIDTH) // wv + h)),
            pl.BlockSpec((CHUNK, 128), lambda b, h, c, *_: (c, 0)),
            pl.BlockSpec((CHUNK, 128), lambda b, h, c, *_: (c, 0)),
            pl.BlockSpec((1, wv), lambda b, h, c, *_: (0, h)),
            dec(RET_QK_DIM, RET_V_DIM), dec(RET_QK_DIM), dec(RET_QK_DIM), dec(RET_V_DIM),
        ],
        out_specs=[pl.BlockSpec((CHUNK, wv), lambda b, h, c, *_: (rb(b, h, c), h)),
                   pl.BlockSpec((1, RET_HB, RET_QK_DIM, RET_V_DIM), lambda b, h, c, *_: (b, h, 0, 0)),
                   dec(RET_QK_DIM, RET_V_DIM), dec(RET_V_DIM)],
    )
    log_g = _ret_log_g()
    return pl.pallas_call(
        _ret_prompt_kernel,
        grid_spec=grid_spec,
        out_shape=[jax.ShapeDtypeStruct((proj.shape[0], RET_WIDTH), BF16),
                   jax.ShapeDtypeStruct((nb, RET_HEADS, RET_QK_DIM, RET_V_DIM), F32),
                   jax.ShapeDtypeStruct(s_dec.shape, F32),
                   jax.ShapeDtypeStruct((ns, RET_HEADS, RET_V_DIM), F32)],
        compiler_params=_cp(("parallel", "parallel", "arbitrary")),
        name="ret_prompt",
    )(log_g, jnp.exp(log_g), proj, proj, proj, proj, cos, sin, gn_w.reshape(1, -1),
      s_dec, q_dec, k_dec, v_dec)


def _ssd_pre_kernel(xs_ref, b_ref, c_ref, csx_ref, csb_ref, csc_ref, dtc_ref,
                    cwx_ref, cwb_ref, cwc_ref, cbx_ref, cbb_ref, cbc_ref, dtb_ref, al_ref,
                    xa_ref, ba_ref, ca_ref, dt_ref, dec_ref):
    def conv(u_ref, cs_ref, w_ref, bias_ref):
        acc = bias_ref[...] + u_ref[...] * w_ref[SSD_CONV - 1:SSD_CONV, :]
        for k in range(SSD_CONV - 1):
            acc = acc + cs_ref[k] * w_ref[k:k + 1, :]
        return _silu(acc)

    xs = conv(xs_ref, csx_ref, cwx_ref, cbx_ref)
    xa_ref[...] = xs
    ba_ref[0] = conv(b_ref, csb_ref, cwb_ref, cbb_ref)
    ca_ref[0] = conv(c_ref, csc_ref, cwc_ref, cbc_ref)
    dt = _softplus(dtc_ref[0] + dtb_ref[0])
    dt_ref[0] = dt
    dec_ref[0] = jnp.exp(dt * (-jnp.exp(al_ref[0])))


def _ssd_pre(proj, rb0, ns, cs_t, dtc, sp):
    g8 = SSD_GROUPS
    in_specs = [
        pl.BlockSpec((ns, 512), lambda g: (rb0, _C_XS // 512 + g)),
        pl.BlockSpec((ns, 128), lambda g: (rb0, _C_B // 128 + g)),
        pl.BlockSpec((ns, 128), lambda g: (rb0, _C_C // 128 + g)),
        pl.BlockSpec((3, ns, 512), lambda g: (0, 0, g)),
        pl.BlockSpec((3, ns, 128), lambda g: (0, 0, 4096 // 128 + g)),
        pl.BlockSpec((3, ns, 128), lambda g: (0, 0, 5120 // 128 + g)),
        pl.BlockSpec((1, ns, 8), lambda g: (g, rb0, 0)),
        pl.BlockSpec((SSD_CONV, 512), lambda g: (0, g)),
        pl.BlockSpec((SSD_CONV, 128), lambda g: (0, g)),
        pl.BlockSpec((SSD_CONV, 128), lambda g: (0, g)),
        pl.BlockSpec((1, 512), lambda g: (0, g)),
        pl.BlockSpec((1, 128), lambda g: (0, g)),
        pl.BlockSpec((1, 128), lambda g: (0, g)),
        pl.BlockSpec((1, 1, 8), lambda g: (g, 0, 0)),
        pl.BlockSpec((1, 1, 8), lambda g: (g, 0, 0)),
    ]
    out_specs = [
        pl.BlockSpec((ns, 512), lambda g: (0, g)),
        pl.BlockSpec((1, ns, 128), lambda g: (g, 0, 0)),
        pl.BlockSpec((1, ns, 128), lambda g: (g, 0, 0)),
        pl.BlockSpec((1, ns, 8), lambda g: (g, 0, 0)),
        pl.BlockSpec((1, ns, 8), lambda g: (g, 0, 0)),
    ]
    out_shape = [
        jax.ShapeDtypeStruct((ns, SSD_WIDTH), F32),
        jax.ShapeDtypeStruct((g8, ns, SSD_STATE), F32),
        jax.ShapeDtypeStruct((g8, ns, SSD_STATE), F32),
        jax.ShapeDtypeStruct((g8, ns, 8), F32),
        jax.ShapeDtypeStruct((g8, ns, 8), F32),
    ]
    return pl.pallas_call(
        _ssd_pre_kernel, grid=(g8,), in_specs=in_specs, out_specs=out_specs, out_shape=out_shape,
        compiler_params=_cp(("parallel",)), name="ssd_sample_pre",
    )(proj, proj, proj, cs_t, cs_t, cs_t, dtc, sp["cwx"], sp["cwb"], sp["cwc"], sp["cbx"], sp["cbb"], sp["cbc"],
      sp["dtbc"], sp["alc"])


def _outer_rows(x, y):
    hi = lambda t: t.astype(BF16).astype(F32)
    xh, yh = hi(x), hi(y)
    rx = lax.broadcasted_iota(jnp.int32, (8, x.shape[1]), 0)
    ry = lax.broadcasted_iota(jnp.int32, (8, y.shape[1]), 0)
    lhs = jnp.where(rx == 1, x - xh, jnp.where((rx == 0) | (rx == 2), xh, 0.0))
    rhs = jnp.where(ry == 2, y - yh, jnp.where(ry < 2, yh, 0.0))
    return lhs.astype(BF16), rhs.astype(BF16)


def _ssm_decode(dt_ref, dec_ref, s_ref, x_ref, b_ref, c_ref, so_ref, y_ref, b0, h0):
    hw = SSD_HPG * SSD_HEAD_DIM
    for j in range(s_ref.shape[0]):
        for gi in range(s_ref.shape[1] // SSD_HPG):
            g = h0 // SSD_HPG + gi
            lhs, rhs = _outer_rows(x_ref[j, :, gi * hw:(gi + 1) * hw], b_ref[j, pl.ds(g, 1), :])
            xb = _dot_tn(lhs, rhs)
            new = []
            for r in range(SSD_HPG):
                hh = gi * SSD_HPG + r
                idx = (b0 + j) * SSD_HEADS + h0 + hh
                sn = s_ref[j, hh] * dec_ref[idx] + xb[r * 64:(r + 1) * 64] * dt_ref[idx]
                so_ref[j, hh] = sn
                new.append(sn)
            crow = jnp.broadcast_to(c_ref[j, pl.ds(g, 1), :], (8, SSD_STATE))
            y_ref[j, gi] = _dot_nt(crow, jnp.concatenate(new, 0))[0:1, :]


def _ssm_state_kernel(dt_ref, dec_ref, s_ref, x_ref, b_ref, c_ref, so_ref, y_ref):
    _ssm_decode(dt_ref, dec_ref, s_ref, x_ref, b_ref, c_ref, so_ref, y_ref, pl.program_id(0) * SSM_BT, 0)


def _ssm_state(dt, dec, s, xa, ba, ca):
    ns = s.shape[0]
    bt = SSM_BT
    smem = pl.BlockSpec(memory_space=pltpu.SMEM)
    sblk = pl.BlockSpec((bt, SSD_HEADS, SSD_HEAD_DIM, SSD_STATE), lambda i: (i, 0, 0, 0))
    bc = pl.BlockSpec((bt, SSD_GROUPS, SSD_STATE), lambda i: (i, 0, 0))
    return pl.pallas_call(
        _ssm_state_kernel, grid=(ns // bt,),
        in_specs=[smem, smem, sblk, pl.BlockSpec((bt, 1, SSD_WIDTH), lambda i: (i, 0, 0)), bc, bc],
        out_specs=[sblk, pl.BlockSpec((bt, SSD_GROUPS, 1, 512), lambda i: (i, 0, 0, 0))],
        out_shape=[jax.ShapeDtypeStruct(s.shape, F32), jax.ShapeDtypeStruct((ns, SSD_GROUPS, 1, 512), F32)],
        compiler_params=_cp(("parallel",)), name="ssm_sample_state",
    )(dt, dec, s, xa, ba, ca)


def _ssd_post_kernel(y_ref, xa_ref, z_ref, dsk_ref, nw_ref, dst_ref, o_ref):
    del dst_ref
    y = (y_ref[...] + xa_ref[...] * dsk_ref[...]) * _silu(z_ref[...])
    y = y * lax.rsqrt(jnp.mean(y * y, -1, keepdims=True) + RMS_EPS) * nw_ref[...]
    o_ref[...] = y.astype(BF16)


def _ssd_post(y, xa, proj, rb0, sp, dst):
    ns = y.shape[0]
    blk = pl.BlockSpec((ns, 512), lambda g: (0, g))
    vec = pl.BlockSpec((1, 512), lambda g: (0, g))
    return pl.pallas_call(
        _ssd_post_kernel, grid=(SSD_GROUPS,),
        in_specs=[blk, blk, pl.BlockSpec((ns, 512), lambda g: (rb0, _C_Z // 512 + g)), vec, vec,
                  pl.BlockSpec(memory_space=pl.ANY)],
        out_specs=pl.BlockSpec((ns, 512), lambda g: (rb0, g)),
        out_shape=jax.ShapeDtypeStruct(dst.shape, BF16),
        input_output_aliases={5: 0},
        compiler_params=_cp(("parallel",)), name="ssd_sample_post",
    )(y, xa, proj, sp["dsk"], sp["nw"], dst)


def _wkv_pre_kernel(r_ref, k_ref, v_ref, wd_ref, ad_ref, sr_ref, sk_ref, sv_ref, swd_ref, sad_ref,
                    mur_ref, muk_ref, muv_ref, muwd_ref, muad_ref,
                    w0_ref, wup_ref, a0_ref, aup_ref, kk_ref, ka_ref, seg_ref,
                    ro_ref, ko_ref, vo_ref, rt_ref, wt_ref, kt_ref, bt_ref, kkt_ref, vt_ref):
    mix = lambda x_ref, s_ref, mu_ref: x_ref[...] + (s_ref[...] - x_ref[...]) * mu_ref[...]
    rm = mix(r_ref, sr_ref, mur_ref)
    km = mix(k_ref, sk_ref, muk_ref)
    vm = mix(v_ref, sv_ref, muv_ref)
    wdm = mix(wd_ref, swd_ref, muwd_ref)
    adm = mix(ad_ref, sad_ref, muad_ref)
    logw, kk, k2, bv = _rwkv_mix(rm, km, vm, wdm, adm, w0_ref[...], wup_ref[...], a0_ref[...], aup_ref[...],
                                 kk_ref[...], ka_ref[...], seg_ref[...])
    ro_ref[...] = rm
    ko_ref[...] = k2
    vo_ref[...] = vm
    rt_ref[...] = rm.T
    wt_ref[...] = jnp.exp(logw).T
    kt_ref[...] = k2.T
    bt_ref[...] = bv.T
    kkt_ref[...] = kk.T
    vt_ref[...] = vm.T


def _wkv_pre(proj, rb0, shift, rp):
    ns = shift.shape[0]
    w512 = HB * 64
    col = lambda c0: pl.BlockSpec((ns, w512), lambda h: (rb0, c0 // w512 + h))
    lora = lambda c0: pl.BlockSpec((ns, LORA), lambda h: (rb0, c0 // LORA))
    scol = lambda c0: pl.BlockSpec((ns, w512), lambda h: (0, c0 // w512 + h))
    slora = lambda c0: pl.BlockSpec((ns, LORA), lambda h: (0, c0 // LORA))
    vec = pl.BlockSpec((1, w512), lambda h: (0, h))
    vec128 = pl.BlockSpec((1, LORA), lambda h: (0, 0))
    up = pl.BlockSpec((LORA, w512), lambda h: (0, h))
    row = pl.BlockSpec((ns, w512), lambda h: (0, h))
    tr = pl.BlockSpec((w512, ns), lambda h: (h, 0))
    return pl.pallas_call(
        _wkv_pre_kernel, grid=(RWKV_HEADS // HB,),
        in_specs=[col(_C_R), col(_C_K), col(_C_V), lora(_C_WD), lora(_C_AD),
                  scol(0), scol(4096), scol(8192), slora(12288), slora(12416),
                  vec, vec, vec, vec128, vec128, vec, up, vec, up, vec, vec,
                  pl.BlockSpec((LANE, LANE), lambda h: (0, 0))],
        out_specs=[row] * 3 + [tr] * 6,
        out_shape=[jax.ShapeDtypeStruct((ns, RWKV_WIDTH), F32)] * 3
        + [jax.ShapeDtypeStruct((RWKV_WIDTH, ns), F32)] * 6,
        compiler_params=_cp(("parallel",)), name="wkv_sample_pre",
    )(proj, proj, proj, proj, proj, shift, shift, shift, shift, shift,
      rp["mur"], rp["muk"], rp["muv"], rp["muwd"], rp["muad"],
      rp["w0"], rp["wup"], rp["a0"], rp["aup"], rp["kk"], rp["ka"], rp["seg"])


def _wkv_post_kernel(o_ref, r_ref, k_ref, v_ref, g_ref, lnw_ref, lnb_ref, rk_ref, seg_ref, dst_ref, y_ref):
    del dst_ref
    y_ref[...] = _rwkv_out(o_ref[...].T, r_ref[...], k_ref[...], v_ref[...], g_ref[...],
                           lnw_ref[...], lnb_ref[...], rk_ref[...], seg_ref[...])


def _wkv_post(o_t, r, k2, v, proj, rb0, rp, dst):
    ns = o_t.shape[1]
    w512 = HB * 64
    row = pl.BlockSpec((ns, w512), lambda h: (0, h))
    vec = pl.BlockSpec((1, w512), lambda h: (0, h))
    return pl.pallas_call(
        _wkv_post_kernel, grid=(RWKV_HEADS // HB,),
        in_specs=[pl.BlockSpec((w512, ns), lambda h: (h, 0)), row, row, row,
                  pl.BlockSpec((pl.Element(ns), pl.Element(w512)), lambda h: (rb0 * ns, pl.multiple_of(_C_G + h * w512, LANE))),
                  vec, vec, vec, pl.BlockSpec((LANE, LANE), lambda h: (0, 0)),
                  pl.BlockSpec(memory_space=pl.ANY)],
        out_specs=pl.BlockSpec((ns, w512), lambda h: (rb0, h)),
        out_shape=jax.ShapeDtypeStruct(dst.shape, BF16),
        input_output_aliases={9: 0},
        compiler_params=_cp(("parallel",)), name="wkv_sample_post",
    )(o_t, r, k2, v, proj, rp["lnw"], rp["lnb"], rp["rk"], rp["seg"], dst)


def _ret_pre_kernel(q_ref, k_ref, cos_ref, sin_ref, qo_ref, ko_ref):
    cos = cos_ref[0:1, :]
    sin = sin_ref[0:1, :]
    qo_ref[...] = _rotate(q_ref[...].astype(F32), cos, sin) * (RET_QK_DIM ** -0.5)
    ko_ref[...] = _rotate(k_ref[...].astype(F32), cos, sin)


def _ret_pre(proj, rb0, ns, cos, sin):
    blk = pl.BlockSpec((ns, RET_QK_DIM), lambda h: (0, h))
    return pl.pallas_call(
        _ret_pre_kernel, grid=(RET_HEADS,),
        in_specs=[pl.BlockSpec((ns, RET_QK_DIM), lambda h: (rb0, h)),
                  pl.BlockSpec((ns, RET_QK_DIM), lambda h: (rb0, RET_HEADS + h)),
                  pl.BlockSpec((8, 128), lambda h: (0, 0)), pl.BlockSpec((8, 128), lambda h: (0, 0))],
        out_specs=[blk, blk],
        out_shape=[jax.ShapeDtypeStruct((ns, RET_QK_WIDTH), F32)] * 2,
        compiler_params=_cp(("parallel",)), name="ret_sample_pre",
    )(proj, proj, cos, sin)


def _ret_post_kernel(y_ref, g_ref, gnw_ref, dst_ref, o_ref):
    del dst_ref
    y = y_ref[...]
    mu = jnp.mean(y, -1, keepdims=True)
    d = y - mu
    var = jnp.mean(d * d, -1, keepdims=True)
    o = d * lax.rsqrt(var + RET_GN_EPS) * gnw_ref[...]
    o_ref[...] = (o * _silu(g_ref[...].astype(F32))).astype(BF16)


def _ret_post(y, proj, rb0, gn_w, dst):
    ns = y.shape[0]
    blk = pl.BlockSpec((ns, RET_V_DIM), lambda h: (0, h))
    return pl.pallas_call(
        _ret_post_kernel, grid=(RET_HEADS,),
        in_specs=[blk, pl.BlockSpec((ns, RET_V_DIM), lambda h: (rb0, (2 * RET_QK_WIDTH + RET_WIDTH) // RET_V_DIM + h)),
                  pl.BlockSpec((1, RET_V_DIM), lambda h: (0, h)), pl.BlockSpec(memory_space=pl.ANY)],
        out_specs=pl.BlockSpec((ns, RET_V_DIM), lambda h: (rb0, h)),
        out_shape=jax.ShapeDtypeStruct(dst.shape, BF16),
        input_output_aliases={3: 0},
        compiler_params=_cp(("parallel",)), name="ret_sample_post",
    )(y, proj, gn_w.reshape(1, -1), dst)


def _ab_layer(xp, xs, xb, nb, l, ns, conv_s, ssm_s, shift_s, wkv_s, w_in, sp, rp, w_out, ln_w, ln_b,
              w_next_in, w_next_out):
    mp = nb * l
    rb0 = mp // 128
    wt = w_in.T
    wt_ssd = wt[:AB_SSD_W + LANE].astype(BF16)
    tiles = lambda w: [j * AB_TN for j in range(w // AB_TN)]
    proj_a, wt_rwkv = _matmul_wt(xb, wt_ssd, tiles(AB_SSD_W), AB_TN, "ab_in_proj_ssd",
                                 ride=(wt, AB_SSD_W + SSD_HEADS, AB_RWKV_W))
    pdt = _matmul_wt(xb, wt_ssd, [AB_SSD_W], LANE, "ab_dt_proj")[:, :SSD_HEADS]
    proj_b = _matmul_wt(xb, wt_rwkv, tiles(AB_RWKV_W), AB_TN, "ab_in_proj_rwkv")
    m = proj_a.shape[0]
    dt3 = pdt.reshape(m, SSD_GROUPS, SSD_HPG)
    dtc = dt3.transpose(1, 0, 2)
    dtr = dt3.transpose(1, 2, 0)

    ya, ssm_p, w_out_b, w_next_out_b = _ssd_prompt(proj_a, dtc, dtr, nb, l, sp, w_out, w_next_out)
    r_s, k_s, v_s, r_t, w_t, k_t, b_t, kk_t, v_t = _wkv_pre(proj_b, rb0, shift_s.reshape(ns, SHIFT_DIM), rp)
    yb, wkv_p, w_next_in_b, wkv_t, o_t = _rwkv_prompt(proj_b, nb, l, rp, w_next_in, wkv_s.transpose(1, 2, 3, 0),
                                                      (r_t, w_t, k_t, b_t, kk_t), v_t)
    wkv_n = wkv_t.transpose(3, 0, 1, 2)
    tail = lambda p, n, c0, c1: jnp.stack([p[(b + 1) * l - n:(b + 1) * l, c0:c1] for b in range(nb)])
    conv_p = tail(proj_a, SSD_CONV - 1, _C_XS, AB_SSD_W)
    shift_p = tail(proj_b, 1, _C_R, _C_G)

    xa, ba, ca, dt_s, dec_s = _ssd_pre(proj_a, rb0, ns, conv_s.transpose(1, 0, 2), dtc, sp)
    flat = lambda t: t.transpose(1, 0, 2).reshape(ns * SSD_HEADS)
    ssm_n, y_s = _ssm_state(flat(dt_s), flat(dec_s), ssm_s, xa.reshape(ns, 1, SSD_WIDTH),
                            ba.transpose(1, 0, 2), ca.transpose(1, 0, 2))
    ya = _ssd_post(y_s.reshape(ns, SSD_WIDTH), xa, proj_a, rb0, sp, ya)
    conv_n = jnp.concatenate([conv_s[:, 1:], proj_a[mp:, None, _C_XS:AB_SSD_W]], axis=1)

    yb = _wkv_post(o_t, r_s, k_s, v_s, proj_b, rb0, rp, yb)
    shift_n = proj_b[mp:, None, _C_R:_C_G]

    out = _matmul2(ya, yb, w_out_b, "ab_out_proj")
    x_new, xb_new = _deepnorm_first(xp, xs, out, ln_w, ln_b, "ab_deepnorm")
    return (x_new, xb_new, (conv_p, ssm_p, shift_p, wkv_p), (conv_n, ssm_n, shift_n, wkv_n),
            w_next_in_b, w_next_out_b)


def _ret_layer(x, xb, nb, l, ns, ret_s, w_in, gn_w, w_out, ln_w, ln_b):
    mp = nb * l
    rb0 = mp // 128
    proj = _matmul(xb, w_in, "ret_in_proj", BF16)
    cos, sin = _trig(jnp.arange(l))
    cos_s, sin_s = _trig(jnp.full((8,), PAST_LEN))
    q_s, k_s = _ret_pre(proj, rb0, ns, cos_s, sin_s)
    v_s = proj[mp:, 2 * RET_QK_WIDTH:2 * RET_QK_WIDTH + RET_WIDTH].astype(F32).reshape(ns, RET_HEADS, RET_V_DIM)
    h3 = lambda t: t.reshape(ns, RET_HEADS, RET_QK_DIM)
    y, ret_p, ret_n, o_s = _ret_prompt(proj, cos, sin, gn_w, nb, l, ret_s, h3(q_s), h3(k_s), v_s)
    y = _ret_post(o_s.reshape(ns, RET_WIDTH), proj, rb0, gn_w, y)

    out = _matmul(y, w_out, "ret_out_proj")
    y_p, y_s = _deepnorm_last(x, out, ln_w, ln_b, ns, "ret_deepnorm")
    return y_p, y_s, ret_p, ret_n


def kernel(x_prompt, x_sample, state_conv, state_ssm, state_shift, state_wkv, state_ret, ab_w_in, ssd_conv_w, ssd_conv_b, ssd_dt_bias, ssd_a_log, ssd_d, ssd_norm_w, rwkv_mu, rwkv_w0, rwkv_w_up, rwkv_a0, rwkv_a_up, rwkv_k_k, rwkv_k_a, rwkv_r_k, rwkv_lnx_w, rwkv_lnx_b, ab_w_out, ab_ln_w, ab_ln_b, ret_w_in, ret_gn_w, ret_w_out, ret_ln_w, ret_ln_b):
    nb, l, d = x_prompt.shape
    ns = x_sample.shape[0]
    assert x_sample.shape[1] == 1 and l % CHUNK == 0 and ns % LANE == 0 and ns == LANE
    mp = nb * l
    xp, xs = x_prompt.reshape(mp, d), x_sample.reshape(ns, d)
    xb = jnp.concatenate([xp.astype(BF16), xs.astype(BF16)], axis=0)

    sp = _ssd_params(ssd_conv_w[0], ssd_conv_b[0], ssd_dt_bias[0], ssd_a_log[0], ssd_d[0], ssd_norm_w[0])
    rp = _rwkv_params(rwkv_mu[0], rwkv_w0[0], rwkv_w_up[0], rwkv_a0[0], rwkv_a_up[0], rwkv_k_k[0], rwkv_k_a[0],
                      rwkv_r_k[0], rwkv_lnx_w[0], rwkv_lnx_b[0])
    x, xb, pst, sst, ret_w_in_b, ret_w_out_b = _ab_layer(
        xp, xs, xb, nb, l, ns, state_conv[0], state_ssm[0], state_shift[0], state_wkv[0],
        ab_w_in[0], sp, rp, ab_w_out[0], ab_ln_w[0], ab_ln_b[0], ret_w_in[0], ret_w_out[0])
    y_p, y_s, ret_p, ret_n = _ret_layer(x, xb, nb, l, ns, state_ret[0], ret_w_in_b, ret_gn_w[0], ret_w_out_b,
                                        ret_ln_w[0], ret_ln_b[0])
    y_prompt = y_p.reshape(nb, l, d)
    y_sample = y_s.reshape(ns, 1, d)
    st = lambda t: t[None]
    return (y_prompt, y_sample,
            st(pst[0]), st(pst[1]), st(pst[2]), st(pst[3]), st(ret_p),
            st(sst[0]), st(sst[1]), st(sst[2]), st(sst[3]), st(ret_n))
```

```python
import functools
import math

import jax
import jax.numpy as jnp
import numpy as np
from jax import lax
from jax.experimental import pallas as pl
from jax.experimental.pallas import tpu as pltpu

F32 = jnp.float32
BF16 = jnp.bfloat16

D_MODEL = 4096
DEPTH = 2
PAST_LEN = 16384
SSD_WIDTH = 4096
SSD_HEAD_DIM = 64
SSD_HEADS = 64
SSD_GROUPS = 8
SSD_HPG = 8
SSD_STATE = 128
SSD_CONV = 4
SSD_CONV_DIM = SSD_WIDTH + 2 * SSD_GROUPS * SSD_STATE
RWKV_WIDTH = 4096
RWKV_HEAD_DIM = 64
RWKV_HEADS = 64
LORA = 128
SHIFT_DIM = 3 * RWKV_WIDTH + 2 * LORA
RET_HEADS = 16
RET_QK_DIM = 256
RET_V_DIM = 512
RET_QK_WIDTH = 4096
RET_WIDTH = 8192
ROPE_BASE = 10000.0
CHUNK = 128
ALPHA = (2 * DEPTH) ** 0.25
LN_EPS = 1e-5
RMS_EPS = 1e-5
RWKV_GN_EPS = 64e-5
RET_GN_EPS = 1e-6

LANE = 128
VMEM_LIMIT = 56 * 1024 * 1024
WKV_CHUNK = 64
RW_ROWS = 512
HB = 8
RET_HB = 4
SSM_BT = 2
SSD_CPS = 4
WKV_UNROLL = 8

_C_Z, _C_XS, _C_B, _C_C = 0, 4096, 8192, 9216
_C_R, _C_K, _C_V, _C_WD, _C_AD, _C_G = 0, 4096, 8192, 12288, 12416, 12544
AB_SSD_W = 10240
AB_RWKV_W = 16640
AB_TN = 1280


def _cp(sem):
    return pltpu.CompilerParams(dimension_semantics=sem, vmem_limit_bytes=VMEM_LIMIT)


def _silu(x):
    return x * jax.nn.sigmoid(x)


def _softplus(x):
    return jnp.maximum(x, 0.0) + jnp.log1p(jnp.exp(-jnp.abs(x)))


def _dot(a, b):
    return jnp.dot(a.astype(BF16), b.astype(BF16), preferred_element_type=F32)


def _dot_nt(a, b):
    return lax.dot_general(a.astype(BF16), b.astype(BF16), (((1,), (1,)), ((), ())),
                           preferred_element_type=F32)


def _dot_tn(a, b):
    return lax.dot_general(a.astype(BF16), b.astype(BF16), (((0,), (0,)), ((), ())),
                           preferred_element_type=F32)


def _split(x, n):
    parts, r = [], x
    for _ in range(n):
        h = r.astype(BF16)
        parts.append(h)
        r = r - h.astype(F32)
    return parts


def _dot01(m01, x, n=3):
    return sum(jnp.dot(m01, p, preferred_element_type=F32) for p in _split(x, n))


def _dot01_r(x, m01, n=2):
    return sum(jnp.dot(p, m01, preferred_element_type=F32) for p in _split(x, n))


def _segsum(x, seg):
    r, w = x.shape
    nt = w // LANE
    tall = jnp.concatenate([x[:, i * LANE:(i + 1) * LANE] for i in range(nt)], axis=0)
    s = _dot01_r(tall, seg)
    return jnp.concatenate([s[i * r:(i + 1) * r] for i in range(nt)], axis=1)


def _onehot_cols(b, n):
    rows = lax.broadcasted_iota(jnp.int32, (LANE, n), 0)
    return jnp.where(rows == b, 1.0, 0.0).astype(BF16)


def _mm_kernel(x_ref, w_ref, o_ref):
    o_ref[...] = jnp.dot(x_ref[...], w_ref[...], preferred_element_type=F32).astype(o_ref.dtype)


def _pick_tile(n, prefs):
    for t in prefs:
        if n % t == 0:
            return t
    return n


def _matmul(x, w, name, out_dtype=F32):
    m, k = x.shape
    n = w.shape[1]
    tm = _pick_tile(m, (640, 512, 256, 128))
    tn = _pick_tile(n, (1280, 1024, 512, 256, 128) if k <= 4096 else (512, 256, 128))
    return pl.pallas_call(
        _mm_kernel,
        grid=(n // tn, m // tm),
        in_specs=[pl.BlockSpec((tm, k), lambda j, i: (i, 0)),
                  pl.BlockSpec((k, tn), lambda j, i: (0, j))],
        out_specs=pl.BlockSpec((tm, tn), lambda j, i: (i, j)),
        out_shape=jax.ShapeDtypeStruct((m, n), out_dtype),
        compiler_params=_cp(("parallel", "parallel")),
        name=name,
    )(x, w)


def _mm_wt_kernel(st_ref, x_ref, wt_ref, o_ref):
    del st_ref
    o_ref[...] = lax.dot_general(x_ref[...], wt_ref[...], (((1,), (1,)), ((), ())), preferred_element_type=F32)


def _mm_wt_ride_kernel(st_ref, x_ref, wt_ref, wi_ref, o_ref, wo_ref):
    _mm_wt_kernel(st_ref, x_ref, wt_ref, o_ref)
    wo_ref[...] = wi_ref[...].astype(BF16)


def _matmul_wt(x, wt, row_starts, tn, name, ride=None):
    m, k = x.shape
    nt = len(row_starts)
    tm = _pick_tile(m, (640, 512, 256, 128))
    nm = m // tm
    starts = jnp.asarray(row_starts, jnp.int32)
    in_specs = [pl.BlockSpec((tm, k), lambda j, i, st: (i, 0)),
                pl.BlockSpec((pl.Element(tn), pl.Element(k)), lambda j, i, st: (pl.multiple_of(st[j], 64), 0))]
    out_specs = pl.BlockSpec((tm, tn), lambda j, i, st: (i, j))
    out_shape = jax.ShapeDtypeStruct((m, nt * tn), F32)
    args = (starts, x, wt)
    body = _mm_wt_kernel
    if ride is not None:
        w_f32, row0, nrows = ride
        rps = nrows // (nt * nm)
        assert rps * nt * nm == nrows and rps % 16 == 0 and row0 % 8 == 0
        in_specs.append(pl.BlockSpec((pl.Element(rps), pl.Element(w_f32.shape[1])),
                                     lambda j, i, st: (pl.multiple_of(row0 + (j * nm + i) * rps, 8), 0)))
        out_specs = [out_specs, pl.BlockSpec((rps, w_f32.shape[1]), lambda j, i, st: (j * nm + i, 0))]
        out_shape = [out_shape, jax.ShapeDtypeStruct((nrows, w_f32.shape[1]), BF16)]
        args = args + (w_f32,)
        body = _mm_wt_ride_kernel
    grid_spec = pltpu.PrefetchScalarGridSpec(num_scalar_prefetch=1, grid=(nt, nm),
                                             in_specs=in_specs, out_specs=out_specs)
    return pl.pallas_call(
        body, grid_spec=grid_spec, out_shape=out_shape,
        compiler_params=_cp(("parallel", "parallel")),
        name=name,
    )(*args)


def _mm2_kernel(a_ref, b_ref, w_ref, o_ref):
    ka = a_ref.shape[1]
    o_ref[...] = (jnp.dot(a_ref[...], w_ref[0:ka, :], preferred_element_type=F32)
                  + jnp.dot(b_ref[...], w_ref[ka:, :], preferred_element_type=F32))


def _matmul2(a, b, w, name):
    m, ka = a.shape
    kb = b.shape[1]
    n = w.shape[1]
    tm = _pick_tile(m, (640, 512, 256, 128))
    tn = _pick_tile(n, (512, 256, 128))
    return pl.pallas_call(
        _mm2_kernel,
        grid=(n // tn, m // tm),
        in_specs=[pl.BlockSpec((tm, ka), lambda j, i: (i, 0)),
                  pl.BlockSpec((tm, kb), lambda j, i: (i, 0)),
                  pl.BlockSpec((ka + kb, tn), lambda j, i: (0, j))],
        out_specs=pl.BlockSpec((tm, tn), lambda j, i: (i, j)),
        out_shape=jax.ShapeDtypeStruct((m, n), F32),
        compiler_params=_cp(("parallel", "parallel")),
        name=name,
    )(a, b, w)


def _post_norm(x, o, w, b):
    h = ALPHA * x + o
    mu = jnp.mean(h, -1, keepdims=True)
    d = h - mu
    var = jnp.mean(d * d, -1, keepdims=True)
    return d * lax.rsqrt(var + LN_EPS) * w + b


def _ln_first_kernel(xp_ref, xs_ref, o_ref, w_ref, b_ref, y_ref, yb_ref):
    is_sample = pl.program_id(0) == pl.num_programs(0) - 1
    x = jnp.where(is_sample, xs_ref[...], xp_ref[...])
    y = _post_norm(x, o_ref[...], w_ref[...], b_ref[...])
    y_ref[...] = y
    yb_ref[...] = y.astype(BF16)


def _deepnorm_first(xp, xs, o, w, b, name):
    mp, d = xp.shape
    ns = xs.shape[0]
    npt = mp // ns
    row = pl.BlockSpec((ns, d), lambda i: (i, 0))
    vec = pl.BlockSpec((1, d), lambda i: (0, 0))
    return pl.pallas_call(
        _ln_first_kernel,
        grid=(npt + 1,),
        in_specs=[pl.BlockSpec((ns, d), lambda i: (jnp.minimum(i, npt - 1), 0)),
                  pl.BlockSpec((ns, d), lambda i: (0, 0)), row, vec, vec],
        out_specs=[row, row],
        out_shape=[jax.ShapeDtypeStruct((mp + ns, d), F32), jax.ShapeDtypeStruct((mp + ns, d), BF16)],
        compiler_params=_cp(("parallel",)),
        name=name,
    )(xp, xs, o, w.reshape(1, d), b.reshape(1, d))


def _ln_last_kernel(x_ref, o_ref, w_ref, b_ref, yp_ref, ys_ref):
    is_sample = pl.program_id(0) == pl.num_programs(0) - 1
    y = _post_norm(x_ref[...], o_ref[...], w_ref[...], b_ref[...])

    @pl.when(jnp.logical_not(is_sample))
    def _prompt():
        yp_ref[...] = y

    @pl.when(is_sample)
    def _sample():
        ys_ref[...] = y


def _deepnorm_last(x, o, w, b, ns, name):
    m, d = x.shape
    npt = m // ns - 1
    row = pl.BlockSpec((ns, d), lambda i: (i, 0))
    vec = pl.BlockSpec((1, d), lambda i: (0, 0))
    return pl.pallas_call(
        _ln_last_kernel,
        grid=(npt + 1,),
        in_specs=[row, row, vec, vec],
        out_specs=[pl.BlockSpec((ns, d), lambda i: (jnp.minimum(i, npt - 1), 0)),
                   pl.BlockSpec((ns, d), lambda i: (0, 0))],
        out_shape=[jax.ShapeDtypeStruct((npt * ns, d), F32), jax.ShapeDtypeStruct((ns, d), F32)],
        compiler_params=_cp(("arbitrary",)),
        name=name,
    )(x, o, w.reshape(1, d), b.reshape(1, d))


def _ssd_prompt_kernel(z_ref, xs_ref, b_ref, c_ref, dtc_ref, dtr_ref,
                       cwx_ref, cwb_ref, cwc_ref, cbx_ref, cbb_ref, cbc_ref,
                       dtbc_ref, dtbr_ref, alc_ref, alr_ref, dsk_ref, nw_ref, tri_ref, rep64_ref, rep128_ref,
                       wi_ref, wi2_ref, y_ref, s_ref, wo_ref, wo2_ref, bufx, bufb, bufc):
    L = CHUNK
    c = pl.program_id(2)
    wo_ref[...] = wi_ref[...].astype(BF16)
    wo2_ref[...] = wi2_ref[...].astype(BF16)

    @pl.when(c == 0)
    def _init():
        for buf in (bufx, bufb, bufc):
            buf[0:8, :] = jnp.zeros((8, buf.shape[1]), F32)
        s_ref[...] = jnp.zeros(s_ref.shape, F32)

    tri = tri_ref[...]
    li = lax.broadcasted_iota(jnp.int32, (L, L), 0)
    si = lax.broadcasted_iota(jnp.int32, (L, L), 1)
    causal = li >= si
    lo = lax.broadcasted_iota(jnp.int32, (1, LANE), 1) < 64
    pairs = range(SSD_HPG // 2)
    tile = lambda a, p: a[:, p * LANE:(p + 1) * LANE]
    for ci in range(SSD_CPS):
        rows = slice(ci * L, (ci + 1) * L)

        def conv(u_ref, buf, w_ref, bias_ref):
            buf[8:8 + L, :] = u_ref[rows, :]
            acc = bias_ref[...] + buf[5:5 + L, :] * w_ref[0:1, :]
            for k in range(1, SSD_CONV):
                acc = acc + buf[5 + k:5 + k + L, :] * w_ref[k:k + 1, :]
            buf[0:8, :] = buf[L:L + 8, :]
            return _silu(acc)

        xs = conv(xs_ref, bufx, cwx_ref, cbx_ref)
        bm = conv(b_ref, bufb, cwb_ref, cbb_ref)
        cm = conv(c_ref, bufc, cwc_ref, cbc_ref)
        dtc = _softplus(dtc_ref[0, rows, :] + dtbc_ref[0])
        dtr = _softplus(dtr_ref[0, :, rows] + dtbr_ref[0])
        adt_c = dtc * (-jnp.exp(alc_ref[0]))
        adt_r = dtr * (-jnp.exp(alr_ref[0]))
        cum_c = _dot01(tri, adt_c)
        cum_r = sum(lax.dot_general(p, tri, (((1,), (1,)), ((), ())), preferred_element_type=F32)
                    for p in _split(adt_r, 3))
        cb = _dot_nt(cm, bm)
        dt_x = _dot01_r(dtc, rep64_ref[...], 3)
        cum_x = _dot01_r(cum_c, rep64_ref[...], 3)
        cum_b = _dot01_r(cum_c, rep128_ref[...], 3)
        xdt = xs * dt_x
        xdt_tail = xdt * jnp.exp(cum_x[L - 1:L, :] - cum_x)
        s_old = [s_ref[0, r] for r in range(SSD_HPG)]
        decay = [jnp.exp(jnp.where(causal, tile(cum_b, r) - cum_r[r:r + 1, :], -jnp.inf))
                 for r in range(SSD_HPG)]
        x_lo = [jnp.where(lo, tile(xdt, p), 0.0) for p in pairs]
        x_hi = [jnp.where(lo, 0.0, tile(xdt, p)) for p in pairs]
        y_in = [_dot(cb * decay[2 * p], x_lo[p]) + _dot(cb * decay[2 * p + 1], x_hi[p]) for p in pairs]
        y_st = [_dot_nt(cm, jnp.concatenate([s_old[2 * p], s_old[2 * p + 1]], 0)) for p in pairs]
        s_in = [_dot_tn(tile(xdt_tail, p), bm) for p in pairs]
        for r in range(SSD_HPG):
            half = s_in[r // 2][(r % 2) * 64:(r % 2 + 1) * 64]
            s_ref[0, r] = s_old[r] * jnp.exp(cum_c[L - 1:L, r:r + 1]) + half
        y = jnp.concatenate(y_in, axis=1) + jnp.concatenate(y_st, axis=1) * jnp.exp(cum_x) + xs * dsk_ref[...]
        y = y * _silu(z_ref[rows, :])
        y = y * lax.rsqrt(jnp.mean(y * y, -1, keepdims=True) + RMS_EPS) * nw_ref[...]
        y_ref[rows, :] = y.astype(BF16)


def _ssd_params(conv_w, conv_b, dt_bias, a_log, d_skip, norm_w):
    g = SSD_GROUPS
    return dict(
        cwx=conv_w[:, :4096], cwb=conv_w[:, 4096:5120], cwc=conv_w[:, 5120:],
        cbx=conv_b[:4096].reshape(1, -1), cbb=conv_b[4096:5120].reshape(1, -1), cbc=conv_b[5120:].reshape(1, -1),
        dtbc=dt_bias.reshape(g, 1, 8), dtbr=dt_bias.reshape(g, 8, 1),
        alc=a_log.reshape(g, 1, 8), alr=a_log.reshape(g, 8, 1),
        dsk=jnp.repeat(d_skip, SSD_HEAD_DIM).reshape(1, -1), nw=norm_w.reshape(1, -1))


def _ride_specs(w, nsteps, lin):
    r, n = w.shape
    assert r % nsteps == 0 and (r // nsteps) % 16 == 0
    blk = pl.BlockSpec((r // nsteps, n), lambda *idx: (lin(*idx), 0))
    return blk, blk, jax.ShapeDtypeStruct((r, n), BF16)


def _ssd_prompt(proj, dtc, dtr, nb, l, sp, w_ride, w_ride2):
    rows = CHUNK * SSD_CPS
    nc = l // rows
    rb = lambda b, g, c, *_: b * nc + c
    tri = jnp.tril(jnp.ones((CHUNK, CHUNK), BF16))
    lin = lambda b, g, c, *_: (b * SSD_GROUPS + g) * nc + c
    nsteps = nb * SSD_GROUPS * nc
    wi_spec, wo_spec, wo_shape = _ride_specs(w_ride, nsteps, lin)
    wi2_spec, wo2_spec, wo2_shape = _ride_specs(w_ride2, nsteps, lin)
    in_specs = [
        pl.BlockSpec((rows, 512), lambda b, g, c, *_: (rb(b, g, c), _C_Z // 512 + g)),
        pl.BlockSpec((rows, 512), lambda b, g, c, *_: (rb(b, g, c), _C_XS // 512 + g)),
        pl.BlockSpec((rows, 128), lambda b, g, c, *_: (rb(b, g, c), _C_B // 128 + g)),
        pl.BlockSpec((rows, 128), lambda b, g, c, *_: (rb(b, g, c), _C_C // 128 + g)),
        pl.BlockSpec((1, rows, 8), lambda b, g, c, *_: (g, rb(b, g, c), 0)),
        pl.BlockSpec((1, 8, rows), lambda b, g, c, *_: (g, 0, rb(b, g, c))),
        pl.BlockSpec((SSD_CONV, 512), lambda b, g, c, *_: (0, g)),
        pl.BlockSpec((SSD_CONV, 128), lambda b, g, c, *_: (0, g)),
        pl.BlockSpec((SSD_CONV, 128), lambda b, g, c, *_: (0, g)),
        pl.BlockSpec((1, 512), lambda b, g, c, *_: (0, g)),
        pl.BlockSpec((1, 128), lambda b, g, c, *_: (0, g)),
        pl.BlockSpec((1, 128), lambda b, g, c, *_: (0, g)),
        pl.BlockSpec((1, 1, 8), lambda b, g, c, *_: (g, 0, 0)),
        pl.BlockSpec((1, 8, 1), lambda b, g, c, *_: (g, 0, 0)),
        pl.BlockSpec((1, 1, 8), lambda b, g, c, *_: (g, 0, 0)),
        pl.BlockSpec((1, 8, 1), lambda b, g, c, *_: (g, 0, 0)),
        pl.BlockSpec((1, 512), lambda b, g, c, *_: (0, g)),
        pl.BlockSpec((1, 512), lambda b, g, c, *_: (0, g)),
        pl.BlockSpec((CHUNK, CHUNK), lambda b, g, c, *_: (0, 0)),
        pl.BlockSpec((SSD_HPG, SSD_HPG * 64), lambda b, g, c, *_: (0, 0)),
        pl.BlockSpec((SSD_HPG, SSD_HPG * LANE), lambda b, g, c, *_: (0, 0)),
        wi_spec, wi2_spec,
    ]
    rep64 = jnp.asarray(np.kron(np.eye(SSD_HPG), np.ones((1, 64))), BF16)
    rep128 = jnp.asarray(np.kron(np.eye(SSD_HPG), np.ones((1, LANE))), BF16)
    out_specs = [pl.BlockSpec((rows, 512), lambda b, g, c, *_: (rb(b, g, c), g)),
                 pl.BlockSpec((1, SSD_HPG, SSD_HEAD_DIM, SSD_STATE), lambda b, g, c, *_: (b, g, 0, 0)),
                 wo_spec, wo2_spec]
    return pl.pallas_call(
        _ssd_prompt_kernel,
        grid=(nb, SSD_GROUPS, nc),
        in_specs=in_specs,
        out_specs=out_specs,
        out_shape=[jax.ShapeDtypeStruct((proj.shape[0], SSD_WIDTH), BF16),
                   jax.ShapeDtypeStruct((nb, SSD_HEADS, SSD_HEAD_DIM, SSD_STATE), F32),
                   wo_shape, wo2_shape],
        scratch_shapes=[pltpu.VMEM((CHUNK + 8, 512), F32), pltpu.VMEM((CHUNK + 8, 128), F32),
                        pltpu.VMEM((CHUNK + 8, 128), F32)],
        compiler_params=_cp(("parallel", "parallel", "arbitrary")),
        name="ssd_prompt",
    )(proj, proj, proj, proj, dtc, dtr, sp["cwx"], sp["cwb"], sp["cwc"], sp["cbx"], sp["cbb"], sp["cbc"],
      sp["dtbc"], sp["dtbr"], sp["alc"], sp["alr"], sp["dsk"], sp["nw"], tri, rep64, rep128, w_ride, w_ride2)


def _rwkv_mix(rm, km, vm, wdm, adm, w0, wup, a0, aup, k_k, k_a, seg):
    wlog = -_softplus(-(w0 + _dot(jnp.tanh(wdm), wup))) - 0.5
    logw = -jnp.exp(wlog)
    aa = jax.nn.sigmoid(a0 + _dot(adm, aup))
    kkr = km * k_k
    kk = kkr * lax.rsqrt(jnp.maximum(_segsum(kkr * kkr, seg), 1e-24))
    k2 = km * (1.0 + (aa - 1.0) * k_a)
    return logw, kk, k2, kk * aa


def _rwkv_out(o, rm, k2, vm, g, lnw, lnb, rk, seg):
    inv = 1.0 / RWKV_HEAD_DIM
    mean = _segsum(o, seg) * inv
    d = o - mean
    var = _segsum(d * d, seg) * inv
    on = d * lax.rsqrt(var + RWKV_GN_EPS) * lnw + lnb
    bonus = _segsum(rm * k2 * rk, seg) * vm
    return ((on + bonus) * _silu(g)).astype(BF16)


def _wkv_decode(s_ref, r_ref, w_ref, k_ref, b_ref, kk_ref, v_ref, so_ref, y_ref):
    nh, nv = s_ref.shape[0], s_ref.shape[1]
    for hh in range(nh):
        ch = slice(hh * 64, (hh + 1) * 64)
        r, w, k, bv, kk = r_ref[ch, :], w_ref[ch, :], k_ref[ch, :], b_ref[ch, :], kk_ref[ch, :]

        def vrow(vi, carry):
            s = s_ref[hh, vi]
            sk = jnp.sum(s * kk, axis=0, keepdims=True)
            sn = s * w - sk * bv + v_ref[pl.ds(hh * nv + vi, 1), :] * k
            so_ref[hh, vi] = sn
            y_ref[pl.ds(hh * nv + vi, 1), :] = jnp.sum(sn * r, axis=0, keepdims=True)
            return carry

        lax.fori_loop(0, nv, vrow, 0, unroll=min(nv, WKV_UNROLL))


def _rwkv_prompt_kernel(r_ref, k_ref, v_ref, g_ref, wd_ref, ad_ref,
                        mur_ref, muk_ref, muv_ref, muwd_ref, muad_ref,
                        w0_ref, wup_ref, a0_ref, aup_ref, kk_ref, ka_ref, lnw_ref, lnb_ref, rk_ref,
                        seg_ref, tri_ref, wi_ref,
                        ds_ref, dr_ref, dw_ref, dk_ref, db_ref, dkk_ref, dv_ref,
                        y_ref, s_ref, wo_ref, dso_ref, dy_ref, cr, ck, cv, cwd, cad):
    R, C = RW_ROWS, WKV_CHUNK
    wo_ref[...] = wi_ref[...].astype(BF16)
    c = pl.program_id(2)

    @pl.when(c == 0)
    def _init():
        for buf in (cr, ck, cv, cwd, cad):
            buf[...] = jnp.zeros(buf.shape, F32)
        s_ref[...] = jnp.zeros(s_ref.shape, F32)

    row0 = lax.broadcasted_iota(jnp.int32, (R, 1), 0) == 0

    def shift(x_ref, carry, mu_ref):
        x = x_ref[...]
        prev = jnp.where(row0, carry[0:1, :], pltpu.roll(x, 1, 0))
        carry[0:1, :] = x[R - 1:R, :]
        return x + (prev - x) * mu_ref[...]

    rm = shift(r_ref, cr, mur_ref)
    km = shift(k_ref, ck, muk_ref)
    vm = shift(v_ref, cv, muv_ref)
    wdm = shift(wd_ref, cwd, muwd_ref)
    adm = shift(ad_ref, cad, muad_ref)
    seg = seg_ref[...]
    logw, kk, k2, bv = _rwkv_mix(rm, km, vm, wdm, adm, w0_ref[...], wup_ref[...], a0_ref[...], aup_ref[...],
                                 kk_ref[...], ka_ref[...], seg)

    tri = tri_ref[...]
    li = lax.broadcasted_iota(jnp.int32, (C, LANE), 0)
    lane = lax.broadcasted_iota(jnp.int32, (C, LANE), 1)
    si = lane % 64
    strict = li > si
    incl = li >= si
    eye = jnp.where(li == si, 1.0, 0.0)
    lo = lane < 64
    rlo = lax.broadcasted_iota(jnp.int32, (LANE, LANE), 0) < 64
    llo = lax.broadcasted_iota(jnp.int32, (LANE, LANE), 1) < 64
    same = rlo == llo

    def bd(a):
        ab = a.astype(BF16)
        zero = jnp.zeros_like(ab)
        return jnp.concatenate([jnp.where(lo, ab, zero), jnp.where(lo, zero, ab)], axis=0)

    nsc = R // C
    prep = []
    for sc in range(nsc):
        rows = slice(sc * C, (sc + 1) * C)
        lw = logw[rows]
        cs = _dot01(tri, lw)
        cl = cs[C - 1:C, :]
        e_tail = jnp.exp(cl - cs)
        e_neg = jnp.exp(-cs)
        prep.append(dict(
            bt=kk[rows] * jnp.exp(cs - lw),
            bb=bv[rows] * e_neg,
            kt=k2[rows] * e_neg,
            rt=rm[rows] * jnp.exp(cs),
            bh=bv[rows] * e_tail,
            kh=k2[rows] * e_tail,
            pc=jnp.exp(cl), v=vm[rows]))
    npair = HB // 2
    keys = [(sc, p) for sc in range(nsc) for p in range(npair)]
    part = lambda name: {k: prep[k[0]][name][:, k[1] * LANE:(k[1] + 1) * LANE] for k in keys}
    bt, bb, kt, rt, bh, kh, vh, pc = (part(n) for n in ("bt", "bb", "kt", "rt", "bh", "kh", "v", "pc"))
    lhs = {k: jnp.concatenate([bt[k], rt[k]], 0) for k in keys}
    gb = {k: _dot_nt(lhs[k], bd(bb[k])) for k in keys}
    gk = {k: _dot_nt(lhs[k], bd(kt[k])) for k in keys}
    lk = {k: jnp.where(strict, gk[k][0:C], 0.0) for k in keys}
    rb = {k: jnp.where(incl, gb[k][C:2 * C], 0.0) for k in keys}
    rkm = {k: jnp.where(incl, gk[k][C:2 * C], 0.0) for k in keys}
    x = {k: jnp.where(strict, -gb[k][0:C], 0.0) for k in keys}
    t = {k: eye + x[k] for k in keys}
    for _ in range(int(math.log2(C)) - 1):
        x = {k: _dot(x[k], bd(x[k])) for k in keys}
        t = {k: t[k] + _dot(t[k], bd(x[k])) for k in keys}
    bdv = {k: bd(vh[k]) for k in keys}
    lkv = {k: _dot(lk[k], bdv[k]) for k in keys}
    tb = {k: _dot(t[k], bd(bt[k])) for k in keys}
    tlv = {k: _dot(t[k], bd(lkv[k])) for k in keys}
    rq = {k: rt[k] - _dot(rb[k], bd(tb[k])) for k in keys}
    yc = {k: _dot(rkm[k], bdv[k]) - _dot(rb[k], bd(tlv[k])) for k in keys}
    mq = {k: jnp.where(same, _dot_tn(tb[k], bh[k]), 0.0).astype(BF16) for k in keys}
    nf = {k: _dot_tn(jnp.concatenate([vh[k], -tlv[k]], 0), jnp.concatenate([kh[k], bh[k]], 0)) for k in keys}
    vlo = lax.broadcasted_iota(jnp.int32, (64, LANE), 1) < 64
    nn = {k: jnp.where(vlo, nf[k][0:64], nf[k][64:128]) for k in keys}
    st = [jnp.concatenate([s_ref[0, 2 * p], s_ref[0, 2 * p + 1]], axis=1) for p in range(npair)]
    o_chunks = []
    for sc in range(nsc):
        ys = [_dot_nt(rq[sc, p], bd(st[p])) + yc[sc, p] for p in range(npair)]
        st = [st[p] * pc[sc, p] - _dot(st[p], mq[sc, p]) + nn[sc, p] for p in range(npair)]
        o_chunks.append(jnp.concatenate(ys, 1))
    for p in range(npair):
        s_ref[0, 2 * p] = st[p][:, 0:64]
        s_ref[0, 2 * p + 1] = st[p][:, 64:128]
    o = jnp.concatenate(o_chunks, 0)
    y_ref[...] = _rwkv_out(o, rm, k2, vm, g_ref[...], lnw_ref[...], lnb_ref[...], rk_ref[...], seg)
    _wkv_decode(ds_ref, dr_ref, dw_ref, dk_ref, db_ref, dkk_ref, dv_ref, dso_ref, dy_ref)


def _rwkv_params(mu, w0, w_up, a0, a_up, k_k, k_a, r_k, lnx_w, lnx_b):
    v = lambda t: t.reshape(1, -1)
    return dict(
        mur=v(mu[0:4096]), muk=v(mu[4096:8192]), muv=v(mu[8192:12288]),
        muwd=v(mu[12288:12416]), muad=v(mu[12416:12544]),
        w0=v(w0), wup=w_up.astype(BF16), a0=v(a0), aup=a_up.astype(BF16), kk=v(k_k), ka=v(k_a),
        lnw=v(lnx_w), lnb=v(lnx_b), rk=v(r_k),
        seg=jnp.asarray(np.kron(np.eye(LANE // 64), np.ones((64, 64))), BF16))


def _rwkv_prompt(proj, nb, l, rp, w_ride, s_dec, dec_rows, v_dec):
    nr = l // RW_ROWS
    rb = lambda b, h, c: b * nr + c
    w512 = HB * 64
    nhg = RWKV_HEADS // HB
    nsteps = nb * nhg * nr
    lin = lambda b, h, c: (b * nhg + h) * nr + c
    wi_spec, wo_spec, wo_shape = _ride_specs(w_ride, nsteps, lin)
    ns = s_dec.shape[-1]
    vps = RWKV_HEADS * 64 // nsteps
    assert vps * nsteps == RWKV_HEADS * 64 and (vps % 64 == 0 or (64 % vps == 0 and vps % 8 == 0))
    if vps >= 64:
        d_state = pl.BlockSpec((vps // 64, 64, 64, ns), lambda b, h, c: (lin(b, h, c), 0, 0, 0))
        d_head = pl.BlockSpec((vps, ns), lambda b, h, c: (lin(b, h, c), 0))
    else:
        per = 64 // vps
        d_state = pl.BlockSpec((1, vps, 64, ns), lambda b, h, c: (lin(b, h, c) // per, lin(b, h, c) % per, 0, 0))
        d_head = pl.BlockSpec((64, ns), lambda b, h, c: (lin(b, h, c) // per, 0))
    d_vrow = pl.BlockSpec((vps, ns), lambda b, h, c: (lin(b, h, c), 0))
    col = lambda c0: pl.BlockSpec((RW_ROWS, w512), lambda b, h, c: (rb(b, h, c), c0 // w512 + h))
    lora = lambda c0: pl.BlockSpec((RW_ROWS, LORA), lambda b, h, c: (rb(b, h, c), c0 // LORA))
    vec = pl.BlockSpec((1, w512), lambda b, h, c: (0, h))
    vec128 = pl.BlockSpec((1, LORA), lambda b, h, c: (0, 0))
    up = pl.BlockSpec((LORA, w512), lambda b, h, c: (0, h))
    tri = jnp.tril(jnp.ones((WKV_CHUNK, WKV_CHUNK), BF16))
    gate = pl.BlockSpec((pl.Element(RW_ROWS), pl.Element(w512)),
                        lambda b, h, c: (rb(b, h, c) * RW_ROWS, pl.multiple_of(_C_G + h * w512, LANE)))
    in_specs = [col(_C_R), col(_C_K), col(_C_V), gate, lora(_C_WD), lora(_C_AD),
                vec, vec, vec, vec128, vec128,
                vec, up, vec, up, vec, vec, vec, vec, vec,
                pl.BlockSpec((LANE, LANE), lambda b, h, c: (0, 0)),
                pl.BlockSpec((WKV_CHUNK, WKV_CHUNK), lambda b, h, c: (0, 0)),
                wi_spec,
                d_state, d_head, d_head, d_head, d_head, d_head, d_vrow]
    out_specs = [pl.BlockSpec((RW_ROWS, w512), lambda b, h, c: (rb(b, h, c), h)),
                 pl.BlockSpec((1, HB, 64, 64), lambda b, h, c: (b, h, 0, 0)),
                 wo_spec, d_state, d_vrow]
    return pl.pallas_call(
        _rwkv_prompt_kernel,
        grid=(nb, RWKV_HEADS // HB, nr),
        in_specs=in_specs,
        out_specs=out_specs,
        out_shape=[jax.ShapeDtypeStruct((proj.shape[0], RWKV_WIDTH), BF16),
                   jax.ShapeDtypeStruct((nb, RWKV_HEADS, 64, 64), F32),
                   wo_shape,
                   jax.ShapeDtypeStruct(s_dec.shape, F32),
                   jax.ShapeDtypeStruct((RWKV_WIDTH, ns), F32)],
        scratch_shapes=[pltpu.VMEM((8, w512), F32)] * 3 + [pltpu.VMEM((8, LORA), F32)] * 2,
        compiler_params=_cp(("parallel", "parallel", "arbitrary")),
        name="rwkv_prompt",
    )(proj, proj, proj, proj, proj, proj,
      rp["mur"], rp["muk"], rp["muv"], rp["muwd"], rp["muad"],
      rp["w0"], rp["wup"], rp["a0"], rp["aup"], rp["kk"], rp["ka"], rp["lnw"], rp["lnb"], rp["rk"],
      rp["seg"], tri, w_ride, s_dec, *dec_rows, v_dec)


def _trig_kernel(pos_ref, freq_ref, cos_ref, sin_ref):
    ang = pos_ref[...] * freq_ref[...]
    cos_ref[...] = jnp.cos(ang)
    sin_ref[...] = jnp.sin(ang)


def _trig(pos):
    n = pos.shape[0]
    half = RET_QK_DIM // 2
    freq = (ROPE_BASE ** (-jnp.arange(half, dtype=F32) / half)).reshape(1, half)
    posb = jnp.broadcast_to(pos.astype(F32)[:, None], (n, half))
    tn = _pick_tile(n, (256, 128, 8))
    blk = pl.BlockSpec((tn, half), lambda i: (i, 0))
    return pl.pallas_call(
        _trig_kernel, grid=(n // tn,),
        in_specs=[blk, pl.BlockSpec((1, half), lambda i: (0, 0))],
        out_specs=[blk, blk],
        out_shape=[jax.ShapeDtypeStruct((n, half), F32)] * 2,
        name="rope_tables",
    )(posb, freq)


def _rotate(x, cos, sin):
    x1, x2 = x[:, :128], x[:, 128:]
    return jnp.concatenate([x1 * cos - x2 * sin, x1 * sin + x2 * cos], 1)


def _ret_decode_tile(s, gd, qrow, krow, vrow):
    lhs, rhs = _outer_rows(krow, vrow)
    sn = s * gd + _dot_tn(lhs, rhs)
    y8 = _dot(jnp.broadcast_to(qrow, (8, RET_QK_DIM)), sn)
    return sn, y8[0:1, :]


def _ret_prompt_kernel(lg_ref, gd_ref, q_ref, k_ref, v_ref, g_ref, cos_ref, sin_ref, gnw_ref,
                       ds_ref, dq_ref, dk_ref, dv_ref,
                       y_ref, s_ref, dso_ref, dy_ref):
    L = CHUNK
    hg = pl.program_id(1)
    c = pl.program_id(2)

    @pl.when(c == 0)
    def _init():
        s_ref[...] = jnp.zeros(s_ref.shape, F32)

    cos = cos_ref[...]
    sin = sin_ref[...]
    li = lax.broadcasted_iota(jnp.int32, (L, L), 0)
    si = lax.broadcasted_iota(jnp.int32, (L, L), 1)
    rel = (li - si).astype(F32)
    causal = li >= si
    icol = lax.broadcasted_iota(jnp.int32, (L, 1), 0).astype(F32)
    heads = range(RET_HB)
    lg = [lg_ref[hg * RET_HB + j] for j in heads]
    part = lambda ref, j: ref[:, j * 256:(j + 1) * 256].astype(F32)
    qr = [(_rotate(part(q_ref, j), cos, sin) * (RET_QK_DIM ** -0.5)).astype(BF16) for j in heads]
    kr = [_rotate(part(k_ref, j), cos, sin) for j in heads]
    v = [v_ref[:, j * 512:(j + 1) * 512].astype(BF16) for j in heads]
    s0 = [s_ref[0, j] for j in heads]
    qk = [_dot_nt(qr[j], kr[j]) for j in heads]
    y_st = [_dot(qr[j], s0[j]) for j in heads]
    s_in = [_dot_tn(kr[j] * jnp.exp((L - 1.0 - icol) * lg[j]), v[j]) for j in heads]
    sc = [qk[j] * jnp.exp(jnp.where(causal, rel * lg[j], -jnp.inf)) for j in heads]
    y_in = [_dot(sc[j], v[j]) for j in heads]
    outs = []
    for j in heads:
        s_ref[0, j] = s0[j] * jnp.exp(L * lg[j]) + s_in[j]
        y = y_in[j] + y_st[j] * jnp.exp((icol + 1.0) * lg[j])
        mu = jnp.mean(y, -1, keepdims=True)
        d = y - mu
        var = jnp.mean(d * d, -1, keepdims=True)
        outs.append(d * lax.rsqrt(var + RET_GN_EPS))
    o = jnp.concatenate(outs, 1) * gnw_ref[...]
    y_ref[...] = (o * _silu(g_ref[...].astype(F32))).astype(BF16)

    spp, hpb = ds_ref.shape[0], ds_ref.shape[1]
    lin = (pl.program_id(0) * pl.num_programs(1) + hg) * pl.num_programs(2) + c
    h0 = (lin % (RET_HEADS // hpb)) * hpb
    for j in range(spp):
        for hh in range(hpb):
            row = lambda ref: ref[j, hh:hh + 1, :]
            sn, yrow = _ret_decode_tile(ds_ref[j, hh], gd_ref[h0 + hh], row(dq_ref), row(dk_ref), row(dv_ref))
            dso_ref[j, hh] = sn
            dy_ref[j, hh:hh + 1, :] = yrow


def _ret_log_g():
    return jnp.log1p(-jnp.exp2(-5.0 - jnp.arange(RET_HEADS, dtype=F32)))


def _ret_prompt(proj, cos, sin, gn_w, nb, l, s_dec, q_dec, k_dec, v_dec):
    nc = l // CHUNK
    wq, wv = RET_HB * RET_QK_DIM, RET_HB * RET_V_DIM
    nhg = RET_HEADS // RET_HB
    nsteps = nb * nhg * nc
    ns = s_dec.shape[0]
    lin = lambda b, h, c, *_: (b * nhg + h) * nc + c
    rb = lambda b, h, c, *_: b * nc + c
    spp, hpb = (ns // nsteps, RET_HEADS) if nsteps <= ns else (1, RET_HEADS * ns // nsteps)
    parts = RET_HEADS // hpb
    assert spp * hpb * nsteps == ns * RET_HEADS and hpb % 8 == 0
    dec = lambda *last: pl.BlockSpec((spp, hpb) + last,
                                     lambda b, h, c, *_: (lin(b, h, c) // parts, lin(b, h, c) % parts) + (0,) * len(last))
    grid_spec = pltpu.PrefetchScalarGridSpec(
        num_scalar_prefetch=2,
        grid=(nb, nhg, nc),
        in_specs=[
            pl.BlockSpec((CHUNK, wq), lambda b, h, c, *_: (rb(b, h, c), h)),
            pl.BlockSpec((CHUNK, wq), lambda b, h, c, *_: (rb(b, h, c), RET_QK_WIDTH // wq + h)),
            pl.BlockSpec((CHUNK, wv), lambda b, h, c, *_: (rb(b, h, c), 2 * RET_QK_WIDTH // wv + h)),
            pl.BlockSpec((CHUNK, wv), lambda b, h, c, *_: (rb(b, h, c), (2 * RET_QK_WIDTH + RET_WIDTH) // wv + h)),
            pl.BlockSpec((CHUNK, 128), lambda b, h, c, *_: (c, 0)),
            pl.BlockSpec((CHUNK, 128), lambda b, h, c, *_: (c, 0)),
            pl.BlockSpec((1, wv), lambda b, h, c, *_: (0, h)),
            dec(RET_QK_DIM, RET_V_DIM), dec(RET_QK_DIM), dec(RET_QK_DIM), dec(RET_V_DIM),
        ],
        out_specs=[pl.BlockSpec((CHUNK, wv), lambda b, h, c, *_: (rb(b, h, c), h)),
                   pl.BlockSpec((1, RET_HB, RET_QK_DIM, RET_V_DIM), lambda b, h, c, *_: (b, h, 0, 0)),
                   dec(RET_QK_DIM, RET_V_DIM), dec(RET_V_DIM)],
    )
    log_g = _ret_log_g()
    return pl.pallas_call(
        _ret_prompt_kernel,
        grid_spec=grid_spec,
        out_shape=[jax.ShapeDtypeStruct((proj.shape[0], RET_WIDTH), BF16),
                   jax.ShapeDtypeStruct((nb, RET_HEADS, RET_QK_DIM, RET_V_DIM), F32),
                   jax.ShapeDtypeStruct(s_dec.shape, F32),
                   jax.ShapeDtypeStruct((ns, RET_HEADS, RET_V_DIM), F32)],
        compiler_params=_cp(("parallel", "parallel", "arbitrary")),
        name="ret_prompt",
    )(log_g, jnp.exp(log_g), proj, proj, proj, proj, cos, sin, gn_w.reshape(1, -1),
      s_dec, q_dec, k_dec, v_dec)


def _ssd_pre_kernel(xs_ref, b_ref, c_ref, csx_ref, csb_ref, csc_ref, dtc_ref,
                    cwx_ref, cwb_ref, cwc_ref, cbx_ref, cbb_ref, cbc_ref, dtb_ref, al_ref,
                    xa_ref, ba_ref, ca_ref, dt_ref, dec_ref):
    def conv(u_ref, cs_ref, w_ref, bias_ref):
        acc = bias_ref[...] + u_ref[...] * w_ref[SSD_CONV - 1:SSD_CONV, :]
        for k in range(SSD_CONV - 1):
            acc = acc + cs_ref[k] * w_ref[k:k + 1, :]
        return _silu(acc)

    xs = conv(xs_ref, csx_ref, cwx_ref, cbx_ref)
    xa_ref[...] = xs
    ba_ref[0] = conv(b_ref, csb_ref, cwb_ref, cbb_ref)
    ca_ref[0] = conv(c_ref, csc_ref, cwc_ref, cbc_ref)
    dt = _softplus(dtc_ref[0] + dtb_ref[0])
    dt_ref[0] = dt
    dec_ref[0] = jnp.exp(dt * (-jnp.exp(al_ref[0])))


def _ssd_pre(proj, rb0, ns, cs_t, dtc, sp):
    g8 = SSD_GROUPS
    in_specs = [
        pl.BlockSpec((ns, 512), lambda g: (rb0, _C_XS // 512 + g)),
        pl.BlockSpec((ns, 128), lambda g: (rb0, _C_B // 128 + g)),
        pl.BlockSpec((ns, 128), lambda g: (rb0, _C_C // 128 + g)),
        pl.BlockSpec((3, ns, 512), lambda g: (0, 0, g)),
        pl.BlockSpec((3, ns, 128), lambda g: (0, 0, 4096 // 128 + g)),
        pl.BlockSpec((3, ns, 128), lambda g: (0, 0, 5120 // 128 + g)),
        pl.BlockSpec((1, ns, 8), lambda g: (g, rb0, 0)),
        pl.BlockSpec((SSD_CONV, 512), lambda g: (0, g)),
        pl.BlockSpec((SSD_CONV, 128), lambda g: (0, g)),
        pl.BlockSpec((SSD_CONV, 128), lambda g: (0, g)),
        pl.BlockSpec((1, 512), lambda g: (0, g)),
        pl.BlockSpec((1, 128), lambda g: (0, g)),
        pl.BlockSpec((1, 128), lambda g: (0, g)),
        pl.BlockSpec((1, 1, 8), lambda g: (g, 0, 0)),
        pl.BlockSpec((1, 1, 8), lambda g: (g, 0, 0)),
    ]
    out_specs = [
        pl.BlockSpec((ns, 512), lambda g: (0, g)),
        pl.BlockSpec((1, ns, 128), lambda g: (g, 0, 0)),
        pl.BlockSpec((1, ns, 128), lambda g: (g, 0, 0)),
        pl.BlockSpec((1, ns, 8), lambda g: (g, 0, 0)),
        pl.BlockSpec((1, ns, 8), lambda g: (g, 0, 0)),
    ]
    out_shape = [
        jax.ShapeDtypeStruct((ns, SSD_WIDTH), F32),
        jax.ShapeDtypeStruct((g8, ns, SSD_STATE), F32),
        jax.ShapeDtypeStruct((g8, ns, SSD_STATE), F32),
        jax.ShapeDtypeStruct((g8, ns, 8), F32),
        jax.ShapeDtypeStruct((g8, ns, 8), F32),
    ]
    return pl.pallas_call(
        _ssd_pre_kernel, grid=(g8,), in_specs=in_specs, out_specs=out_specs, out_shape=out_shape,
        compiler_params=_cp(("parallel",)), name="ssd_sample_pre",
    )(proj, proj, proj, cs_t, cs_t, cs_t, dtc, sp["cwx"], sp["cwb"], sp["cwc"], sp["cbx"], sp["cbb"], sp["cbc"],
      sp["dtbc"], sp["alc"])


def _outer_rows(x, y):
    hi = lambda t: t.astype(BF16).astype(F32)
    xh, yh = hi(x), hi(y)
    rx = lax.broadcasted_iota(jnp.int32, (8, x.shape[1]), 0)
    ry = lax.broadcasted_iota(jnp.int32, (8, y.shape[1]), 0)
    lhs = jnp.where(rx == 1, x - xh, jnp.where((rx == 0) | (rx == 2), xh, 0.0))
    rhs = jnp.where(ry == 2, y - yh, jnp.where(ry < 2, yh, 0.0))
    return lhs.astype(BF16), rhs.astype(BF16)


def _ssm_decode(dt_ref, dec_ref, s_ref, x_ref, b_ref, c_ref, so_ref, y_ref, b0, h0):
    hw = SSD_HPG * SSD_HEAD_DIM
    for j in range(s_ref.shape[0]):
        for gi in range(s_ref.shape[1] // SSD_HPG):
            g = h0 // SSD_HPG + gi
            lhs, rhs = _outer_rows(x_ref[j, :, gi * hw:(gi + 1) * hw], b_ref[j, pl.ds(g, 1), :])
            xb = _dot_tn(lhs, rhs)
            new = []
            for r in range(SSD_HPG):
                hh = gi * SSD_HPG + r
                idx = (b0 + j) * SSD_HEADS + h0 + hh
                sn = s_ref[j, hh] * dec_ref[idx] + xb[r * 64:(r + 1) * 64] * dt_ref[idx]
                so_ref[j, hh] = sn
                new.append(sn)
            crow = jnp.broadcast_to(c_ref[j, pl.ds(g, 1), :], (8, SSD_STATE))
            y_ref[j, gi] = _dot_nt(crow, jnp.concatenate(new, 0))[0:1, :]


def _ssm_state_kernel(dt_ref, dec_ref, s_ref, x_ref, b_ref, c_ref, so_ref, y_ref):
    _ssm_decode(dt_ref, dec_ref, s_ref, x_ref, b_ref, c_ref, so_ref, y_ref, pl.program_id(0) * SSM_BT, 0)


def _ssm_state(dt, dec, s, xa, ba, ca):
    ns = s.shape[0]
    bt = SSM_BT
    smem = pl.BlockSpec(memory_space=pltpu.SMEM)
    sblk = pl.BlockSpec((bt, SSD_HEADS, SSD_HEAD_DIM, SSD_STATE), lambda i: (i, 0, 0, 0))
    bc = pl.BlockSpec((bt, SSD_GROUPS, SSD_STATE), lambda i: (i, 0, 0))
    return pl.pallas_call(
        _ssm_state_kernel, grid=(ns // bt,),
        in_specs=[smem, smem, sblk, pl.BlockSpec((bt, 1, SSD_WIDTH), lambda i: (i, 0, 0)), bc, bc],
        out_specs=[sblk, pl.BlockSpec((bt, SSD_GROUPS, 1, 512), lambda i: (i, 0, 0, 0))],
        out_shape=[jax.ShapeDtypeStruct(s.shape, F32), jax.ShapeDtypeStruct((ns, SSD_GROUPS, 1, 512), F32)],
        compiler_params=_cp(("parallel",)), name="ssm_sample_state",
    )(dt, dec, s, xa, ba, ca)


def _ssd_post_kernel(y_ref, xa_ref, z_ref, dsk_ref, nw_ref, dst_ref, o_ref):
    del dst_ref
    y = (y_ref[...] + xa_ref[...] * dsk_ref[...]) * _silu(z_ref[...])
    y = y * lax.rsqrt(jnp.mean(y * y, -1, keepdims=True) + RMS_EPS) * nw_ref[...]
    o_ref[...] = y.astype(BF16)


def _ssd_post(y, xa, proj, rb0, sp, dst):
    ns = y.shape[0]
    blk = pl.BlockSpec((ns, 512), lambda g: (0, g))
    vec = pl.BlockSpec((1, 512), lambda g: (0, g))
    return pl.pallas_call(
        _ssd_post_kernel, grid=(SSD_GROUPS,),
        in_specs=[blk, blk, pl.BlockSpec((ns, 512), lambda g: (rb0, _C_Z // 512 + g)), vec, vec,
                  pl.BlockSpec(memory_space=pl.ANY)],
        out_specs=pl.BlockSpec((ns, 512), lambda g: (rb0, g)),
        out_shape=jax.ShapeDtypeStruct(dst.shape, BF16),
        input_output_aliases={5: 0},
        compiler_params=_cp(("parallel",)), name="ssd_sample_post",
    )(y, xa, proj, sp["dsk"], sp["nw"], dst)


def _wkv_pre_kernel(r_ref, k_ref, v_ref, wd_ref, ad_ref, sr_ref, sk_ref, sv_ref, swd_ref, sad_ref,
                    mur_ref, muk_ref, muv_ref, muwd_ref, muad_ref,
                    w0_ref, wup_ref, a0_ref, aup_ref, kk_ref, ka_ref, seg_ref,
                    ro_ref, ko_ref, vo_ref, rt_ref, wt_ref, kt_ref, bt_ref, kkt_ref, vt_ref):
    mix = lambda x_ref, s_ref, mu_ref: x_ref[...] + (s_ref[...] - x_ref[...]) * mu_ref[...]
    rm = mix(r_ref, sr_ref, mur_ref)
    km = mix(k_ref, sk_ref, muk_ref)
    vm = mix(v_ref, sv_ref, muv_ref)
    wdm = mix(wd_ref, swd_ref, muwd_ref)
    adm = mix(ad_ref, sad_ref, muad_ref)
    logw, kk, k2, bv = _rwkv_mix(rm, km, vm, wdm, adm, w0_ref[...], wup_ref[...], a0_ref[...], aup_ref[...],
                                 kk_ref[...], ka_ref[...], seg_ref[...])
    ro_ref[...] = rm
    ko_ref[...] = k2
    vo_ref[...] = vm
    rt_ref[...] = rm.T
    wt_ref[...] = jnp.exp(logw).T
    kt_ref[...] = k2.T
    bt_ref[...] = bv.T
    kkt_ref[...] = kk.T
    vt_ref[...] = vm.T


def _wkv_pre(proj, rb0, shift, rp):
    ns = shift.shape[0]
    w512 = HB * 64
    col = lambda c0: pl.BlockSpec((ns, w512), lambda h: (rb0, c0 // w512 + h))
    lora = lambda c0: pl.BlockSpec((ns, LORA), lambda h: (rb0, c0 // LORA))
    scol = lambda c0: pl.BlockSpec((ns, w512), lambda h: (0, c0 // w512 + h))
    slora = lambda c0: pl.BlockSpec((ns, LORA), lambda h: (0, c0 // LORA))
    vec = pl.BlockSpec((1, w512), lambda h: (0, h))
    vec128 = pl.BlockSpec((1, LORA), lambda h: (0, 0))
    up = pl.BlockSpec((LORA, w512), lambda h: (0, h))
    row = pl.BlockSpec((ns, w512), lambda h: (0, h))
    tr = pl.BlockSpec((w512, ns), lambda h: (h, 0))
    return pl.pallas_call(
        _wkv_pre_kernel, grid=(RWKV_HEADS // HB,),
        in_specs=[col(_C_R), col(_C_K), col(_C_V), lora(_C_WD), lora(_C_AD),
                  scol(0), scol(4096), scol(8192), slora(12288), slora(12416),
                  vec, vec, vec, vec128, vec128, vec, up, vec, up, vec, vec,
                  pl.BlockSpec((LANE, LANE), lambda h: (0, 0))],
        out_specs=[row] * 3 + [tr] * 6,
        out_shape=[jax.ShapeDtypeStruct((ns, RWKV_WIDTH), F32)] * 3
        + [jax.ShapeDtypeStruct((RWKV_WIDTH, ns), F32)] * 6,
        compiler_params=_cp(("parallel",)), name="wkv_sample_pre",
    )(proj, proj, proj, proj, proj, shift, shift, shift, shift, shift,
      rp["mur"], rp["muk"], rp["muv"], rp["muwd"], rp["muad"],
      rp["w0"], rp["wup"], rp["a0"], rp["aup"], rp["kk"], rp["ka"], rp["seg"])


def _wkv_post_kernel(o_ref, r_ref, k_ref, v_ref, g_ref, lnw_ref, lnb_ref, rk_ref, seg_ref, dst_ref, y_ref):
    del dst_ref
    y_ref[...] = _rwkv_out(o_ref[...].T, r_ref[...], k_ref[...], v_ref[...], g_ref[...],
                           lnw_ref[...], lnb_ref[...], rk_ref[...], seg_ref[...])


def _wkv_post(o_t, r, k2, v, proj, rb0, rp, dst):
    ns = o_t.shape[1]
    w512 = HB * 64
    row = pl.BlockSpec((ns, w512), lambda h: (0, h))
    vec = pl.BlockSpec((1, w512), lambda h: (0, h))
    return pl.pallas_call(
        _wkv_post_kernel, grid=(RWKV_HEADS // HB,),
        in_specs=[pl.BlockSpec((w512, ns), lambda h: (h, 0)), row, row, row,
                  pl.BlockSpec((pl.Element(ns), pl.Element(w512)), lambda h: (rb0 * ns, pl.multiple_of(_C_G + h * w512, LANE))),
                  vec, vec, vec, pl.BlockSpec((LANE, LANE), lambda h: (0, 0)),
                  pl.BlockSpec(memory_space=pl.ANY)],
        out_specs=pl.BlockSpec((ns, w512), lambda h: (rb0, h)),
        out_shape=jax.ShapeDtypeStruct(dst.shape, BF16),
        input_output_aliases={9: 0},
        compiler_params=_cp(("parallel",)), name="wkv_sample_post",
    )(o_t, r, k2, v, proj, rp["lnw"], rp["lnb"], rp["rk"], rp["seg"], dst)


def _ret_pre_kernel(q_ref, k_ref, cos_ref, sin_ref, qo_ref, ko_ref):
    cos = cos_ref[0:1, :]
    sin = sin_ref[0:1, :]
    qo_ref[...] = _rotate(q_ref[...].astype(F32), cos, sin) * (RET_QK_DIM ** -0.5)
    ko_ref[...] = _rotate(k_ref[...].astype(F32), cos, sin)


def _ret_pre(proj, rb0, ns, cos, sin):
    blk = pl.BlockSpec((ns, RET_QK_DIM), lambda h: (0, h))
    return pl.pallas_call(
        _ret_pre_kernel, grid=(RET_HEADS,),
        in_specs=[pl.BlockSpec((ns, RET_QK_DIM), lambda h: (rb0, h)),
                  pl.BlockSpec((ns, RET_QK_DIM), lambda h: (rb0, RET_HEADS + h)),
                  pl.BlockSpec((8, 128), lambda h: (0, 0)), pl.BlockSpec((8, 128), lambda h: (0, 0))],
        out_specs=[blk, blk],
        out_shape=[jax.ShapeDtypeStruct((ns, RET_QK_WIDTH), F32)] * 2,
        compiler_params=_cp(("parallel",)), name="ret_sample_pre",
    )(proj, proj, cos, sin)


def _ret_post_kernel(y_ref, g_ref, gnw_ref, dst_ref, o_ref):
    del dst_ref
    y = y_ref[...]
    mu = jnp.mean(y, -1, keepdims=True)
    d = y - mu
    var = jnp.mean(d * d, -1, keepdims=True)
    o = d * lax.rsqrt(var + RET_GN_EPS) * gnw_ref[...]
    o_ref[...] = (o * _silu(g_ref[...].astype(F32))).astype(BF16)


def _ret_post(y, proj, rb0, gn_w, dst):
    ns = y.shape[0]
    blk = pl.BlockSpec((ns, RET_V_DIM), lambda h: (0, h))
    return pl.pallas_call(
        _ret_post_kernel, grid=(RET_HEADS,),
        in_specs=[blk, pl.BlockSpec((ns, RET_V_DIM), lambda h: (rb0, (2 * RET_QK_WIDTH + RET_WIDTH) // RET_V_DIM + h)),
                  pl.BlockSpec((1, RET_V_DIM), lambda h: (0, h)), pl.BlockSpec(memory_space=pl.ANY)],
        out_specs=pl.BlockSpec((ns, RET_V_DIM), lambda h: (rb0, h)),
        out_shape=jax.ShapeDtypeStruct(dst.shape, BF16),
        input_output_aliases={3: 0},
        compiler_params=_cp(("parallel",)), name="ret_sample_post",
    )(y, proj, gn_w.reshape(1, -1), dst)


def _ab_layer(xp, xs, xb, nb, l, ns, conv_s, ssm_s, shift_s, wkv_s, w_in, sp, rp, w_out, ln_w, ln_b,
              w_next_in, w_next_out):
    mp = nb * l
    rb0 = mp // 128
    wt = w_in.T
    wt_ssd = wt[:AB_SSD_W + LANE].astype(BF16)
    tiles = lambda w: [j * AB_TN for j in range(w // AB_TN)]
    proj_a, wt_rwkv = _matmul_wt(xb, wt_ssd, tiles(AB_SSD_W), AB_TN, "ab_in_proj_ssd",
                                 ride=(wt, AB_SSD_W + SSD_HEADS, AB_RWKV_W))
    pdt = _matmul_wt(xb, wt_ssd, [AB_SSD_W], LANE, "ab_dt_proj")[:, :SSD_HEADS]
    proj_b = _matmul_wt(xb, wt_rwkv, tiles(AB_RWKV_W), AB_TN, "ab_in_proj_rwkv")
    m = proj_a.shape[0]
    dt3 = pdt.reshape(m, SSD_GROUPS, SSD_HPG)
    dtc = dt3.transpose(1, 0, 2)
    dtr = dt3.transpose(1, 2, 0)

    ya, ssm_p, w_out_b, w_next_out_b = _ssd_prompt(proj_a, dtc, dtr, nb, l, sp, w_out, w_next_out)
    r_s, k_s, v_s, r_t, w_t, k_t, b_t, kk_t, v_t = _wkv_pre(proj_b, rb0, shift_s.reshape(ns, SHIFT_DIM), rp)
    yb, wkv_p, w_next_in_b, wkv_t, o_t = _rwkv_prompt(proj_b, nb, l, rp, w_next_in, wkv_s.transpose(1, 2, 3, 0),
                                                      (r_t, w_t, k_t, b_t, kk_t), v_t)
    wkv_n = wkv_t.transpose(3, 0, 1, 2)
    tail = lambda p, n, c0, c1: jnp.stack([p[(b + 1) * l - n:(b + 1) * l, c0:c1] for b in range(nb)])
    conv_p = tail(proj_a, SSD_CONV - 1, _C_XS, AB_SSD_W)
    shift_p = tail(proj_b, 1, _C_R, _C_G)

    xa, ba, ca, dt_s, dec_s = _ssd_pre(proj_a, rb0, ns, conv_s.transpose(1, 0, 2), dtc, sp)
    flat = lambda t: t.transpose(1, 0, 2).reshape(ns * SSD_HEADS)
    ssm_n, y_s = _ssm_state(flat(dt_s), flat(dec_s), ssm_s, xa.reshape(ns, 1, SSD_WIDTH),
                            ba.transpose(1, 0, 2), ca.transpose(1, 0, 2))
    ya = _ssd_post(y_s.reshape(ns, SSD_WIDTH), xa, proj_a, rb0, sp, ya)
    conv_n = jnp.concatenate([conv_s[:, 1:], proj_a[mp:, None, _C_XS:AB_SSD_W]], axis=1)

    yb = _wkv_post(o_t, r_s, k_s, v_s, proj_b, rb0, rp, yb)
    shift_n = proj_b[mp:, None, _C_R:_C_G]

    out = _matmul2(ya, yb, w_out_b, "ab_out_proj")
    x_new, xb_new = _deepnorm_first(xp, xs, out, ln_w, ln_b, "ab_deepnorm")
    return (x_new, xb_new, (conv_p, ssm_p, shift_p, wkv_p), (conv_n, ssm_n, shift_n, wkv_n),
            w_next_in_b, w_next_out_b)


def _ret_layer(x, xb, nb, l, ns, ret_s, w_in, gn_w, w_out, ln_w, ln_b):
    mp = nb * l
    rb0 = mp // 128
    proj = _matmul(xb, w_in, "ret_in_proj", BF16)
    cos, sin = _trig(jnp.arange(l))
    cos_s, sin_s = _trig(jnp.full((8,), PAST_LEN))
    q_s, k_s = _ret_pre(proj, rb0, ns, cos_s, sin_s)
    v_s = proj[mp:, 2 * RET_QK_WIDTH:2 * RET_QK_WIDTH + RET_WIDTH].astype(F32).reshape(ns, RET_HEADS, RET_V_DIM)
    h3 = lambda t: t.reshape(ns, RET_HEADS, RET_QK_DIM)
    y, ret_p, ret_n, o_s = _ret_prompt(proj, cos, sin, gn_w, nb, l, ret_s, h3(q_s), h3(k_s), v_s)
    y = _ret_post(o_s.reshape(ns, RET_WIDTH), proj, rb0, gn_w, y)

    out = _matmul(y, w_out, "ret_out_proj")
    y_p, y_s = _deepnorm_last(x, out, ln_w, ln_b, ns, "ret_deepnorm")
    return y_p, y_s, ret_p, ret_n


def kernel(x_prompt, x_sample, state_conv, state_ssm, state_shift, state_wkv, state_ret, ab_w_in, ssd_conv_w, ssd_conv_b, ssd_dt_bias, ssd_a_log, ssd_d, ssd_norm_w, rwkv_mu, rwkv_w0, rwkv_w_up, rwkv_a0, rwkv_a_up, rwkv_k_k, rwkv_k_a, rwkv_r_k, rwkv_lnx_w, rwkv_lnx_b, ab_w_out, ab_ln_w, ab_ln_b, ret_w_in, ret_gn_w, ret_w_out, ret_ln_w, ret_ln_b):
    nb, l, d = x_prompt.shape
    ns = x_sample.shape[0]
    assert x_sample.shape[1] == 1 and l % CHUNK == 0 and ns % LANE == 0 and ns == LANE
    mp = nb * l
    xp, xs = x_prompt.reshape(mp, d), x_sample.reshape(ns, d)
    xb = jnp.concatenate([xp.astype(BF16), xs.astype(BF16)], axis=0)

    sp = _ssd_params(ssd_conv_w[0], ssd_conv_b[0], ssd_dt_bias[0], ssd_a_log[0], ssd_d[0], ssd_norm_w[0])
    rp = _rwkv_params(rwkv_mu[0], rwkv_w0[0], rwkv_w_up[0], rwkv_a0[0], rwkv_a_up[0], rwkv_k_k[0], rwkv_k_a[0],
                      rwkv_r_k[0], rwkv_lnx_w[0], rwkv_lnx_b[0])
    x, xb, pst, sst, ret_w_in_b, ret_w_out_b = _ab_layer(
        xp, xs, xb, nb, l, ns, state_conv[0], state_ssm[0], state_shift[0], state_wkv[0],
        ab_w_in[0], sp, rp, ab_w_out[0], ab_ln_w[0], ab_ln_b[0], ret_w_in[0], ret_w_out[0])
    y_p, y_s, ret_p, ret_n = _ret_layer(x, xb, nb, l, ns, state_ret[0], ret_w_in_b, ret_gn_w[0], ret_w_out_b,
                                        ret_ln_w[0], ret_ln_b[0])
    y_prompt = y_p.reshape(nb, l, d)
    y_sample = y_s.reshape(ns, 1, d)
    st = lambda t: t[None]
    return (y_prompt, y_sample,
            st(pst[0]), st(pst[1]), st(pst[2]), st(pst[3]), st(ret_p),
            st(sst[0]), st(sst[1]), st(sst[2]), st(sst[3]), st(ret_n))
```

```python
import functools
import math

import jax
import jax.numpy as jnp
import numpy as np
from jax import lax
from jax.experimental import pallas as pl
from jax.experimental.pallas import tpu as pltpu

F32 = jnp.float32
BF16 = jnp.bfloat16

D_MODEL = 4096
DEPTH = 2
PAST_LEN = 16384
SSD_WIDTH = 4096
SSD_HEAD_DIM = 64
SSD_HEADS = 64
SSD_GROUPS = 8
SSD_HPG = 8
SSD_STATE = 128
SSD_CONV = 4
SSD_CONV_DIM = SSD_WIDTH + 2 * SSD_GROUPS * SSD_STATE
RWKV_WIDTH = 4096
RWKV_HEAD_DIM = 64
RWKV_HEADS = 64
LORA = 128
SHIFT_DIM = 3 * RWKV_WIDTH + 2 * LORA
RET_HEADS = 16
RET_QK_DIM = 256
RET_V_DIM = 512
RET_QK_WIDTH = 4096
RET_WIDTH = 8192
ROPE_BASE = 10000.0
CHUNK = 128
ALPHA = (2 * DEPTH) ** 0.25
LN_EPS = 1e-5
RMS_EPS = 1e-5
RWKV_GN_EPS = 64e-5
RET_GN_EPS = 1e-6

LANE = 128
VMEM_LIMIT = 56 * 1024 * 1024
WKV_CHUNK = 64
RW_ROWS = 512
HB = 8
RET_HB = 4
SSM_BT = 2
SSD_CPS = 4
WKV_UNROLL = 8

_C_Z, _C_XS, _C_B, _C_C = 0, 4096, 8192, 9216
_C_R, _C_K, _C_V, _C_WD, _C_AD, _C_G = 0, 4096, 8192, 12288, 12416, 12544
AB_SSD_W = 10240
AB_RWKV_W = 16640
AB_TN = 1280


def _cp(sem):
    return pltpu.CompilerParams(dimension_semantics=sem, vmem_limit_bytes=VMEM_LIMIT)


def _silu(x):
    return x * jax.nn.sigmoid(x)


def _softplus(x):
    return jnp.maximum(x, 0.0) + jnp.log1p(jnp.exp(-jnp.abs(x)))


def _dot(a, b):
    return jnp.dot(a.astype(BF16), b.astype(BF16), preferred_element_type=F32)


def _dot_nt(a, b):
    return lax.dot_general(a.astype(BF16), b.astype(BF16), (((1,), (1,)), ((), ())),
                           preferred_element_type=F32)


def _dot_tn(a, b):
    return lax.dot_general(a.astype(BF16), b.astype(BF16), (((0,), (0,)), ((), ())),
                           preferred_element_type=F32)


def _split(x, n):
    parts, r = [], x
    for _ in range(n):
        h = r.astype(BF16)
        parts.append(h)
        r = r - h.astype(F32)
    return parts


def _dot01(m01, x, n=3):
    return sum(jnp.dot(m01, p, preferred_element_type=F32) for p in _split(x, n))


def _dot01_r(x, m01, n=2):
    return sum(jnp.dot(p, m01, preferred_element_type=F32) for p in _split(x, n))


def _segsum(x, seg):
    r, w = x.shape
    nt = w // LANE
    tall = jnp.concatenate([x[:, i * LANE:(i + 1) * LANE] for i in range(nt)], axis=0)
    s = _dot01_r(tall, seg)
    return jnp.concatenate([s[i * r:(i + 1) * r] for i in range(nt)], axis=1)


def _onehot_cols(b, n):
    rows = lax.broadcasted_iota(jnp.int32, (LANE, n), 0)
    return jnp.where(rows == b, 1.0, 0.0).astype(BF16)


def _mm_kernel(x_ref, w_ref, o_ref):
    o_ref[...] = jnp.dot(x_ref[...], w_ref[...], preferred_element_type=F32).astype(o_ref.dtype)


def _pick_tile(n, prefs):
    for t in prefs:
        if n % t == 0:
            return t
    return n


def _matmul(x, w, name, out_dtype=F32):
    m, k = x.shape
    n = w.shape[1]
    tm = _pick_tile(m, (640, 512, 256, 128))
    tn = _pick_tile(n, (1280, 1024, 512, 256, 128) if k <= 4096 else (512, 256, 128))
    return pl.pallas_call(
        _mm_kernel,
        grid=(n // tn, m // tm),
        in_specs=[pl.BlockSpec((tm, k), lambda j, i: (i, 0)),
                  pl.BlockSpec((k, tn), lambda j, i: (0, j))],
        out_specs=pl.BlockSpec((tm, tn), lambda j, i: (i, j)),
        out_shape=jax.ShapeDtypeStruct((m, n), out_dtype),
        compiler_params=_cp(("parallel", "parallel")),
        name=name,
    )(x, w)


def _mm_wt_kernel(st_ref, x_ref, wt_ref, o_ref):
    del st_ref
    o_ref[...] = lax.dot_general(x_ref[...], wt_ref[...], (((1,), (1,)), ((), ())), preferred_element_type=F32)


def _mm_wt_ride_kernel(st_ref, x_ref, wt_ref, wi_ref, o_ref, wo_ref):
    _mm_wt_kernel(st_ref, x_ref, wt_ref, o_ref)
    wo_ref[...] = wi_ref[...].astype(BF16)


def _matmul_wt(x, wt, row_starts, tn, name, ride=None):
    m, k = x.shape
    nt = len(row_starts)
    tm = _pick_tile(m, (640, 512, 256, 128))
    nm = m // tm
    starts = jnp.asarray(row_starts, jnp.int32)
    in_specs = [pl.BlockSpec((tm, k), lambda j, i, st: (i, 0)),
                pl.BlockSpec((pl.Element(tn), pl.Element(k)), lambda j, i, st: (pl.multiple_of(st[j], 64), 0))]
    out_specs = pl.BlockSpec((tm, tn), lambda j, i, st: (i, j))
    out_shape = jax.ShapeDtypeStruct((m, nt * tn), F32)
    args = (starts, x, wt)
    body = _mm_wt_kernel
    if ride is not None:
        w_f32, row0, nrows = ride
        rps = nrows // (nt * nm)
        assert rps * nt * nm == nrows and rps % 16 == 0 and row0 % 8 == 0
        in_specs.append(pl.BlockSpec((pl.Element(rps), pl.Element(w_f32.shape[1])),
                                     lambda j, i, st: (pl.multiple_of(row0 + (j * nm + i) * rps, 8), 0)))
        out_specs = [out_specs, pl.BlockSpec((rps, w_f32.shape[1]), lambda j, i, st: (j * nm + i, 0))]
        out_shape = [out_shape, jax.ShapeDtypeStruct((nrows, w_f32.shape[1]), BF16)]
        args = args + (w_f32,)
        body = _mm_wt_ride_kernel
    grid_spec = pltpu.PrefetchScalarGridSpec(num_scalar_prefetch=1, grid=(nt, nm),
                                             in_specs=in_specs, out_specs=out_specs)
    return pl.pallas_call(
        body, grid_spec=grid_spec, out_shape=out_shape,
        compiler_params=_cp(("parallel", "parallel")),
        name=name,
    )(*args)


def _mm_wt_ssm_kernel(st_ref, dt_ref, dec_ref, x_ref, wt_ref, ds_ref, dx_ref, db_ref, dc_ref,
                      o_ref, dso_ref, dy_ref):
    spp = ds_ref.shape[0]
    lin = pl.program_id(0) * pl.num_programs(1) + pl.program_id(1)
    nblk = dt_ref.shape[0] // (SSD_HEADS * spp)

    @pl.when(lin < nblk)
    def _decode():
        _ssm_decode(dt_ref, dec_ref, ds_ref, dx_ref, db_ref, dc_ref, dso_ref, dy_ref, lin * spp, 0)

    _mm_wt_kernel(st_ref, x_ref, wt_ref, o_ref)


def _matmul_wt_ssm(x, wt, row_starts, tn, name, dt, dec, s, xa, ba, ca):
    m, k = x.shape
    nt = len(row_starts)
    tm = _pick_tile(m, (640, 512, 256, 128))
    nm = m // tm
    ns = s.shape[0]
    nblk = max(d for d in range(1, nt * nm + 1) if ns % d == 0)
    spp = ns // nblk
    blk = lambda j, i, *_: jnp.minimum(j * nm + i, nblk - 1)
    d_state = pl.BlockSpec((spp, SSD_HEADS, SSD_HEAD_DIM, SSD_STATE), lambda *a: (blk(*a), 0, 0, 0))
    d_bc = pl.BlockSpec((spp, SSD_GROUPS, SSD_STATE), lambda *a: (blk(*a), 0, 0))
    d_y = pl.BlockSpec((spp, SSD_GROUPS, 1, SSD_HPG * SSD_HEAD_DIM), lambda *a: (blk(*a), 0, 0, 0))
    grid_spec = pltpu.PrefetchScalarGridSpec(
        num_scalar_prefetch=3, grid=(nt, nm),
        in_specs=[pl.BlockSpec((tm, k), lambda j, i, *_: (i, 0)),
                  pl.BlockSpec((pl.Element(tn), pl.Element(k)),
                               lambda j, i, st, *_: (pl.multiple_of(st[j], 64), 0)),
                  d_state, pl.BlockSpec((spp, 1, SSD_WIDTH), lambda *a: (blk(*a), 0, 0)), d_bc, d_bc],
        out_specs=[pl.BlockSpec((tm, tn), lambda j, i, *_: (i, j)), d_state, d_y])
    return pl.pallas_call(
        _mm_wt_ssm_kernel, grid_spec=grid_spec,
        out_shape=[jax.ShapeDtypeStruct((m, nt * tn), F32), jax.ShapeDtypeStruct(s.shape, F32),
                   jax.ShapeDtypeStruct((ns, SSD_GROUPS, 1, SSD_HPG * SSD_HEAD_DIM), F32)],
        compiler_params=_cp(("arbitrary", "arbitrary")),
        name=name,
    )(jnp.asarray(row_starts, jnp.int32), dt, dec, x, wt, s, xa, ba, ca)


def _mm2_kernel(a_ref, b_ref, w_ref, o_ref):
    ka = a_ref.shape[1]
    o_ref[...] = (jnp.dot(a_ref[...], w_ref[0:ka, :], preferred_element_type=F32)
                  + jnp.dot(b_ref[...], w_ref[ka:, :], preferred_element_type=F32))


def _matmul2(a, b, w, name):
    m, ka = a.shape
    kb = b.shape[1]
    n = w.shape[1]
    tm = _pick_tile(m, (640, 512, 256, 128))
    tn = _pick_tile(n, (512, 256, 128))
    return pl.pallas_call(
        _mm2_kernel,
        grid=(n // tn, m // tm),
        in_specs=[pl.BlockSpec((tm, ka), lambda j, i: (i, 0)),
                  pl.BlockSpec((tm, kb), lambda j, i: (i, 0)),
                  pl.BlockSpec((ka + kb, tn), lambda j, i: (0, j))],
        out_specs=pl.BlockSpec((tm, tn), lambda j, i: (i, j)),
        out_shape=jax.ShapeDtypeStruct((m, n), F32),
        compiler_params=_cp(("parallel", "parallel")),
        name=name,
    )(a, b, w)


def _post_norm(x, o, w, b):
    h = ALPHA * x + o
    mu = jnp.mean(h, -1, keepdims=True)
    d = h - mu
    var = jnp.mean(d * d, -1, keepdims=True)
    return d * lax.rsqrt(var + LN_EPS) * w + b


def _ln_first_kernel(xp_ref, xs_ref, o_ref, w_ref, b_ref, y_ref, yb_ref):
    is_sample = pl.program_id(0) == pl.num_programs(0) - 1
    x = jnp.where(is_sample, xs_ref[...], xp_ref[...])
    y = _post_norm(x, o_ref[...], w_ref[...], b_ref[...])
    y_ref[...] = y
    yb_ref[...] = y.astype(BF16)


def _deepnorm_first(xp, xs, o, w, b, name):
    mp, d = xp.shape
    ns = xs.shape[0]
    npt = mp // ns
    row = pl.BlockSpec((ns, d), lambda i: (i, 0))
    vec = pl.BlockSpec((1, d), lambda i: (0, 0))
    return pl.pallas_call(
        _ln_first_kernel,
        grid=(npt + 1,),
        in_specs=[pl.BlockSpec((ns, d), lambda i: (jnp.minimum(i, npt - 1), 0)),
                  pl.BlockSpec((ns, d), lambda i: (0, 0)), row, vec, vec],
        out_specs=[row, row],
        out_shape=[jax.ShapeDtypeStruct((mp + ns, d), F32), jax.ShapeDtypeStruct((mp + ns, d), BF16)],
        compiler_params=_cp(("parallel",)),
        name=name,
    )(xp, xs, o, w.reshape(1, d), b.reshape(1, d))


def _ln_last_kernel(x_ref, o_ref, w_ref, b_ref, yp_ref, ys_ref):
    is_sample = pl.program_id(0) == pl.num_programs(0) - 1
    y = _post_norm(x_ref[...], o_ref[...], w_ref[...], b_ref[...])

    @pl.when(jnp.logical_not(is_sample))
    def _prompt():
        yp_ref[...] = y

    @pl.when(is_sample)
    def _sample():
        ys_ref[...] = y


def _deepnorm_last(x, o, w, b, ns, name):
    m, d = x.shape
    npt = m // ns - 1
    row = pl.BlockSpec((ns, d), lambda i: (i, 0))
    vec = pl.BlockSpec((1, d), lambda i: (0, 0))
    return pl.pallas_call(
        _ln_last_kernel,
        grid=(npt + 1,),
        in_specs=[row, row, vec, vec],
        out_specs=[pl.BlockSpec((ns, d), lambda i: (jnp.minimum(i, npt - 1), 0)),
                   pl.BlockSpec((ns, d), lambda i: (0, 0))],
        out_shape=[jax.ShapeDtypeStruct((npt * ns, d), F32), jax.ShapeDtypeStruct((ns, d), F32)],
        compiler_params=_cp(("arbitrary",)),
        name=name,
    )(x, o, w.reshape(1, d), b.reshape(1, d))


def _ssd_prompt_kernel(z_ref, xs_ref, b_ref, c_ref, dtc_ref, dtr_ref,
                       cwx_ref, cwb_ref, cwc_ref, cbx_ref, cbb_ref, cbc_ref,
                       dtbc_ref, dtbr_ref, alc_ref, alr_ref, dsk_ref, nw_ref, tri_ref, rep64_ref, rep128_ref,
                       wi_ref, wi2_ref, y_ref, s_ref, wo_ref, wo2_ref, bufx, bufb, bufc):
    L = CHUNK
    c = pl.program_id(2)
    wo_ref[...] = wi_ref[...].astype(BF16)
    wo2_ref[...] = wi2_ref[...].astype(BF16)

    @pl.when(c == 0)
    def _init():
        for buf in (bufx, bufb, bufc):
            buf[0:8, :] = jnp.zeros((8, buf.shape[1]), F32)
        s_ref[...] = jnp.zeros(s_ref.shape, F32)

    tri = tri_ref[...]
    li = lax.broadcasted_iota(jnp.int32, (L, L), 0)
    si = lax.broadcasted_iota(jnp.int32, (L, L), 1)
    causal = li >= si
    lo = lax.broadcasted_iota(jnp.int32, (1, LANE), 1) < 64
    pairs = range(SSD_HPG // 2)
    tile = lambda a, p: a[:, p * LANE:(p + 1) * LANE]
    for ci in range(SSD_CPS):
        rows = slice(ci * L, (ci + 1) * L)

        def conv(u_ref, buf, w_ref, bias_ref):
            buf[8:8 + L, :] = u_ref[rows, :]
            acc = bias_ref[...] + buf[5:5 + L, :] * w_ref[0:1, :]
            for k in range(1, SSD_CONV):
                acc = acc + buf[5 + k:5 + k + L, :] * w_ref[k:k + 1, :]
            buf[0:8, :] = buf[L:L + 8, :]
            return _silu(acc)

        xs = conv(xs_ref, bufx, cwx_ref, cbx_ref)
        bm = conv(b_ref, bufb, cwb_ref, cbb_ref)
        cm = conv(c_ref, bufc, cwc_ref, cbc_ref)
        dtc = _softplus(dtc_ref[0, rows, :] + dtbc_ref[0])
        dtr = _softplus(dtr_ref[0, :, rows] + dtbr_ref[0])
        adt_c = dtc * (-jnp.exp(alc_ref[0]))
        adt_r = dtr * (-jnp.exp(alr_ref[0]))
        cum_c = _dot01(tri, adt_c)
        cum_r = sum(lax.dot_general(p, tri, (((1,), (1,)), ((), ())), preferred_element_type=F32)
                    for p in _split(adt_r, 3))
        cb = _dot_nt(cm, bm)
        dt_x = _dot01_r(dtc, rep64_ref[...], 3)
        cum_x = _dot01_r(cum_c, rep64_ref[...], 3)
        cum_b = _dot01_r(cum_c, rep128_ref[...], 3)
        xdt = xs * dt_x
        xdt_tail = xdt * jnp.exp(cum_x[L - 1:L, :] - cum_x)
        s_old = [s_ref[0, r] for r in range(SSD_HPG)]
        decay = [jnp.exp(jnp.where(causal, tile(cum_b, r) - cum_r[r:r + 1, :], -jnp.inf))
                 for r in range(SSD_HPG)]
        x_lo = [jnp.where(lo, tile(xdt, p), 0.0) for p in pairs]
        x_hi = [jnp.where(lo, 0.0, tile(xdt, p)) for p in pairs]
        y_in = [_dot(cb * decay[2 * p], x_lo[p]) + _dot(cb * decay[2 * p + 1], x_hi[p]) for p in pairs]
        y_st = [_dot_nt(cm, jnp.concatenate([s_old[2 * p], s_old[2 * p + 1]], 0)) for p in pairs]
        s_in = [_dot_tn(tile(xdt_tail, p), bm) for p in pairs]
        for r in range(SSD_HPG):
            half = s_in[r // 2][(r % 2) * 64:(r % 2 + 1) * 64]
            s_ref[0, r] = s_old[r] * jnp.exp(cum_c[L - 1:L, r:r + 1]) + half
        y = jnp.concatenate(y_in, axis=1) + jnp.concatenate(y_st, axis=1) * jnp.exp(cum_x) + xs * dsk_ref[...]
        y = y * _silu(z_ref[rows, :])
        y = y * lax.rsqrt(jnp.mean(y * y, -1, keepdims=True) + RMS_EPS) * nw_ref[...]
        y_ref[rows, :] = y.astype(BF16)


def _ssd_params(conv_w, conv_b, dt_bias, a_log, d_skip, norm_w):
    g = SSD_GROUPS
    return dict(
        cwx=conv_w[:, :4096], cwb=conv_w[:, 4096:5120], cwc=conv_w[:, 5120:],
        cbx=conv_b[:4096].reshape(1, -1), cbb=conv_b[4096:5120].reshape(1, -1), cbc=conv_b[5120:].reshape(1, -1),
        dtbc=dt_bias.reshape(g, 1, 8), dtbr=dt_bias.reshape(g, 8, 1),
        alc=a_log.reshape(g, 1, 8), alr=a_log.reshape(g, 8, 1),
        dsk=jnp.repeat(d_skip, SSD_HEAD_DIM).reshape(1, -1), nw=norm_w.reshape(1, -1))


def _ride_specs(w, nsteps, lin):
    r, n = w.shape
    assert r % nsteps == 0 and (r // nsteps) % 16 == 0
    blk = pl.BlockSpec((r // nsteps, n), lambda *idx: (lin(*idx), 0))
    return blk, blk, jax.ShapeDtypeStruct((r, n), BF16)


def _ssd_prompt(proj, dtc, dtr, nb, l, sp, w_ride, w_ride2):
    rows = CHUNK * SSD_CPS
    nc = l // rows
    rb = lambda b, g, c, *_: b * nc + c
    tri = jnp.tril(jnp.ones((CHUNK, CHUNK), BF16))
    lin = lambda b, g, c, *_: (b * SSD_GROUPS + g) * nc + c
    nsteps = nb * SSD_GROUPS * nc
    wi_spec, wo_spec, wo_shape = _ride_specs(w_ride, nsteps, lin)
    wi2_spec, wo2_spec, wo2_shape = _ride_specs(w_ride2, nsteps, lin)
    in_specs = [
        pl.BlockSpec((rows, 512), lambda b, g, c, *_: (rb(b, g, c), _C_Z // 512 + g)),
        pl.BlockSpec((rows, 512), lambda b, g, c, *_: (rb(b, g, c), _C_XS // 512 + g)),
        pl.BlockSpec((rows, 128), lambda b, g, c, *_: (rb(b, g, c), _C_B // 128 + g)),
        pl.BlockSpec((rows, 128), lambda b, g, c, *_: (rb(b, g, c), _C_C // 128 + g)),
        pl.BlockSpec((1, rows, 8), lambda b, g, c, *_: (g, rb(b, g, c), 0)),
        pl.BlockSpec((1, 8, rows), lambda b, g, c, *_: (g, 0, rb(b, g, c))),
        pl.BlockSpec((SSD_CONV, 512), lambda b, g, c, *_: (0, g)),
        pl.BlockSpec((SSD_CONV, 128), lambda b, g, c, *_: (0, g)),
        pl.BlockSpec((SSD_CONV, 128), lambda b, g, c, *_: (0, g)),
        pl.BlockSpec((1, 512), lambda b, g, c, *_: (0, g)),
        pl.BlockSpec((1, 128), lambda b, g, c, *_: (0, g)),
        pl.BlockSpec((1, 128), lambda b, g, c, *_: (0, g)),
        pl.BlockSpec((1, 1, 8), lambda b, g, c, *_: (g, 0, 0)),
        pl.BlockSpec((1, 8, 1), lambda b, g, c, *_: (g, 0, 0)),
        pl.BlockSpec((1, 1, 8), lambda b, g, c, *_: (g, 0, 0)),
        pl.BlockSpec((1, 8, 1), lambda b, g, c, *_: (g, 0, 0)),
        pl.BlockSpec((1, 512), lambda b, g, c, *_: (0, g)),
        pl.BlockSpec((1, 512), lambda b, g, c, *_: (0, g)),
        pl.BlockSpec((CHUNK, CHUNK), lambda b, g, c, *_: (0, 0)),
        pl.BlockSpec((SSD_HPG, SSD_HPG * 64), lambda b, g, c, *_: (0, 0)),
        pl.BlockSpec((SSD_HPG, SSD_HPG * LANE), lambda b, g, c, *_: (0, 0)),
        wi_spec, wi2_spec,
    ]
    rep64 = jnp.asarray(np.kron(np.eye(SSD_HPG), np.ones((1, 64))), BF16)
    rep128 = jnp.asarray(np.kron(np.eye(SSD_HPG), np.ones((1, LANE))), BF16)
    out_specs = [pl.BlockSpec((rows, 512), lambda b, g, c, *_: (rb(b, g, c), g)),
                 pl.BlockSpec((1, SSD_HPG, SSD_HEAD_DIM, SSD_STATE), lambda b, g, c, *_: (b, g, 0, 0)),
                 wo_spec, wo2_spec]
    return pl.pallas_call(
        _ssd_prompt_kernel,
        grid=(nb, SSD_GROUPS, nc),
        in_specs=in_specs,
        out_specs=out_specs,
        out_shape=[jax.ShapeDtypeStruct((proj.shape[0], SSD_WIDTH), BF16),
                   jax.ShapeDtypeStruct((nb, SSD_HEADS, SSD_HEAD_DIM, SSD_STATE), F32),
                   wo_shape, wo2_shape],
        scratch_shapes=[pltpu.VMEM((CHUNK + 8, 512), F32), pltpu.VMEM((CHUNK + 8, 128), F32),
                        pltpu.VMEM((CHUNK + 8, 128), F32)],
        compiler_params=_cp(("parallel", "parallel", "arbitrary")),
        name="ssd_prompt",
    )(proj, proj, proj, proj, dtc, dtr, sp["cwx"], sp["cwb"], sp["cwc"], sp["cbx"], sp["cbb"], sp["cbc"],
      sp["dtbc"], sp["dtbr"], sp["alc"], sp["alr"], sp["dsk"], sp["nw"], tri, rep64, rep128, w_ride, w_ride2)


def _rwkv_mix(rm, km, vm, wdm, adm, w0, wup, a0, aup, k_k, k_a, seg):
    wlog = -_softplus(-(w0 + _dot(jnp.tanh(wdm), wup))) - 0.5
    logw = -jnp.exp(wlog)
    aa = jax.nn.sigmoid(a0 + _dot(adm, aup))
    kkr = km * k_k
    kk = kkr * lax.rsqrt(jnp.maximum(_segsum(kkr * kkr, seg), 1e-24))
    k2 = km * (1.0 + (aa - 1.0) * k_a)
    return logw, kk, k2, kk * aa


def _rwkv_out(o, rm, k2, vm, g, lnw, lnb, rk, seg):
    inv = 1.0 / RWKV_HEAD_DIM
    mean = _segsum(o, seg) * inv
    d = o - mean
    var = _segsum(d * d, seg) * inv
    on = d * lax.rsqrt(var + RWKV_GN_EPS) * lnw + lnb
    bonus = _segsum(rm * k2 * rk, seg) * vm
    return ((on + bonus) * _silu(g)).astype(BF16)


def _wkv_decode(s_ref, r_ref, w_ref, k_ref, b_ref, kk_ref, v_ref, so_ref, y_ref):
    nh, nv = s_ref.shape[0], s_ref.shape[1]
    for hh in range(nh):
        ch = slice(hh * 64, (hh + 1) * 64)
        r, w, k, bv, kk = r_ref[ch, :], w_ref[ch, :], k_ref[ch, :], b_ref[ch, :], kk_ref[ch, :]

        def vrow(vi, carry):
            s = s_ref[hh, vi]
            sk = jnp.sum(s * kk, axis=0, keepdims=True)
            sn = s * w - sk * bv + v_ref[pl.ds(hh * nv + vi, 1), :] * k
            so_ref[hh, vi] = sn
            y_ref[pl.ds(hh * nv + vi, 1), :] = jnp.sum(sn * r, axis=0, keepdims=True)
            return carry

        lax.fori_loop(0, nv, vrow, 0, unroll=min(nv, WKV_UNROLL))


def _rwkv_prompt_kernel(r_ref, k_ref, v_ref, g_ref, wd_ref, ad_ref,
                        mur_ref, muk_ref, muv_ref, muwd_ref, muad_ref,
                        w0_ref, wup_ref, a0_ref, aup_ref, kk_ref, ka_ref, lnw_ref, lnb_ref, rk_ref,
                        seg_ref, tri_ref, wi_ref,
                        ds_ref, dr_ref, dw_ref, dk_ref, db_ref, dkk_ref, dv_ref,
                        y_ref, s_ref, wo_ref, dso_ref, dy_ref, cr, ck, cv, cwd, cad):
    R, C = RW_ROWS, WKV_CHUNK
    wo_ref[...] = wi_ref[...].astype(BF16)
    c = pl.program_id(2)

    @pl.when(c == 0)
    def _init():
        for buf in (cr, ck, cv, cwd, cad):
            buf[...] = jnp.zeros(buf.shape, F32)
        s_ref[...] = jnp.zeros(s_ref.shape, F32)

    row0 = lax.broadcasted_iota(jnp.int32, (R, 1), 0) == 0

    def shift(x_ref, carry, mu_ref):
        x = x_ref[...]
        prev = jnp.where(row0, carry[0:1, :], pltpu.roll(x, 1, 0))
        carry[0:1, :] = x[R - 1:R, :]
        return x + (prev - x) * mu_ref[...]

    rm = shift(r_ref, cr, mur_ref)
    km = shift(k_ref, ck, muk_ref)
    vm = shift(v_ref, cv, muv_ref)
    wdm = shift(wd_ref, cwd, muwd_ref)
    adm = shift(ad_ref, cad, muad_ref)
    seg = seg_ref[...]
    logw, kk, k2, bv = _rwkv_mix(rm, km, vm, wdm, adm, w0_ref[...], wup_ref[...], a0_ref[...], aup_ref[...],
                                 kk_ref[...], ka_ref[...], seg)

    tri = tri_ref[...]
    li = lax.broadcasted_iota(jnp.int32, (C, LANE), 0)
    lane = lax.broadcasted_iota(jnp.int32, (C, LANE), 1)
    si = lane % 64
    strict = li > si
    incl = li >= si
    eye = jnp.where(li == si, 1.0, 0.0)
    lo = lane < 64
    rlo = lax.broadcasted_iota(jnp.int32, (LANE, LANE), 0) < 64
    llo = lax.broadcasted_iota(jnp.int32, (LANE, LANE), 1) < 64
    same = rlo == llo

    def bd(a):
        ab = a.astype(BF16)
        zero = jnp.zeros_like(ab)
        return jnp.concatenate([jnp.where(lo, ab, zero), jnp.where(lo, zero, ab)], axis=0)

    nsc = R // C
    prep = []
    for sc in range(nsc):
        rows = slice(sc * C, (sc + 1) * C)
        lw = logw[rows]
        cs = _dot01(tri, lw)
        cl = cs[C - 1:C, :]
        e_tail = jnp.exp(cl - cs)
        e_neg = jnp.exp(-cs)
        prep.append(dict(
            bt=kk[rows] * jnp.exp(cs - lw),
            bb=bv[rows] * e_neg,
            kt=k2[rows] * e_neg,
            rt=rm[rows] * jnp.exp(cs),
            bh=bv[rows] * e_tail,
            kh=k2[rows] * e_tail,
            pc=jnp.exp(cl), v=vm[rows]))
    npair = HB // 2
    keys = [(sc, p) for sc in range(nsc) for p in range(npair)]
    part = lambda name: {k: prep[k[0]][name][:, k[1] * LANE:(k[1] + 1) * LANE] for k in keys}
    bt, bb, kt, rt, bh, kh, vh, pc = (part(n) for n in ("bt", "bb", "kt", "rt", "bh", "kh", "v", "pc"))
    lhs = {k: jnp.concatenate([bt[k], rt[k]], 0) for k in keys}
    gb = {k: _dot_nt(lhs[k], bd(bb[k])) for k in keys}
    gk = {k: _dot_nt(lhs[k], bd(kt[k])) for k in keys}
    lk = {k: jnp.where(strict, gk[k][0:C], 0.0) for k in keys}
    rb = {k: jnp.where(incl, gb[k][C:2 * C], 0.0) for k in keys}
    rkm = {k: jnp.where(incl, gk[k][C:2 * C], 0.0) for k in keys}
    x = {k: jnp.where(strict, -gb[k][0:C], 0.0) for k in keys}
    t = {k: eye + x[k] for k in keys}
    for _ in range(int(math.log2(C)) - 1):
        x = {k: _dot(x[k], bd(x[k])) for k in keys}
        t = {k: t[k] + _dot(t[k], bd(x[k])) for k in keys}
    bdv = {k: bd(vh[k]) for k in keys}
    lkv = {k: _dot(lk[k], bdv[k]) for k in keys}
    tb = {k: _dot(t[k], bd(bt[k])) for k in keys}
    tlv = {k: _dot(t[k], bd(lkv[k])) for k in keys}
    rq = {k: rt[k] - _dot(rb[k], bd(tb[k])) for k in keys}
    yc = {k: _dot(rkm[k], bdv[k]) - _dot(rb[k], bd(tlv[k])) for k in keys}
    mq = {k: jnp.where(same, _dot_tn(tb[k], bh[k]), 0.0).astype(BF16) for k in keys}
    nf = {k: _dot_tn(jnp.concatenate([vh[k], -tlv[k]], 0), jnp.concatenate([kh[k], bh[k]], 0)) for k in keys}
    vlo = lax.broadcasted_iota(jnp.int32, (64, LANE), 1) < 64
    nn = {k: jnp.where(vlo, nf[k][0:64], nf[k][64:128]) for k in keys}
    st = [jnp.concatenate([s_ref[0, 2 * p], s_ref[0, 2 * p + 1]], axis=1) for p in range(npair)]
    o_chunks = []
    for sc in range(nsc):
        ys = [_dot_nt(rq[sc, p], bd(st[p])) + yc[sc, p] for p in range(npair)]
        st = [st[p] * pc[sc, p] - _dot(st[p], mq[sc, p]) + nn[sc, p] for p in range(npair)]
        o_chunks.append(jnp.concatenate(ys, 1))
    for p in range(npair):
        s_ref[0, 2 * p] = st[p][:, 0:64]
        s_ref[0, 2 * p + 1] = st[p][:, 64:128]
    o = jnp.concatenate(o_chunks, 0)
    y_ref[...] = _rwkv_out(o, rm, k2, vm, g_ref[...], lnw_ref[...], lnb_ref[...], rk_ref[...], seg)
    _wkv_decode(ds_ref, dr_ref, dw_ref, dk_ref, db_ref, dkk_ref, dv_ref, dso_ref, dy_ref)


def _rwkv_params(mu, w0, w_up, a0, a_up, k_k, k_a, r_k, lnx_w, lnx_b):
    v = lambda t: t.reshape(1, -1)
    return dict(
        mur=v(mu[0:4096]), muk=v(mu[4096:8192]), muv=v(mu[8192:12288]),
        muwd=v(mu[12288:12416]), muad=v(mu[12416:12544]),
        w0=v(w0), wup=w_up.astype(BF16), a0=v(a0), aup=a_up.astype(BF16), kk=v(k_k), ka=v(k_a),
        lnw=v(lnx_w), lnb=v(lnx_b), rk=v(r_k),
        seg=jnp.asarray(np.kron(np.eye(LANE // 64), np.ones((64, 64))), BF16))


def _rwkv_prompt(proj, nb, l, rp, w_ride, s_dec, dec_rows, v_dec):
    nr = l // RW_ROWS
    rb = lambda b, h, c: b * nr + c
    w512 = HB * 64
    nhg = RWKV_HEADS // HB
    nsteps = nb * nhg * nr
    lin = lambda b, h, c: (b * nhg + h) * nr + c
    wi_spec, wo_spec, wo_shape = _ride_specs(w_ride, nsteps, lin)
    ns = s_dec.shape[-1]
    vps = RWKV_HEADS * 64 // nsteps
    assert vps * nsteps == RWKV_HEADS * 64 and (vps % 64 == 0 or (64 % vps == 0 and vps % 8 == 0))
    if vps >= 64:
        d_state = pl.BlockSpec((vps // 64, 64, 64, ns), lambda b, h, c: (lin(b, h, c), 0, 0, 0))
        d_head = pl.BlockSpec((vps, ns), lambda b, h, c: (lin(b, h, c), 0))
    else:
        per = 64 // vps
        d_state = pl.BlockSpec((1, vps, 64, ns), lambda b, h, c: (lin(b, h, c) // per, lin(b, h, c) % per, 0, 0))
        d_head = pl.BlockSpec((64, ns), lambda b, h, c: (lin(b, h, c) // per, 0))
    d_vrow = pl.BlockSpec((vps, ns), lambda b, h, c: (lin(b, h, c), 0))
    col = lambda c0: pl.BlockSpec((RW_ROWS, w512), lambda b, h, c: (rb(b, h, c), c0 // w512 + h))
    lora = lambda c0: pl.BlockSpec((RW_ROWS, LORA), lambda b, h, c: (rb(b, h, c), c0 // LORA))
    vec = pl.BlockSpec((1, w512), lambda b, h, c: (0, h))
    vec128 = pl.BlockSpec((1, LORA), lambda b, h, c: (0, 0))
    up = pl.BlockSpec((LORA, w512), lambda b, h, c: (0, h))
    tri = jnp.tril(jnp.ones((WKV_CHUNK, WKV_CHUNK), BF16))
    gate = pl.BlockSpec((pl.Element(RW_ROWS), pl.Element(w512)),
                        lambda b, h, c: (rb(b, h, c) * RW_ROWS, pl.multiple_of(_C_G + h * w512, LANE)))
    in_specs = [col(_C_R), col(_C_K), col(_C_V), gate, lora(_C_WD), lora(_C_AD),
                vec, vec, vec, vec128, vec128,
                vec, up, vec, up, vec, vec, vec, vec, vec,
                pl.BlockSpec((LANE, LANE), lambda b, h, c: (0, 0)),
                pl.BlockSpec((WKV_CHUNK, WKV_CHUNK), lambda b, h, c: (0, 0)),
                wi_spec,
                d_state, d_head, d_head, d_head, d_head, d_head, d_vrow]
    out_specs = [pl.BlockSpec((RW_ROWS, w512), lambda b, h, c: (rb(b, h, c), h)),
                 pl.BlockSpec((1, HB, 64, 64), lambda b, h, c: (b, h, 0, 0)),
                 wo_spec, d_state, d_vrow]
    return pl.pallas_call(
        _rwkv_prompt_kernel,
        grid=(nb, RWKV_HEADS // HB, nr),
        in_specs=in_specs,
        out_specs=out_specs,
        out_shape=[jax.ShapeDtypeStruct((proj.shape[0], RWKV_WIDTH), BF16),
                   jax.ShapeDtypeStruct((nb, RWKV_HEADS, 64, 64), F32),
                   wo_shape,
                   jax.ShapeDtypeStruct(s_dec.shape, F32),
                   jax.ShapeDtypeStruct((RWKV_WIDTH, ns), F32)],
        scratch_shapes=[pltpu.VMEM((8, w512), F32)] * 3 + [pltpu.VMEM((8, LORA), F32)] * 2,
        compiler_params=_cp(("parallel", "parallel", "arbitrary")),
        name="rwkv_prompt",
    )(proj, proj, proj, proj, proj, proj,
      rp["mur"], rp["muk"], rp["muv"], rp["muwd"], rp["muad"],
      rp["w0"], rp["wup"], rp["a0"], rp["aup"], rp["kk"], rp["ka"], rp["lnw"], rp["lnb"], rp["rk"],
      rp["seg"], tri, w_ride, s_dec, *dec_rows, v_dec)


def _trig_kernel(pos_ref, freq_ref, cos_ref, sin_ref):
    ang = pos_ref[...] * freq_ref[...]
    cos_ref[...] = jnp.cos(ang)
    sin_ref[...] = jnp.sin(ang)


def _trig(pos):
    n = pos.shape[0]
    half = RET_QK_DIM // 2
    freq = (ROPE_BASE ** (-jnp.arange(half, dtype=F32) / half)).reshape(1, half)
    posb = jnp.broadcast_to(pos.astype(F32)[:, None], (n, half))
    tn = _pick_tile(n, (256, 128, 8))
    blk = pl.BlockSpec((tn, half), lambda i: (i, 0))
    return pl.pallas_call(
        _trig_kernel, grid=(n // tn,),
        in_specs=[blk, pl.BlockSpec((1, half), lambda i: (0, 0))],
        out_specs=[blk, blk],
        out_shape=[jax.ShapeDtypeStruct((n, half), F32)] * 2,
        name="rope_tables",
    )(posb, freq)


def _rotate(x, cos, sin):
    x1, x2 = x[:, :128], x[:, 128:]
    return jnp.concatenate([x1 * cos - x2 * sin, x1 * sin + x2 * cos], 1)


def _ret_decode_tile(s, gd, qrow, krow, vrow):
    lhs, rhs = _outer_rows(krow, vrow)
    sn = s * gd + _dot_tn(lhs, rhs)
    y8 = _dot(jnp.broadcast_to(qrow, (8, RET_QK_DIM)), sn)
    return sn, y8[0:1, :]


def _ret_prompt_kernel(lg_ref, gd_ref, q_ref, k_ref, v_ref, g_ref, cos_ref, sin_ref, gnw_ref,
                       ds_ref, dq_ref, dk_ref, dv_ref,
                       y_ref, s_ref, dso_ref, dy_ref):
    L = CHUNK
    hg = pl.program_id(1)
    c = pl.program_id(2)

    @pl.when(c == 0)
    def _init():
        s_ref[...] = jnp.zeros(s_ref.shape, F32)

    cos = cos_ref[...]
    sin = sin_ref[...]
    li = lax.broadcasted_iota(jnp.int32, (L, L), 0)
    si = lax.broadcasted_iota(jnp.int32, (L, L), 1)
    rel = (li - si).astype(F32)
    causal = li >= si
    icol = lax.broadcasted_iota(jnp.int32, (L, 1), 0).astype(F32)
    heads = range(RET_HB)
    lg = [lg_ref[hg * RET_HB + j] for j in heads]
    part = lambda ref, j: ref[:, j * 256:(j + 1) * 256].astype(F32)
    qr = [(_rotate(part(q_ref, j), cos, sin) * (RET_QK_DIM ** -0.5)).astype(BF16) for j in heads]
    kr = [_rotate(part(k_ref, j), cos, sin) for j in heads]
    v = [v_ref[:, j * 512:(j + 1) * 512].astype(BF16) for j in heads]
    s0 = [s_ref[0, j] for j in heads]
    qk = [_dot_nt(qr[j], kr[j]) for j in heads]
    y_st = [_dot(qr[j], s0[j]) for j in heads]
    s_in = [_dot_tn(kr[j] * jnp.exp((L - 1.0 - icol) * lg[j]), v[j]) for j in heads]
    sc = [qk[j] * jnp.exp(jnp.where(causal, rel * lg[j], -jnp.inf)) for j in heads]
    y_in = [_dot(sc[j], v[j]) for j in heads]
    outs = []
    for j in heads:
        s_ref[0, j] = s0[j] * jnp.exp(L * lg[j]) + s_in[j]
        y = y_in[j] + y_st[j] * jnp.exp((icol + 1.0) * lg[j])
        mu = jnp.mean(y, -1, keepdims=True)
        d = y - mu
        var = jnp.mean(d * d, -1, keepdims=True)
        outs.append(d * lax.rsqrt(var + RET_GN_EPS))
    o = jnp.concatenate(outs, 1) * gnw_ref[...]
    y_ref[...] = (o * _silu(g_ref[...].astype(F32))).astype(BF16)

    spp, hpb = ds_ref.shape[0], ds_ref.shape[1]
    lin = (pl.program_id(0) * pl.num_programs(1) + hg) * pl.num_programs(2) + c
    h0 = (lin % (RET_HEADS // hpb)) * hpb
    for j in range(spp):
        for hh in range(hpb):
            row = lambda ref: ref[j, hh:hh + 1, :]
            sn, yrow = _ret_decode_tile(ds_ref[j, hh], gd_ref[h0 + hh], row(dq_ref), row(dk_ref), row(dv_ref))
            dso_ref[j, hh] = sn
            dy_ref[j, hh:hh + 1, :] = yrow


def _ret_log_g():
    return jnp.log1p(-jnp.exp2(-5.0 - jnp.arange(RET_HEADS, dtype=F32)))


def _ret_prompt(proj, cos, sin, gn_w, nb, l, s_dec, q_dec, k_dec, v_dec):
    nc = l // CHUNK
    wq, wv = RET_HB * RET_QK_DIM, RET_HB * RET_V_DIM
    nhg = RET_HEADS // RET_HB
    nsteps = nb * nhg * nc
    ns = s_dec.shape[0]
    lin = lambda b, h, c, *_: (b * nhg + h) * nc + c
    rb = lambda b, h, c, *_: b * nc + c
    spp, hpb = (ns // nsteps, RET_HEADS) if nsteps <= ns else (1, RET_HEADS * ns // nsteps)
    parts = RET_HEADS // hpb
    assert spp * hpb * nsteps == ns * RET_HEADS and hpb % 8 == 0
    dec = lambda *last: pl.BlockSpec((spp, hpb) + last,
                                     lambda b, h, c, *_: (lin(b, h, c) // parts, lin(b, h, c) % parts) + (0,) * len(last))
    grid_spec = pltpu.PrefetchScalarGridSpec(
        num_scalar_prefetch=2,
        grid=(nb, nhg, nc),
        in_specs=[
            pl.BlockSpec((CHUNK, wq), lambda b, h, c, *_: (rb(b, h, c), h)),
            pl.BlockSpec((CHUNK, wq), lambda b, h, c, *_: (rb(b, h, c), RET_QK_WIDTH // wq + h)),
            pl.BlockSpec((CHUNK, wv), lambda b, h, c, *_: (rb(b, h, c), 2 * RET_QK_WIDTH // wv + h)),
            pl.BlockSpec((CHUNK, wv), lambda b, h, c, *_: (rb(b, h, c), (2 * RET_QK_WIDTH + RET_WIDTH) // wv + h)),
            pl.BlockSpec((CHUNK, 128), lambda b, h, c, *_: (c, 0)),
            pl.BlockSpec((CHUNK, 128), lambda b, h, c, *_: (c, 0)),
            pl.BlockSpec((1, wv), lambda b, h, c, *_: (0, h)),
            dec(RET_QK_DIM, RET_V_DIM), dec(RET_QK_DIM), dec(RET_QK_DIM), dec(RET_V_DIM),
        ],
        out_specs=[pl.BlockSpec((CHUNK, wv), lambda b, h, c, *_: (rb(b, h, c), h)),
                   pl.BlockSpec((1, RET_HB, RET_QK_DIM, RET_V_DIM), lambda b, h, c, *_: (b, h, 0, 0)),
                   dec(RET_QK_DIM, RET_V_DIM), dec(RET_V_DIM)],
    )
    log_g = _ret_log_g()
    return pl.pallas_call(
        _ret_prompt_kernel,
        grid_spec=grid_spec,
        out_shape=[jax.ShapeDtypeStruct((proj.shape[0], RET_WIDTH), BF16),
                   jax.ShapeDtypeStruct((nb, RET_HEADS, RET_QK_DIM, RET_V_DIM), F32),
                   jax.ShapeDtypeStruct(s_dec.shape, F32),
                   jax.ShapeDtypeStruct((ns, RET_HEADS, RET_V_DIM), F32)],
        compiler_params=_cp(("parallel", "parallel", "arbitrary")),
        name="ret_prompt",
    )(log_g, jnp.exp(log_g), proj, proj, proj, proj, cos, sin, gn_w.reshape(1, -1),
      s_dec, q_dec, k_dec, v_dec)


def _ssd_pre_kernel(xs_ref, b_ref, c_ref, csx_ref, csb_ref, csc_ref, dtc_ref,
                    cwx_ref, cwb_ref, cwc_ref, cbx_ref, cbb_ref, cbc_ref, dtb_ref, al_ref,
                    xa_ref, ba_ref, ca_ref, dt_ref, dec_ref):
    def conv(u_ref, cs_ref, w_ref, bias_ref):
        acc = bias_ref[...] + u_ref[...] * w_ref[SSD_CONV - 1:SSD_CONV, :]
        for k in range(SSD_CONV - 1):
            acc = acc + cs_ref[k] * w_ref[k:k + 1, :]
        return _silu(acc)

    xs = conv(xs_ref, csx_ref, cwx_ref, cbx_ref)
    xa_ref[...] = xs
    ba_ref[0] = conv(b_ref, csb_ref, cwb_ref, cbb_ref)
    ca_ref[0] = conv(c_ref, csc_ref, cwc_ref, cbc_ref)
    dt = _softplus(dtc_ref[0] + dtb_ref[0])
    dt_ref[0] = dt
    dec_ref[0] = jnp.exp(dt * (-jnp.exp(al_ref[0])))


def _ssd_pre(proj, rb0, ns, cs_t, dtc, sp):
    g8 = SSD_GROUPS
    in_specs = [
        pl.BlockSpec((ns, 512), lambda g: (rb0, _C_XS // 512 + g)),
        pl.BlockSpec((ns, 128), lambda g: (rb0, _C_B // 128 + g)),
        pl.BlockSpec((ns, 128), lambda g: (rb0, _C_C // 128 + g)),
        pl.BlockSpec((3, ns, 512), lambda g: (0, 0, g)),
        pl.BlockSpec((3, ns, 128), lambda g: (0, 0, 4096 // 128 + g)),
        pl.BlockSpec((3, ns, 128), lambda g: (0, 0, 5120 // 128 + g)),
        pl.BlockSpec((1, ns, 8), lambda g: (g, rb0, 0)),
        pl.BlockSpec((SSD_CONV, 512), lambda g: (0, g)),
        pl.BlockSpec((SSD_CONV, 128), lambda g: (0, g)),
        pl.BlockSpec((SSD_CONV, 128), lambda g: (0, g)),
        pl.BlockSpec((1, 512), lambda g: (0, g)),
        pl.BlockSpec((1, 128), lambda g: (0, g)),
        pl.BlockSpec((1, 128), lambda g: (0, g)),
        pl.BlockSpec((1, 1, 8), lambda g: (g, 0, 0)),
        pl.BlockSpec((1, 1, 8), lambda g: (g, 0, 0)),
    ]
    out_specs = [
        pl.BlockSpec((ns, 512), lambda g: (0, g)),
        pl.BlockSpec((1, ns, 128), lambda g: (g, 0, 0)),
        pl.BlockSpec((1, ns, 128), lambda g: (g, 0, 0)),
        pl.BlockSpec((1, ns, 8), lambda g: (g, 0, 0)),
        pl.BlockSpec((1, ns, 8), lambda g: (g, 0, 0)),
    ]
    out_shape = [
        jax.ShapeDtypeStruct((ns, SSD_WIDTH), F32),
        jax.ShapeDtypeStruct((g8, ns, SSD_STATE), F32),
        jax.ShapeDtypeStruct((g8, ns, SSD_STATE), F32),
        jax.ShapeDtypeStruct((g8, ns, 8), F32),
        jax.ShapeDtypeStruct((g8, ns, 8), F32),
    ]
    return pl.pallas_call(
        _ssd_pre_kernel, grid=(g8,), in_specs=in_specs, out_specs=out_specs, out_shape=out_shape,
        compiler_params=_cp(("parallel",)), name="ssd_sample_pre",
    )(proj, proj, proj, cs_t, cs_t, cs_t, dtc, sp["cwx"], sp["cwb"], sp["cwc"], sp["cbx"], sp["cbb"], sp["cbc"],
      sp["dtbc"], sp["alc"])


def _outer_rows(x, y):
    hi = lambda t: t.astype(BF16).astype(F32)
    xh, yh = hi(x), hi(y)
    rx = lax.broadcasted_iota(jnp.int32, (8, x.shape[1]), 0)
    ry = lax.broadcasted_iota(jnp.int32, (8, y.shape[1]), 0)
    lhs = jnp.where(rx == 1, x - xh, jnp.where((rx == 0) | (rx == 2), xh, 0.0))
    rhs = jnp.where(ry == 2, y - yh, jnp.where(ry < 2, yh, 0.0))
    return lhs.astype(BF16), rhs.astype(BF16)


def _ssm_decode(dt_ref, dec_ref, s_ref, x_ref, b_ref, c_ref, so_ref, y_ref, b0, h0):
    hw = SSD_HPG * SSD_HEAD_DIM
    for j in range(s_ref.shape[0]):
        for gi in range(s_ref.shape[1] // SSD_HPG):
            g = h0 // SSD_HPG + gi
            lhs, rhs = _outer_rows(x_ref[j, :, gi * hw:(gi + 1) * hw], b_ref[j, pl.ds(g, 1), :])
            xb = _dot_tn(lhs, rhs)
            new = []
            for r in range(SSD_HPG):
                hh = gi * SSD_HPG + r
                idx = (b0 + j) * SSD_HEADS + h0 + hh
                sn = s_ref[j, hh] * dec_ref[idx] + xb[r * 64:(r + 1) * 64] * dt_ref[idx]
                so_ref[j, hh] = sn
                new.append(sn)
            crow = jnp.broadcast_to(c_ref[j, pl.ds(g, 1), :], (8, SSD_STATE))
            y_ref[j, gi] = _dot_nt(crow, jnp.concatenate(new, 0))[0:1, :]


def _ssm_state_kernel(dt_ref, dec_ref, s_ref, x_ref, b_ref, c_ref, so_ref, y_ref):
    _ssm_decode(dt_ref, dec_ref, s_ref, x_ref, b_ref, c_ref, so_ref, y_ref, pl.program_id(0) * SSM_BT, 0)


def _ssm_state(dt, dec, s, xa, ba, ca):
    ns = s.shape[0]
    bt = SSM_BT
    smem = pl.BlockSpec(memory_space=pltpu.SMEM)
    sblk = pl.BlockSpec((bt, SSD_HEADS, SSD_HEAD_DIM, SSD_STATE), lambda i: (i, 0, 0, 0))
    bc = pl.BlockSpec((bt, SSD_GROUPS, SSD_STATE), lambda i: (i, 0, 0))
    return pl.pallas_call(
        _ssm_state_kernel, grid=(ns // bt,),
        in_specs=[smem, smem, sblk, pl.BlockSpec((bt, 1, SSD_WIDTH), lambda i: (i, 0, 0)), bc, bc],
        out_specs=[sblk, pl.BlockSpec((bt, SSD_GROUPS, 1, 512), lambda i: (i, 0, 0, 0))],
        out_shape=[jax.ShapeDtypeStruct(s.shape, F32), jax.ShapeDtypeStruct((ns, SSD_GROUPS, 1, 512), F32)],
        compiler_params=_cp(("parallel",)), name="ssm_sample_state",
    )(dt, dec, s, xa, ba, ca)


def _ssd_post_kernel(y_ref, xa_ref, z_ref, dsk_ref, nw_ref, dst_ref, o_ref):
    del dst_ref
    y = (y_ref[...] + xa_ref[...] * dsk_ref[...]) * _silu(z_ref[...])
    y = y * lax.rsqrt(jnp.mean(y * y, -1, keepdims=True) + RMS_EPS) * nw_ref[...]
    o_ref[...] = y.astype(BF16)


def _ssd_post(y, xa, proj, rb0, sp, dst):
    ns = y.shape[0]
    blk = pl.BlockSpec((ns, 512), lambda g: (0, g))
    vec = pl.BlockSpec((1, 512), lambda g: (0, g))
    return pl.pallas_call(
        _ssd_post_kernel, grid=(SSD_GROUPS,),
        in_specs=[blk, blk, pl.BlockSpec((ns, 512), lambda g: (rb0, _C_Z // 512 + g)), vec, vec,
                  pl.BlockSpec(memory_space=pl.ANY)],
        out_specs=pl.BlockSpec((ns, 512), lambda g: (rb0, g)),
        out_shape=jax.ShapeDtypeStruct(dst.shape, BF16),
        input_output_aliases={5: 0},
        compiler_params=_cp(("parallel",)), name="ssd_sample_post",
    )(y, xa, proj, sp["dsk"], sp["nw"], dst)


def _wkv_pre_kernel(r_ref, k_ref, v_ref, wd_ref, ad_ref, sr_ref, sk_ref, sv_ref, swd_ref, sad_ref,
                    mur_ref, muk_ref, muv_ref, muwd_ref, muad_ref,
                    w0_ref, wup_ref, a0_ref, aup_ref, kk_ref, ka_ref, seg_ref,
                    ro_ref, ko_ref, vo_ref, rt_ref, wt_ref, kt_ref, bt_ref, kkt_ref, vt_ref):
    mix = lambda x_ref, s_ref, mu_ref: x_ref[...] + (s_ref[...] - x_ref[...]) * mu_ref[...]
    rm = mix(r_ref, sr_ref, mur_ref)
    km = mix(k_ref, sk_ref, muk_ref)
    vm = mix(v_ref, sv_ref, muv_ref)
    wdm = mix(wd_ref, swd_ref, muwd_ref)
    adm = mix(ad_ref, sad_ref, muad_ref)
    logw, kk, k2, bv = _rwkv_mix(rm, km, vm, wdm, adm, w0_ref[...], wup_ref[...], a0_ref[...], aup_ref[...],
                                 kk_ref[...], ka_ref[...], seg_ref[...])
    ro_ref[...] = rm
    ko_ref[...] = k2
    vo_ref[...] = vm
    rt_ref[...] = rm.T
    wt_ref[...] = jnp.exp(logw).T
    kt_ref[...] = k2.T
    bt_ref[...] = bv.T
    kkt_ref[...] = kk.T
    vt_ref[...] = vm.T


def _wkv_pre(proj, rb0, shift, rp):
    ns = shift.shape[0]
    w512 = HB * 64
    col = lambda c0: pl.BlockSpec((ns, w512), lambda h: (rb0, c0 // w512 + h))
    lora = lambda c0: pl.BlockSpec((ns, LORA), lambda h: (rb0, c0 // LORA))
    scol = lambda c0: pl.BlockSpec((ns, w512), lambda h: (0, c0 // w512 + h))
    slora = lambda c0: pl.BlockSpec((ns, LORA), lambda h: (0, c0 // LORA))
    vec = pl.BlockSpec((1, w512), lambda h: (0, h))
    vec128 = pl.BlockSpec((1, LORA), lambda h: (0, 0))
    up = pl.BlockSpec((LORA, w512), lambda h: (0, h))
    row = pl.BlockSpec((ns, w512), lambda h: (0, h))
    tr = pl.BlockSpec((w512, ns), lambda h: (h, 0))
    return pl.pallas_call(
        _wkv_pre_kernel, grid=(RWKV_HEADS // HB,),
        in_specs=[col(_C_R), col(_C_K), col(_C_V), lora(_C_WD), lora(_C_AD),
                  scol(0), scol(4096), scol(8192), slora(12288), slora(12416),
                  vec, vec, vec, vec128, vec128, vec, up, vec, up, vec, vec,
                  pl.BlockSpec((LANE, LANE), lambda h: (0, 0))],
        out_specs=[row] * 3 + [tr] * 6,
        out_shape=[jax.ShapeDtypeStruct((ns, RWKV_WIDTH), F32)] * 3
        + [jax.ShapeDtypeStruct((RWKV_WIDTH, ns), F32)] * 6,
        compiler_params=_cp(("parallel",)), name="wkv_sample_pre",
    )(proj, proj, proj, proj, proj, shift, shift, shift, shift, shift,
      rp["mur"], rp["muk"], rp["muv"], rp["muwd"], rp["muad"],
      rp["w0"], rp["wup"], rp["a0"], rp["aup"], rp["kk"], rp["ka"], rp["seg"])


def _wkv_post_kernel(o_ref, r_ref, k_ref, v_ref, g_ref, lnw_ref, lnb_ref, rk_ref, seg_ref, dst_ref, y_ref):
    del dst_ref
    y_ref[...] = _rwkv_out(o_ref[...].T, r_ref[...], k_ref[...], v_ref[...], g_ref[...],
                           lnw_ref[...], lnb_ref[...], rk_ref[...], seg_ref[...])


def _wkv_post(o_t, r, k2, v, proj, rb0, rp, dst):
    ns = o_t.shape[1]
    w512 = HB * 64
    row = pl.BlockSpec((ns, w512), lambda h: (0, h))
    vec = pl.BlockSpec((1, w512), lambda h: (0, h))
    return pl.pallas_call(
        _wkv_post_kernel, grid=(RWKV_HEADS // HB,),
        in_specs=[pl.BlockSpec((w512, ns), lambda h: (h, 0)), row, row, row,
                  pl.BlockSpec((pl.Element(ns), pl.Element(w512)), lambda h: (rb0 * ns, pl.multiple_of(_C_G + h * w512, LANE))),
                  vec, vec, vec, pl.BlockSpec((LANE, LANE), lambda h: (0, 0)),
                  pl.BlockSpec(memory_space=pl.ANY)],
        out_specs=pl.BlockSpec((ns, w512), lambda h: (rb0, h)),
        out_shape=jax.ShapeDtypeStruct(dst.shape, BF16),
        input_output_aliases={9: 0},
        compiler_params=_cp(("parallel",)), name="wkv_sample_post",
    )(o_t, r, k2, v, proj, rp["lnw"], rp["lnb"], rp["rk"], rp["seg"], dst)


def _ret_pre_kernel(q_ref, k_ref, cos_ref, sin_ref, qo_ref, ko_ref):
    cos = cos_ref[0:1, :]
    sin = sin_ref[0:1, :]
    qo_ref[...] = _rotate(q_ref[...].astype(F32), cos, sin) * (RET_QK_DIM ** -0.5)
    ko_ref[...] = _rotate(k_ref[...].astype(F32), cos, sin)


def _ret_pre(proj, rb0, ns, cos, sin):
    blk = pl.BlockSpec((ns, RET_QK_DIM), lambda h: (0, h))
    return pl.pallas_call(
        _ret_pre_kernel, grid=(RET_HEADS,),
        in_specs=[pl.BlockSpec((ns, RET_QK_DIM), lambda h: (rb0, h)),
                  pl.BlockSpec((ns, RET_QK_DIM), lambda h: (rb0, RET_HEADS + h)),
                  pl.BlockSpec((8, 128), lambda h: (0, 0)), pl.BlockSpec((8, 128), lambda h: (0, 0))],
        out_specs=[blk, blk],
        out_shape=[jax.ShapeDtypeStruct((ns, RET_QK_WIDTH), F32)] * 2,
        compiler_params=_cp(("parallel",)), name="ret_sample_pre",
    )(proj, proj, cos, sin)


def _ret_post_kernel(y_ref, g_ref, gnw_ref, dst_ref, o_ref):
    del dst_ref
    y = y_ref[...]
    mu = jnp.mean(y, -1, keepdims=True)
    d = y - mu
    var = jnp.mean(d * d, -1, keepdims=True)
    o = d * lax.rsqrt(var + RET_GN_EPS) * gnw_ref[...]
    o_ref[...] = (o * _silu(g_ref[...].astype(F32))).astype(BF16)


def _ret_post(y, proj, rb0, gn_w, dst):
    ns = y.shape[0]
    blk = pl.BlockSpec((ns, RET_V_DIM), lambda h: (0, h))
    return pl.pallas_call(
        _ret_post_kernel, grid=(RET_HEADS,),
        in_specs=[blk, pl.BlockSpec((ns, RET_V_DIM), lambda h: (rb0, (2 * RET_QK_WIDTH + RET_WIDTH) // RET_V_DIM + h)),
                  pl.BlockSpec((1, RET_V_DIM), lambda h: (0, h)), pl.BlockSpec(memory_space=pl.ANY)],
        out_specs=pl.BlockSpec((ns, RET_V_DIM), lambda h: (rb0, h)),
        out_shape=jax.ShapeDtypeStruct(dst.shape, BF16),
        input_output_aliases={3: 0},
        compiler_params=_cp(("parallel",)), name="ret_sample_post",
    )(y, proj, gn_w.reshape(1, -1), dst)


def _ab_layer(xp, xs, xb, nb, l, ns, conv_s, ssm_s, shift_s, wkv_s, w_in, sp, rp, w_out, ln_w, ln_b,
              w_next_in, w_next_out):
    mp = nb * l
    rb0 = mp // 128
    wt = w_in.T
    wt_ssd = wt[:AB_SSD_W + LANE].astype(BF16)
    tiles = lambda w: [j * AB_TN for j in range(w // AB_TN)]
    proj_a, wt_rwkv = _matmul_wt(xb, wt_ssd, tiles(AB_SSD_W), AB_TN, "ab_in_proj_ssd",
                                 ride=(wt, AB_SSD_W + SSD_HEADS, AB_RWKV_W))
    pdt = _matmul_wt(xb, wt_ssd, [AB_SSD_W], LANE, "ab_dt_proj")[:, :SSD_HEADS]
    m = proj_a.shape[0]
    dt3 = pdt.reshape(m, SSD_GROUPS, SSD_HPG)
    dtc = dt3.transpose(1, 0, 2)
    dtr = dt3.transpose(1, 2, 0)
    xa, ba, ca, dt_s, dec_s = _ssd_pre(proj_a, rb0, ns, conv_s.transpose(1, 0, 2), dtc, sp)
    flat = lambda t: t.transpose(1, 0, 2).reshape(ns * SSD_HEADS)
    proj_b, ssm_n, y_s = _matmul_wt_ssm(xb, wt_rwkv, tiles(AB_RWKV_W), AB_TN, "ab_in_proj_rwkv",
                                        flat(dt_s), flat(dec_s), ssm_s, xa.reshape(ns, 1, SSD_WIDTH),
                                        ba.transpose(1, 0, 2), ca.transpose(1, 0, 2))

    ya, ssm_p, w_out_b, w_next_out_b = _ssd_prompt(proj_a, dtc, dtr, nb, l, sp, w_out, w_next_out)
    r_s, k_s, v_s, r_t, w_t, k_t, b_t, kk_t, v_t = _wkv_pre(proj_b, rb0, shift_s.reshape(ns, SHIFT_DIM), rp)
    yb, wkv_p, w_next_in_b, wkv_t, o_t = _rwkv_prompt(proj_b, nb, l, rp, w_next_in, wkv_s.transpose(1, 2, 3, 0),
                                                      (r_t, w_t, k_t, b_t, kk_t), v_t)
    wkv_n = wkv_t.transpose(3, 0, 1, 2)
    tail = lambda p, n, c0, c1: jnp.stack([p[(b + 1) * l - n:(b + 1) * l, c0:c1] for b in range(nb)])
    conv_p = tail(proj_a, SSD_CONV - 1, _C_XS, AB_SSD_W)
    shift_p = tail(proj_b, 1, _C_R, _C_G)

    ya = _ssd_post(y_s.reshape(ns, SSD_WIDTH), xa, proj_a, rb0, sp, ya)
    conv_n = jnp.concatenate([conv_s[:, 1:], proj_a[mp:, None, _C_XS:AB_SSD_W]], axis=1)

    yb = _wkv_post(o_t, r_s, k_s, v_s, proj_b, rb0, rp, yb)
    shift_n = proj_b[mp:, None, _C_R:_C_G]

    out = _matmul2(ya, yb, w_out_b, "ab_out_proj")
    x_new, xb_new = _deepnorm_first(xp, xs, out, ln_w, ln_b, "ab_deepnorm")
    return (x_new, xb_new, (conv_p, ssm_p, shift_p, wkv_p), (conv_n, ssm_n, shift_n, wkv_n),
            w_next_in_b, w_next_out_b)


def _ret_layer(x, xb, nb, l, ns, ret_s, w_in, gn_w, w_out, ln_w, ln_b):
    mp = nb * l
    rb0 = mp // 128
    proj = _matmul(xb, w_in, "ret_in_proj", BF16)
    cos, sin = _trig(jnp.arange(l))
    cos_s, sin_s = _trig(jnp.full((8,), PAST_LEN))
    q_s, k_s = _ret_pre(proj, rb0, ns, cos_s, sin_s)
    v_s = proj[mp:, 2 * RET_QK_WIDTH:2 * RET_QK_WIDTH + RET_WIDTH].astype(F32).reshape(ns, RET_HEADS, RET_V_DIM)
    h3 = lambda t: t.reshape(ns, RET_HEADS, RET_QK_DIM)
    y, ret_p, ret_n, o_s = _ret_prompt(proj, cos, sin, gn_w, nb, l, ret_s, h3(q_s), h3(k_s), v_s)
    y = _ret_post(o_s.reshape(ns, RET_WIDTH), proj, rb0, gn_w, y)

    out = _matmul(y, w_out, "ret_out_proj")
    y_p, y_s = _deepnorm_last(x, out, ln_w, ln_b, ns, "ret_deepnorm")
    return y_p, y_s, ret_p, ret_n


def kernel(x_prompt, x_sample, state_conv, state_ssm, state_shift, state_wkv, state_ret, ab_w_in, ssd_conv_w, ssd_conv_b, ssd_dt_bias, ssd_a_log, ssd_d, ssd_norm_w, rwkv_mu, rwkv_w0, rwkv_w_up, rwkv_a0, rwkv_a_up, rwkv_k_k, rwkv_k_a, rwkv_r_k, rwkv_lnx_w, rwkv_lnx_b, ab_w_out, ab_ln_w, ab_ln_b, ret_w_in, ret_gn_w, ret_w_out, ret_ln_w, ret_ln_b):
    nb, l, d = x_prompt.shape
    ns = x_sample.shape[0]
    assert x_sample.shape[1] == 1 and l % CHUNK == 0 and ns % LANE == 0 and ns == LANE
    mp = nb * l
    xp, xs = x_prompt.reshape(mp, d), x_sample.reshape(ns, d)
    xb = jnp.concatenate([xp.astype(BF16), xs.astype(BF16)], axis=0)

    sp = _ssd_params(ssd_conv_w[0], ssd_conv_b[0], ssd_dt_bias[0], ssd_a_log[0], ssd_d[0], ssd_norm_w[0])
    rp = _rwkv_params(rwkv_mu[0], rwkv_w0[0], rwkv_w_up[0], rwkv_a0[0], rwkv_a_up[0], rwkv_k_k[0], rwkv_k_a[0],
                      rwkv_r_k[0], rwkv_lnx_w[0], rwkv_lnx_b[0])
    x, xb, pst, sst, ret_w_in_b, ret_w_out_b = _ab_layer(
        xp, xs, xb, nb, l, ns, state_conv[0], state_ssm[0], state_shift[0], state_wkv[0],
        ab_w_in[0], sp, rp, ab_w_out[0], ab_ln_w[0], ab_ln_b[0], ret_w_in[0], ret_w_out[0])
    y_p, y_s, ret_p, ret_n = _ret_layer(x, xb, nb, l, ns, state_ret[0], ret_w_in_b, ret_gn_w[0], ret_w_out_b,
                                        ret_ln_w[0], ret_ln_b[0])
    y_prompt = y_p.reshape(nb, l, d)
    y_sample = y_s.reshape(ns, 1, d)
    st = lambda t: t[None]
    return (y_prompt, y_sample,
            st(pst[0]), st(pst[1]), st(pst[2]), st(pst[3]), st(ret_p),
            st(sst[0]), st(sst[1]), st(sst[2]), st(sst[3]), st(ret_n))
```
